```python
import math, functools
import jax, jax.numpy as jnp
from jax import lax
import numpy as np

D_MODEL = 1024
BATCH = 8
SEQ = 4096
DEPTH = 1

CHUNK = 64
DN_HEADS = 8
DN_HEAD_DIM = 64
DN_WIDTH = DN_HEADS * DN_HEAD_DIM
CONV_WIDTH = 4
POOL_GROUPS = 4
POOL_WINDOWS = (2, 4, 8, 16)
POOL_WIDTH = D_MODEL // 2
POOL_GROUP_DIM = POOL_WIDTH // POOL_GROUPS
N_BRANCH = 2
IN_WIDTH = 3 * DN_WIDTH + DN_WIDTH + 2 * DN_HEADS + POOL_WIDTH + N_BRANCH * D_MODEL
N_EXPERTS = 32
TOP_K = 4
D_FF = D_MODEL
SWIGLU_LIMIT = 7.0
SWIGLU_ALPHA = 1.702
NORM_EPS = 1e-6

kernel_name = "hybrid_deltanet_pool_moe_block"


def rms_norm(x, g):
    xf = x.astype(jnp.float32)
    y = xf * lax.rsqrt(jnp.mean(xf * xf, axis=-1, keepdims=True) + NORM_EPS)
    return (y * g.astype(jnp.float32)).astype(x.dtype)


def l2_normalize(t):
    return t * lax.rsqrt(jnp.sum(t * t, axis=-1, keepdims=True) + NORM_EPS)


def causal_depthwise_conv(u, w):
    K = w.shape[0]
    L = u.shape[1]
    up = jnp.pad(u, ((0, 0), (K - 1, 0), (0, 0)))
    return sum(up[:, j:j + L] * w[j] for j in range(K))


def chunked_gated_delta_rule(q, k, v, beta, g):
    B, H, L, dk = q.shape
    dv = v.shape[-1]
    n = L // CHUNK
    rs = lambda t: t.reshape(B, H, n, CHUNK, *t.shape[3:])
    q, k, v, beta, g = rs(q), rs(k), rs(v), rs(beta), rs(g)
    G = jnp.cumsum(g, axis=-1)
    causal = jnp.tril(jnp.ones((CHUNK, CHUNK), dtype=bool))
    strict = jnp.tril(jnp.ones((CHUNK, CHUNK), dtype=bool), -1)
    decay = jnp.exp(jnp.where(causal, G[..., :, None] - G[..., None, :], -jnp.inf))
    kk = jnp.einsum('bhncd,bhned->bhnce', k, k)
    A = jnp.where(strict, beta[..., :, None] * kk * decay, 0.0)
    eye = jnp.eye(CHUNK, dtype=jnp.float32)
    T = lax.linalg.triangular_solve(eye + A, jnp.broadcast_to(eye, A.shape),
                                    left_side=True, lower=True)
    eG = jnp.exp(G)
    W = jnp.einsum('bhnce,bhned->bhncd', T, k * (beta * eG)[..., None])
    U = jnp.einsum('bhnce,bhned->bhncd', T, v * beta[..., None])
    qk = jnp.einsum('bhncd,bhned->bhnce', q, k) * decay
    qg = q * eG[..., None]
    kd = k * jnp.exp(G[..., -1:] - G)[..., None]
    glast = jnp.exp(G[..., -1])

    def step(S, inp):
        Wc, Uc, qkc, qgc, kdc, gc = inp
        v_new = Uc - jnp.einsum('bhcd,bhdv->bhcv', Wc, S)
        o = jnp.einsum('bhcd,bhdv->bhcv', qgc, S) + jnp.einsum('bhce,bhev->bhcv', qkc, v_new)
        S = S * gc[..., None, None] + jnp.einsum('bhcd,bhcv->bhdv', kdc, v_new)
        return S, o

    xs = tuple(jnp.moveaxis(t, 2, 0) for t in (W, U, qk, qg, kd, glast))
    S0 = jnp.zeros((B, H, dk, dv), jnp.float32)
    _, o = lax.scan(step, S0, xs)
    return jnp.moveaxis(o, 0, 2).reshape(B, H, L, dv)


def trailing_mean(u, w):
    L = u.shape[1]
    cs = jnp.cumsum(u.astype(jnp.float32), axis=1)
    cs = jnp.pad(cs, ((0, 0), (w, 0), (0, 0)))
    window_sum = cs[:, w:] - cs[:, :L]
    count = jnp.minimum(jnp.arange(L) + 1, w).astype(jnp.float32)
    return (window_sum / count[None, :, None]).astype(u.dtype)


def multiscale_pool(u, pool_w, pool_scale):
    B, L, _ = u.shape
    ug = u.reshape(B, L, POOL_GROUPS, POOL_GROUP_DIM)
    pooled = jnp.stack([trailing_mean(ug[:, :, i], POOL_WINDOWS[i]) for i in range(POOL_GROUPS)],
                       axis=2) - ug
    y = jnp.einsum('blgc,gcd->blgd', pooled, pool_w)
    return y.reshape(B, L, POOL_WIDTH) * pool_scale


def moe_ffn(h, w_router, b_router, w_gate_up, b_gate_up, w_down, b_down):
    B, L, D = h.shape
    t = h.reshape(B * L, D)
    logits = (t @ w_router + b_router).astype(jnp.float32)
    top_v, top_i = lax.top_k(logits, TOP_K)
    top_w = jax.nn.softmax(top_v, axis=-1)
    comb = jnp.sum(jax.nn.one_hot(top_i, N_EXPERTS, dtype=jnp.float32) * top_w[..., None],
                   axis=1).astype(h.dtype)
    y = jnp.zeros_like(t)
    for e in range(N_EXPERTS):
        gu = t @ w_gate_up[e] + b_gate_up[e]
        gate, up = gu[:, :D_FF], gu[:, D_FF:]
        gate = jnp.minimum(gate, SWIGLU_LIMIT)
        up = jnp.clip(up, -SWIGLU_LIMIT, SWIGLU_LIMIT)
        act = gate * jax.nn.sigmoid(SWIGLU_ALPHA * gate) * (up + 1.0)
        y = y + comb[:, e:e + 1] * (act @ w_down[e] + b_down[e])
    return y.reshape(B, L, D)


def setup_inputs(seed: int = 0) -> dict:
    key = jax.random.key(seed)
    ks = jax.random.split(key, 24)
    f32 = jnp.float32
    nrm = lambda k, shape, s: jax.random.normal(k, shape, f32) * s
    dt = jnp.exp(jax.random.uniform(ks[4], (DEPTH, DN_HEADS), f32,
                                    math.log(1e-3), math.log(1e-1)))
    return {
        "x": jax.random.normal(ks[0], (BATCH, SEQ, D_MODEL), f32),
        "g_mix": 1.0 + nrm(ks[1], (DEPTH, D_MODEL), 0.02),
        "w_in": nrm(ks[2], (DEPTH, D_MODEL, IN_WIDTH), D_MODEL ** -0.5),
        "conv_w": nrm(ks[3], (DEPTH, CONV_WIDTH, 3 * DN_WIDTH), CONV_WIDTH ** -0.5),
        "a_log": jnp.log(jax.random.uniform(ks[5], (DEPTH, DN_HEADS), f32, 1.0, 16.0)),
        "dt_bias": dt + jnp.log(-jnp.expm1(-dt)),
        "dn_norm": 1.0 + nrm(ks[6], (DEPTH, DN_HEAD_DIM), 0.02),
        "w_up_a": nrm(ks[7], (DEPTH, DN_WIDTH, D_MODEL), DN_WIDTH ** -0.5),
        "pool_w": nrm(ks[8], (DEPTH, POOL_GROUPS, POOL_GROUP_DIM, POOL_GROUP_DIM), POOL_GROUP_DIM ** -0.5),
        "pool_scale": 1.0 + nrm(ks[9], (DEPTH, POOL_WIDTH), 0.02),
        "w_up_b": nrm(ks[10], (DEPTH, POOL_WIDTH, D_MODEL), POOL_WIDTH ** -0.5),
        "w_out": nrm(ks[11], (DEPTH, D_MODEL, D_MODEL), D_MODEL ** -0.5),
        "g_ffn": 1.0 + nrm(ks[12], (DEPTH, D_MODEL), 0.02),
        "w_router": nrm(ks[13], (DEPTH, D_MODEL, N_EXPERTS), D_MODEL ** -0.5),
        "b_router": nrm(ks[14], (DEPTH, N_EXPERTS), 0.01),
        "w_gate_up": nrm(ks[15], (DEPTH, N_EXPERTS, D_MODEL, 2 * D_FF), D_MODEL ** -0.5),
        "b_gate_up": nrm(ks[16], (DEPTH, N_EXPERTS, 2 * D_FF), 0.01),
        "w_down": nrm(ks[17], (DEPTH, N_EXPERTS, D_FF, D_MODEL), D_FF ** -0.5),
        "b_down": nrm(ks[18], (DEPTH, N_EXPERTS, D_MODEL), 0.01),
        "g_final": 1.0 + nrm(ks[19], (D_MODEL,), 0.02),
    }


def reference(x, g_mix, w_in, conv_w, a_log, dt_bias, dn_norm, w_up_a, pool_w, pool_scale,
              w_up_b, w_out, g_ffn, w_router, b_router, w_gate_up, b_gate_up, w_down, b_down,
              g_final):
    f32 = jnp.float32
    B, L, D = x.shape
    sizes = [3 * DN_WIDTH, DN_WIDTH, DN_HEADS, DN_HEADS, POOL_WIDTH]
    split_idx = [int(s) for s in np.cumsum(sizes)]
    for l in range(DEPTH):
        h = rms_norm(x, g_mix[l])
        p = h @ w_in[l]
        qkv, z, b_raw, a_raw, u, gates = jnp.split(p, split_idx, axis=-1)

        qkv = jax.nn.silu(causal_depthwise_conv(qkv, conv_w[l]))
        q, k, v = jnp.split(qkv, 3, axis=-1)
        heads = lambda t: t.reshape(B, L, DN_HEADS, DN_HEAD_DIM).transpose(0, 2, 1, 3).astype(f32)
        q = l2_normalize(heads(q)) * (DN_HEAD_DIM ** -0.5)
        k = l2_normalize(heads(k))
        v = heads(v)
        beta = jax.nn.sigmoid(b_raw.astype(f32)).transpose(0, 2, 1)
        g = (-jnp.exp(a_log[l].astype(f32))
             * jax.nn.softplus(a_raw.astype(f32) + dt_bias[l].astype(f32))).transpose(0, 2, 1)
        o = chunked_gated_delta_rule(q, k, v, beta, g).transpose(0, 2, 1, 3)
        o = rms_norm(o, dn_norm[l]) * jax.nn.silu(z.reshape(B, L, DN_HEADS, DN_HEAD_DIM).astype(f32))
        y_a = o.reshape(B, L, DN_WIDTH).astype(x.dtype) @ w_up_a[l]

        y_b = multiscale_pool(u, pool_w[l], pool_scale[l]) @ w_up_b[l]

        gate_a, gate_b = jnp.split(jax.nn.sigmoid(gates), N_BRANCH, axis=-1)
        x = x + (gate_a * y_a + gate_b * y_b) @ w_out[l]

        x = x + moe_ffn(rms_norm(x, g_ffn[l]), w_router[l], b_router[l], w_gate_up[l],
                        b_gate_up[l], w_down[l], b_down[l])
    return rms_norm(x, g_final)
```

```python
import functools

import jax
import jax.numpy as jnp
from jax import lax
from jax.experimental import pallas as pl
from jax.experimental.pallas import tpu as pltpu

F32 = jnp.float32
BF16 = jnp.bfloat16

D_MODEL = 1024
CHUNK = 64
DN_HEADS = 8
DN_HEAD_DIM = 64
DN_WIDTH = DN_HEADS * DN_HEAD_DIM
CONV_WIDTH = 4
POOL_GROUPS = 4
POOL_WINDOWS = (2, 4, 8, 16)
POOL_WIDTH = 512
POOL_GROUP_DIM = 128
POOL_HALO = 16
N_EXPERTS = 32
TOP_K = 4
D_FF = D_MODEL
SWIGLU_LIMIT = 7.0
SWIGLU_ALPHA = 1.702
NORM_EPS = 1e-6
LANES = 128
SUBLANES = 8
VMEM_LIMIT = 56 * 1024 * 1024

IN_TM = 256
DN_LB = 256
MIX_TM = 256
MOE_TM = 512


def _dot(a, b):
    return jnp.dot(a, b, preferred_element_type=F32)


def _dot_nt(a, b):
    return lax.dot_general(a, b, (((1,), (1,)), ((), ())), preferred_element_type=F32)


def _dot_tn(a, b):
    return lax.dot_general(a, b, (((0,), (0,)), ((), ())), preferred_element_type=F32)


def _split3(x):
    hi = x.astype(BF16)
    r = x - hi.astype(F32)
    mid = r.astype(BF16)
    lo = (r - mid.astype(F32)).astype(BF16)
    return hi, mid, lo


def _dot_exact_rhs(x, m):
    hi, mid, lo = _split3(x)
    return _dot(hi, m) + _dot(mid, m) + _dot(lo, m)


def _dot_exact_lhs(m, x):
    hi, mid, lo = _split3(x)
    return _dot(m, hi) + _dot(m, mid) + _dot(m, lo)


def _softplus(x):
    return jnp.maximum(x, 0.0) + jnp.log1p(jnp.exp(-jnp.abs(x)))


def _sigmoid(x):
    return 1.0 / (1.0 + jnp.exp(-x))


def _rms(x, g):
    return x * lax.rsqrt(jnp.mean(x * x, axis=-1, keepdims=True) + NORM_EPS) * g


def _in_proj_kernel(x_ref, g_ref, wqkv_ref, wz_ref, wu_ref, wg_ref, wba_ref, wbat_ref,
                    qkv_ref, z_ref, u_ref, gates_ref, ba_ref, bat_ref):
    hb = _rms(x_ref[...], g_ref[...]).astype(BF16)
    qkv_ref[...] = _dot(hb, wqkv_ref[...])
    z_ref[...] = _dot(hb, wz_ref[...])
    u_ref[...] = _dot(hb, wu_ref[...])
    gates_ref[...] = _dot(hb, wg_ref[...])
    ba_ref[...] = _dot(hb, wba_ref[...])
    bat_ref[...] = _dot_nt(wbat_ref[...], hb)


def _in_proj(x2, g_mix, wqkv, wz, wu, wg, wba, wbat):
    t = x2.shape[0]
    tm = IN_TM
    full = lambda a: pl.BlockSpec(a.shape, lambda i: (0, 0))
    row = lambda n: pl.BlockSpec((tm, n), lambda i: (i, 0))
    return pl.pallas_call(
        _in_proj_kernel,
        grid=(t // tm,),
        in_specs=[row(D_MODEL), full(g_mix), full(wqkv), full(wz), full(wu), full(wg), full(wba), full(wbat)],
        out_specs=[row(3 * DN_WIDTH), row(DN_WIDTH), row(POOL_WIDTH), row(2 * D_MODEL), row(2 * LANES),
                   pl.BlockSpec((2 * SUBLANES, tm), lambda i: (0, i))],
        out_shape=[jax.ShapeDtypeStruct((t, 3 * DN_WIDTH), F32),
                   jax.ShapeDtypeStruct((t, DN_WIDTH), F32),
                   jax.ShapeDtypeStruct((t, POOL_WIDTH), F32),
                   jax.ShapeDtypeStruct((t, 2 * D_MODEL), F32),
                   jax.ShapeDtypeStruct((t, 2 * LANES), F32),
                   jax.ShapeDtypeStruct((2 * SUBLANES, t), F32)],
        compiler_params=pltpu.CompilerParams(dimension_semantics=("arbitrary",), vmem_limit_bytes=VMEM_LIMIT),
        name="in_proj",
    )(x2, g_mix, wqkv, wz, wu, wg, wba, wbat)


def _deltanet_kernel(qkv_ref, z_ref, ba_ref, bat_ref, convw_ref, alog_r_ref, dtb_r_ref, alog_c_ref, dtb_c_ref,
                     dnw_ref, o_ref,
                     s_ref, carry_ref, qn_ref, kn_ref, kbe_ref, vb_ref, qg_ref, kd_ref,
                     xbeta_ref, xgc_ref, xgl_ref, gr_ref, oacc_ref):
    lb = qkv_ref.shape[0]
    n_chunks = lb // CHUNK

    @pl.when(pl.program_id(1) == 0)
    def _():
        s_ref[...] = jnp.zeros_like(s_ref)
        carry_ref[...] = jnp.zeros_like(carry_ref)

    blk = qkv_ref[...]
    carry = carry_ref[...]
    cw = convw_ref[...]
    row8 = lax.broadcasted_iota(jnp.int32, (SUBLANES, 3 * DN_WIDTH), 0)
    acc = blk * cw[CONV_WIDTH - 1:CONV_WIDTH]
    head = blk[0:SUBLANES] * cw[CONV_WIDTH - 1:CONV_WIDTH]
    for s in range(1, CONV_WIDTH):
        w_s = cw[CONV_WIDTH - 1 - s:CONV_WIDTH - s]
        rolled = pltpu.roll(blk, s, 0)
        acc = acc + rolled * w_s
        head = head + jnp.where(row8 < s, pltpu.roll(carry, s, 0), rolled[0:SUBLANES]) * w_s
    carry_ref[...] = blk[lb - SUBLANES:lb]
    act = jnp.concatenate([head, acc[SUBLANES:lb]], axis=0)
    act = act * _sigmoid(act)
    q = act[:, 0:DN_WIDTH]
    k = act[:, DN_WIDTH:2 * DN_WIDTH]
    v = act[:, 2 * DN_WIDTH:3 * DN_WIDTH]

    hr = lax.broadcasted_iota(jnp.int32, (DN_WIDTH, DN_WIDTH), 0) // DN_HEAD_DIM
    hc = lax.broadcasted_iota(jnp.int32, (DN_WIDTH, DN_WIDTH), 1) // DN_HEAD_DIM
    head_ones = (hr == hc).astype(BF16)
    qn = q * lax.rsqrt(_dot_exact_rhs(q * q, head_ones) + NORM_EPS) * (DN_HEAD_DIM ** -0.5)
    kn = k * lax.rsqrt(_dot_exact_rhs(k * k, head_ones) + NORM_EPS)

    ba = ba_ref[...]
    beta_c = _sigmoid(ba[:, 0:LANES])
    g_c = -jnp.exp(alog_r_ref[...]) * _softplus(ba[:, LANES:2 * LANES] + dtb_r_ref[...])
    lane = lax.broadcasted_iota(jnp.int32, (lb, LANES), 1)
    g_c = jnp.where(lane < DN_HEADS, g_c, 0.0)
    bat = bat_ref[...]
    g_r = -jnp.exp(alog_c_ref[...]) * _softplus(bat[SUBLANES:2 * SUBLANES] + dtb_c_ref[...])

    rr = lax.broadcasted_iota(jnp.int32, (lb, lb), 0)
    cc = lax.broadcasted_iota(jnp.int32, (lb, lb), 1)
    same = (rr // CHUNK) == (cc // CHUNK)
    tril = (same & (cc <= rr)).astype(BF16)
    triu = (same & (rr <= cc)).astype(BF16)
    chunk_ones = same.astype(BF16)
    gc = _dot_exact_lhs(tril, g_c)
    gtot = _dot_exact_lhs(chunk_ones, g_c)
    gr = _dot_exact_rhs(g_r, triu)
    for c in range(n_chunks):
        gr_ref[c] = gr[:, c * CHUNK:(c + 1) * CHUNK]

    er = lax.broadcasted_iota(jnp.int32, (LANES, DN_WIDTH), 0)
    ec = lax.broadcasted_iota(jnp.int32, (LANES, DN_WIDTH), 1) // DN_HEAD_DIM
    expand = (er == ec).astype(BF16)
    eg = jnp.exp(gc)
    x_beta = _dot_exact_rhs(beta_c, expand)
    xbeta_ref[...] = x_beta
    xgc_ref[...] = _dot_exact_rhs(gc, expand)
    xgl_ref[...] = _dot_exact_rhs(jnp.exp(gtot), expand)
    qn_ref[...] = qn.astype(BF16)
    kn_ref[...] = kn.astype(BF16)
    kbe_ref[...] = (kn * _dot_exact_rhs(beta_c * eg, expand)).astype(BF16)
    vb_ref[...] = (v * x_beta).astype(BF16)
    qg_ref[...] = (qn * _dot_exact_rhs(eg, expand)).astype(BF16)
    kd_ref[...] = (kn * _dot_exact_rhs(jnp.exp(gtot - gc), expand)).astype(BF16)

    ci = lax.broadcasted_iota(jnp.int32, (CHUNK, CHUNK), 0)
    cj = lax.broadcasted_iota(jnp.int32, (CHUNK, CHUNK), 1)
    eye = (ci == cj).astype(F32)

    def chunk_body(c, carry_unused):
        r0 = pl.multiple_of(c * CHUNK, CHUNK)
        rows = pl.ds(r0, CHUNK)
        for h in range(DN_HEADS):
            lanes = pl.ds(h * DN_HEAD_DIM, DN_HEAD_DIM)
            kb = kn_ref[rows, lanes]
            qb = qn_ref[rows, lanes]
            diff = xgc_ref[rows, lanes] - gr_ref[c, h:h + 1, :]
            decay = jnp.exp(jnp.where(ci >= cj, diff, -jnp.inf))
            a = jnp.where(ci > cj, xbeta_ref[rows, lanes] * _dot_nt(kb, kb) * decay, 0.0)
            qk = _dot_nt(qb, kb) * decay
            pw = a.astype(BF16)
            t_inv = eye - a
            for _ in range(5):
                pw_f = _dot(pw, pw)
                pw = pw_f.astype(BF16)
                t_inv = t_inv + _dot(t_inv.astype(BF16), pw)
            tb = t_inv.astype(BF16)
            w = _dot(tb, kbe_ref[rows, lanes])
            u = _dot(tb, vb_ref[rows, lanes])
            s = s_ref[h]
            sb = s.astype(BF16)
            v_new = u - _dot(w.astype(BF16), sb)
            vnb = v_new.astype(BF16)
            o = _dot(qg_ref[rows, lanes], sb) + _dot(qk.astype(BF16), vnb)
            s_ref[h] = s * xgl_ref[rows, lanes] + _dot_tn(kd_ref[rows, lanes], vnb)
            oacc_ref[rows, lanes] = o
        return carry_unused

    lax.fori_loop(0, n_chunks, chunk_body, 0)

    o = oacc_ref[...]
    ms = _dot_exact_rhs(o * o, head_ones) * (1.0 / DN_HEAD_DIM)
    z = z_ref[...]
    o_ref[...] = (o * lax.rsqrt(ms + NORM_EPS) * dnw_ref[...] * (z * _sigmoid(z))).astype(o_ref.dtype)


def _deltanet(qkv, z, ba, bat, conv_w, alog_r, dtb_r, alog_c, dtb_c, dnw, batch, seq):
    lb = DN_LB
    nb = seq // lb
    full = lambda a: pl.BlockSpec(a.shape, lambda b, j: (0,) * a.ndim)
    row = lambda n: pl.BlockSpec((lb, n), lambda b, j: (b * nb + j, 0))
    return pl.pallas_call(
        _deltanet_kernel,
        grid=(batch, nb),
        in_specs=[row(3 * DN_WIDTH), row(DN_WIDTH), row(2 * LANES),
                  pl.BlockSpec((2 * SUBLANES, lb), lambda b, j: (0, b * nb + j)),
                  full(conv_w), full(alog_r), full(dtb_r), full(alog_c), full(dtb_c), full(dnw)],
        out_specs=row(DN_WIDTH),
        out_shape=jax.ShapeDtypeStruct((batch * seq, DN_WIDTH), BF16),
        scratch_shapes=[
            pltpu.VMEM((DN_HEADS, DN_HEAD_DIM, DN_HEAD_DIM), F32),
            pltpu.VMEM((SUBLANES, 3 * DN_WIDTH), F32),
            pltpu.VMEM((lb, DN_WIDTH), BF16),
            pltpu.VMEM((lb, DN_WIDTH), BF16),
            pltpu.VMEM((lb, DN_WIDTH), BF16),
            pltpu.VMEM((lb, DN_WIDTH), BF16),
            pltpu.VMEM((lb, DN_WIDTH), BF16),
            pltpu.VMEM((lb, DN_WIDTH), BF16),
            pltpu.VMEM((lb, DN_WIDTH), F32),
            pltpu.VMEM((lb, DN_WIDTH), F32),
            pltpu.VMEM((lb, DN_WIDTH), F32),
            pltpu.VMEM((lb // CHUNK, DN_HEADS, CHUNK), F32),
            pltpu.VMEM((lb, DN_WIDTH), F32),
        ],
        compiler_params=pltpu.CompilerParams(dimension_semantics=("arbitrary", "arbitrary"),
                                             vmem_limit_bytes=VMEM_LIMIT),
        name="deltanet",
    )(qkv, z, ba, bat, conv_w, alog_r, dtb_r, alog_c, dtb_c, dnw)


def _mix_kernel(x_ref, on_ref, u_ref, gates_ref, wupa_ref, poolw_ref, pscale_ref, wupb_ref, wout_ref,
                gffn_ref, wr_hi_ref, wr_lo_ref, br_ref,
                x1_ref, h2_ref, comb_ref, ucarry_ref):
    tm = x_ref.shape[0]
    j = pl.program_id(1)

    @pl.when(j == 0)
    def _():
        ucarry_ref[...] = jnp.zeros_like(ucarry_ref)

    u = u_ref[...]
    ext = jnp.concatenate([ucarry_ref[...], u], axis=0)
    ucarry_ref[...] = u[tm - POOL_HALO:tm]
    t_pos = j * tm + lax.broadcasted_iota(jnp.int32, (tm, POOL_GROUP_DIM), 0)
    ys = []
    for g in range(POOL_GROUPS):
        s = ext[:, g * POOL_GROUP_DIM:(g + 1) * POOL_GROUP_DIM]
        shift = 1
        while shift < POOL_WINDOWS[g]:
            s = s + pltpu.roll(s, shift, 0)
            shift *= 2
        count = jnp.minimum(t_pos + 1, POOL_WINDOWS[g]).astype(F32)
        pooled = s[POOL_HALO:] / count - u[:, g * POOL_GROUP_DIM:(g + 1) * POOL_GROUP_DIM]
        ys.append(_dot(pooled.astype(BF16), poolw_ref[g]))
    yb = jnp.concatenate(ys, axis=-1) * pscale_ref[...]
    y_b = _dot(yb.astype(BF16), wupb_ref[...])
    y_a = _dot(on_ref[...], wupa_ref[...])
    gates = gates_ref[...]
    merged = _sigmoid(gates[:, 0:D_MODEL]) * y_a + _sigmoid(gates[:, D_MODEL:2 * D_MODEL]) * y_b
    x1 = x_ref[...] + _dot(merged.astype(BF16), wout_ref[...])
    x1_ref[...] = x1

    h2 = _rms(x1, gffn_ref[...])
    h2_ref[...] = h2.astype(BF16)

    hi, mid, _ = _split3(h2)
    logits = _dot(hi, wr_hi_ref[...]) + _dot(mid, wr_hi_ref[...]) + _dot(hi, wr_lo_ref[...]) + br_ref[...]
    lane = lax.broadcasted_iota(jnp.int32, (tm, LANES), 1)
    lg = jnp.where(lane < N_EXPERTS, logits, -jnp.inf)
    vals, sels = [], []
    for _ in range(TOP_K):
        m = jnp.max(lg, axis=-1, keepdims=True)
        idx = jnp.min(jnp.where(lg == m, lane, LANES), axis=-1, keepdims=True)
        sel = lane == idx
        vals.append(m)
        sels.append(sel)
        lg = jnp.where(sel, -jnp.inf, lg)
    es = [jnp.exp(vk - vals[0]) for vk in vals]
    denom = es[0] + es[1] + es[2] + es[3]
    comb = jnp.zeros((tm, LANES), F32)
    for ek, sel in zip(es, sels):
        comb = comb + jnp.where(sel, ek / denom, 0.0)
    comb_ref[...] = comb


def _mix(x2, on, u, gates, wupa, poolw, pscale, wupb, wout, gffn, wr_hi, wr_lo, br, batch, seq):
    tm = MIX_TM
    nb = seq // tm
    t = batch * seq
    full = lambda a: pl.BlockSpec(a.shape, lambda b, j: (0,) * a.ndim)
    row = lambda n: pl.BlockSpec((tm, n), lambda b, j: (b * nb + j, 0))
    return pl.pallas_call(
        _mix_kernel,
        grid=(batch, nb),
        in_specs=[row(D_MODEL), row(DN_WIDTH), row(POOL_WIDTH), row(2 * D_MODEL),
                  full(wupa), full(poolw), full(pscale), full(wupb), full(wout), full(gffn),
                  full(wr_hi), full(wr_lo), full(br)],
        out_specs=[row(D_MODEL), row(D_MODEL), row(LANES)],
        out_shape=[jax.ShapeDtypeStruct((t, D_MODEL), F32),
                   jax.ShapeDtypeStruct((t, D_MODEL), BF16),
                   jax.ShapeDtypeStruct((t, LANES), F32)],
        scratch_shapes=[pltpu.VMEM((POOL_HALO, POOL_WIDTH), F32)],
        compiler_params=pltpu.CompilerParams(dimension_semantics=("arbitrary", "arbitrary"),
                                             vmem_limit_bytes=VMEM_LIMIT),
        name="mix",
    )(x2, on, u, gates, wupa, poolw, pscale, wupb, wout, gffn, wr_hi, wr_lo, br)


def _moe_kernel(h2_ref, comb_ref, x1_ref, wgu_ref, bgu_ref, wd_ref, bd_ref, gfin_ref, out_ref, acc_ref):
    e = pl.program_id(1)

    @pl.when(e == 0)
    def _():
        acc_ref[...] = jnp.zeros_like(acc_ref)

    gu = _dot(h2_ref[...], wgu_ref[...]) + bgu_ref[...]
    gate = jnp.minimum(gu[:, 0:D_FF], SWIGLU_LIMIT)
    up = jnp.clip(gu[:, D_FF:2 * D_FF], -SWIGLU_LIMIT, SWIGLU_LIMIT)
    act = gate * _sigmoid(SWIGLU_ALPHA * gate) * (up + 1.0)
    y = _dot(act.astype(BF16), wd_ref[...]) + bd_ref[...]
    comb = comb_ref[...]
    lane = lax.broadcasted_iota(jnp.int32, comb.shape, 1)
    c_e = jnp.sum(jnp.where(lane == e, comb, 0.0), axis=-1, keepdims=True)
    acc_ref[...] += c_e * y

    @pl.when(e == pl.num_programs(1) - 1)
    def _():
        out_ref[...] = _rms(x1_ref[...] + acc_ref[...], gfin_ref[...])


def _moe(h2, comb, x1, wgu, bgu, wd, bd, gfin):
    t = h2.shape[0]
    tm = MOE_TM
    return pl.pallas_call(
        _moe_kernel,
        grid=(t // tm, N_EXPERTS),
        in_specs=[pl.BlockSpec((tm, D_MODEL), lambda i, e: (i, 0)),
                  pl.BlockSpec((tm, LANES), lambda i, e: (i, 0)),
                  pl.BlockSpec((tm, D_MODEL), lambda i, e: (i, 0)),
                  pl.BlockSpec((None, D_MODEL, 2 * D_FF), lambda i, e: (e, 0, 0)),
                  pl.BlockSpec((None, 1, 2 * D_FF), lambda i, e: (e, 0, 0)),
                  pl.BlockSpec((None, D_FF, D_MODEL), lambda i, e: (e, 0, 0)),
                  pl.BlockSpec((None, 1, D_MODEL), lambda i, e: (e, 0, 0)),
                  pl.BlockSpec((1, D_MODEL), lambda i, e: (0, 0))],
        out_specs=pl.BlockSpec((tm, D_MODEL), lambda i, e: (i, 0)),
        out_shape=jax.ShapeDtypeStruct((t, D_MODEL), F32),
        scratch_shapes=[pltpu.VMEM((tm, D_MODEL), F32)],
        compiler_params=pltpu.CompilerParams(dimension_semantics=("arbitrary", "arbitrary"),
                                             vmem_limit_bytes=VMEM_LIMIT),
        name="moe",
    )(h2, comb, x1, wgu, bgu, wd, bd, gfin)


def kernel(x, g_mix, w_in, conv_w, a_log, dt_bias, dn_norm, w_up_a, pool_w, pool_scale, w_up_b, w_out, g_ffn,
           w_router, b_router, w_gate_up, b_gate_up, w_down, b_down, g_final):
    batch, seq, d = x.shape
    assert d == D_MODEL and seq % DN_LB == 0 and seq % MIX_TM == 0 and (batch * seq) % MOE_TM == 0
    assert g_mix.shape[0] == 1, "one layer"
    t = batch * seq
    x2 = x.reshape(t, d)

    w = w_in[0]
    o_z = 3 * DN_WIDTH
    o_b = o_z + DN_WIDTH
    o_a = o_b + DN_HEADS
    o_u = o_a + DN_HEADS
    o_g = o_u + POOL_WIDTH
    wqkv = w[:, 0:o_z].astype(BF16)
    wz = w[:, o_z:o_b].astype(BF16)
    wu = w[:, o_u:o_g].astype(BF16)
    wg = w[:, o_g:].astype(BF16)
    w_b = w[:, o_b:o_a]
    w_a = w[:, o_a:o_u]
    pad = jnp.zeros((d, LANES - DN_HEADS), F32)
    wba = jnp.concatenate([w_b, pad, w_a, pad], axis=1).astype(BF16)
    wbat = jnp.concatenate([w_b, w_a], axis=1).T.astype(BF16)

    qkv, z, u, gates, ba, bat = _in_proj(x2, g_mix, wqkv, wz, wu, wg, wba, wbat)

    lane_pad = lambda p: jnp.pad(p.reshape(1, DN_HEADS), ((0, 0), (0, LANES - DN_HEADS)))
    on = _deltanet(qkv, z, ba, bat, conv_w[0], lane_pad(a_log[0]), lane_pad(dt_bias[0]),
                   a_log[0].reshape(DN_HEADS, 1), dt_bias[0].reshape(DN_HEADS, 1),
                   jnp.tile(dn_norm[0], DN_HEADS).reshape(1, DN_WIDTH), batch, seq)

    wr = jnp.pad(w_router[0], ((0, 0), (0, LANES - N_EXPERTS)))
    wr_hi = wr.astype(BF16)
    wr_lo = (wr - wr_hi.astype(F32)).astype(BF16)
    br = jnp.pad(b_router[0].reshape(1, N_EXPERTS), ((0, 0), (0, LANES - N_EXPERTS)))
    x1, h2, comb = _mix(x2, on, u, gates, w_up_a[0].astype(BF16), pool_w[0].astype(BF16),
                        pool_scale[0].reshape(1, POOL_WIDTH), w_up_b[0].astype(BF16), w_out[0].astype(BF16),
                        g_ffn, wr_hi, wr_lo, br, batch, seq)

    out = _moe(h2, comb, x1, w_gate_up[0].astype(BF16), b_gate_up[0].reshape(N_EXPERTS, 1, 2 * D_FF),
               w_down[0].astype(BF16), b_down[0].reshape(N_EXPERTS, 1, D_MODEL), g_final.reshape(1, D_MODEL))
    return out.reshape(batch, seq, d)
```

```python
import functools

import jax
import jax.numpy as jnp
from jax import lax
from jax.experimental import pallas as pl
from jax.experimental.pallas import tpu as pltpu

F32 = jnp.float32
BF16 = jnp.bfloat16

D_MODEL = 1024
CHUNK = 64
DN_HEADS = 8
DN_HEAD_DIM = 64
DN_WIDTH = DN_HEADS * DN_HEAD_DIM
CONV_WIDTH = 4
POOL_GROUPS = 4
POOL_WINDOWS = (2, 4, 8, 16)
POOL_WIDTH = 512
POOL_GROUP_DIM = 128
POOL_HALO = 16
N_EXPERTS = 32
TOP_K = 4
D_FF = D_MODEL
SWIGLU_LIMIT = 7.0
SWIGLU_ALPHA = 1.702
NORM_EPS = 1e-6
LANES = 128
SUBLANES = 8
VMEM_LIMIT = 56 * 1024 * 1024

IN_TM = 256
DN_LB = 256
MIX_TM = 256
MOE_TM = 512


def _dot(a, b):
    return jnp.dot(a, b, preferred_element_type=F32)


def _dot_nt(a, b):
    return lax.dot_general(a, b, (((1,), (1,)), ((), ())), preferred_element_type=F32)


def _dot_tn(a, b):
    return lax.dot_general(a, b, (((0,), (0,)), ((), ())), preferred_element_type=F32)


def _split2(x):
    hi = x.astype(BF16)
    lo = (x - hi.astype(F32)).astype(BF16)
    return hi, lo


def _dot_exact_rhs(x, m):
    hi, lo = _split2(x)
    return _dot(hi, m) + _dot(lo, m)


def _dot_exact_lhs(m, x):
    hi, lo = _split2(x)
    return _dot(m, hi) + _dot(m, lo)


def _softplus(x):
    return jnp.maximum(x, 0.0) + jnp.log1p(jnp.exp(-jnp.abs(x)))


def _sigmoid(x):
    return 1.0 / (1.0 + jnp.exp(-x))


def _rms(x, g):
    return x * lax.rsqrt(jnp.mean(x * x, axis=-1, keepdims=True) + NORM_EPS) * g


def _in_proj_kernel(x_ref, g_ref, wqkv_ref, wz_ref, wu_ref, wg_ref, wba_ref, wbat_ref,
                    qkv_ref, z_ref, u_ref, gates_ref, ba_ref, bat_ref):
    hb = _rms(x_ref[...], g_ref[...]).astype(BF16)
    qkv_ref[...] = _dot(hb, wqkv_ref[...])
    z_ref[...] = _dot(hb, wz_ref[...])
    u_ref[...] = _dot(hb, wu_ref[...])
    gates_ref[...] = _dot(hb, wg_ref[...])
    ba_ref[...] = _dot(hb, wba_ref[...])
    bat_ref[...] = _dot_nt(wbat_ref[...], hb)


def _in_proj(x2, g_mix, wqkv, wz, wu, wg, wba, wbat):
    t = x2.shape[0]
    tm = IN_TM
    full = lambda a: pl.BlockSpec(a.shape, lambda i: (0, 0))
    row = lambda n: pl.BlockSpec((tm, n), lambda i: (i, 0))
    return pl.pallas_call(
        _in_proj_kernel,
        grid=(t // tm,),
        in_specs=[row(D_MODEL), full(g_mix), full(wqkv), full(wz), full(wu), full(wg), full(wba), full(wbat)],
        out_specs=[row(3 * DN_WIDTH), row(DN_WIDTH), row(POOL_WIDTH), row(2 * D_MODEL), row(2 * LANES),
                   pl.BlockSpec((2 * SUBLANES, tm), lambda i: (0, i))],
        out_shape=[jax.ShapeDtypeStruct((t, 3 * DN_WIDTH), F32),
                   jax.ShapeDtypeStruct((t, DN_WIDTH), F32),
                   jax.ShapeDtypeStruct((t, POOL_WIDTH), F32),
                   jax.ShapeDtypeStruct((t, 2 * D_MODEL), F32),
                   jax.ShapeDtypeStruct((t, 2 * LANES), F32),
                   jax.ShapeDtypeStruct((2 * SUBLANES, t), F32)],
        compiler_params=pltpu.CompilerParams(dimension_semantics=("arbitrary",), vmem_limit_bytes=VMEM_LIMIT),
        name="in_proj",
    )(x2, g_mix, wqkv, wz, wu, wg, wba, wbat)


def _deltanet_kernel(qkv_ref, z_ref, ba_ref, bat_ref, convw_ref, alog_r_ref, dtb_r_ref, alog_c_ref, dtb_c_ref,
                     dnw_ref, o_ref,
                     s_ref, carry_ref, qn_ref, kn_ref, kbe_ref, vb_ref, qg_ref, kd_ref,
                     xbeta_ref, xgc_ref, xgl_ref, gr_ref, oacc_ref):
    lb = qkv_ref.shape[0]
    n_chunks = lb // CHUNK

    @pl.when(pl.program_id(1) == 0)
    def _():
        s_ref[...] = jnp.zeros_like(s_ref)
        carry_ref[...] = jnp.zeros_like(carry_ref)

    blk = qkv_ref[...]
    carry = carry_ref[...]
    cw = convw_ref[...]
    row8 = lax.broadcasted_iota(jnp.int32, (SUBLANES, 3 * DN_WIDTH), 0)
    acc = blk * cw[CONV_WIDTH - 1:CONV_WIDTH]
    head = blk[0:SUBLANES] * cw[CONV_WIDTH - 1:CONV_WIDTH]
    for s in range(1, CONV_WIDTH):
        w_s = cw[CONV_WIDTH - 1 - s:CONV_WIDTH - s]
        rolled = pltpu.roll(blk, s, 0)
        acc = acc + rolled * w_s
        head = head + jnp.where(row8 < s, pltpu.roll(carry, s, 0), rolled[0:SUBLANES]) * w_s
    carry_ref[...] = blk[lb - SUBLANES:lb]
    act = jnp.concatenate([head, acc[SUBLANES:lb]], axis=0)
    act = act * _sigmoid(act)
    q = act[:, 0:DN_WIDTH]
    k = act[:, DN_WIDTH:2 * DN_WIDTH]
    v = act[:, 2 * DN_WIDTH:3 * DN_WIDTH]

    hr = lax.broadcasted_iota(jnp.int32, (DN_WIDTH, DN_WIDTH), 0) // DN_HEAD_DIM
    hc = lax.broadcasted_iota(jnp.int32, (DN_WIDTH, DN_WIDTH), 1) // DN_HEAD_DIM
    head_ones = (hr == hc).astype(BF16)
    qn = q * lax.rsqrt(_dot_exact_rhs(q * q, head_ones) + NORM_EPS) * (DN_HEAD_DIM ** -0.5)
    kn = k * lax.rsqrt(_dot_exact_rhs(k * k, head_ones) + NORM_EPS)

    ba = ba_ref[...]
    beta_c = _sigmoid(ba[:, 0:LANES])
    g_c = -jnp.exp(alog_r_ref[...]) * _softplus(ba[:, LANES:2 * LANES] + dtb_r_ref[...])
    lane = lax.broadcasted_iota(jnp.int32, (lb, LANES), 1)
    g_c = jnp.where(lane < DN_HEADS, g_c, 0.0)
    bat = bat_ref[...]
    g_r = -jnp.exp(alog_c_ref[...]) * _softplus(bat[SUBLANES:2 * SUBLANES] + dtb_c_ref[...])

    rr = lax.broadcasted_iota(jnp.int32, (lb, lb), 0)
    cc = lax.broadcasted_iota(jnp.int32, (lb, lb), 1)
    same = (rr // CHUNK) == (cc // CHUNK)
    tril = (same & (cc <= rr)).astype(BF16)
    triu = (same & (rr <= cc)).astype(BF16)
    chunk_ones = same.astype(BF16)
    gc = _dot_exact_lhs(tril, g_c)
    gtot = _dot_exact_lhs(chunk_ones, g_c)
    gr = _dot_exact_rhs(g_r, triu)
    for c in range(n_chunks):
        gr_ref[c] = gr[:, c * CHUNK:(c + 1) * CHUNK]

    er = lax.broadcasted_iota(jnp.int32, (LANES, DN_WIDTH), 0)
    ec = lax.broadcasted_iota(jnp.int32, (LANES, DN_WIDTH), 1) // DN_HEAD_DIM
    expand = (er == ec).astype(BF16)
    x_beta = _dot_exact_rhs(beta_c, expand)
    x_gc = _dot_exact_rhs(gc, expand)
    x_gtot = _dot_exact_rhs(gtot, expand)
    x_eg = jnp.exp(x_gc)
    xbeta_ref[...] = x_beta
    xgc_ref[...] = x_gc
    xgl_ref[...] = jnp.exp(x_gtot)
    qn_ref[...] = qn.astype(BF16)
    kn_ref[...] = kn.astype(BF16)
    kbe_ref[...] = (kn * (x_beta * x_eg)).astype(BF16)
    vb_ref[...] = (v * x_beta).astype(BF16)
    qg_ref[...] = (qn * x_eg).astype(BF16)
    kd_ref[...] = (kn * jnp.exp(x_gtot - x_gc)).astype(BF16)

    ci = lax.broadcasted_iota(jnp.int32, (CHUNK, CHUNK), 0)
    cj = lax.broadcasted_iota(jnp.int32, (CHUNK, CHUNK), 1)
    eye = (ci == cj).astype(F32)

    heads = range(DN_HEADS)
    lanes = [pl.ds(h * DN_HEAD_DIM, DN_HEAD_DIM) for h in heads]
    stack = lambda top, bot: jnp.concatenate([top, bot], axis=0)

    def chunk_body(c, carry_unused):
        rows = pl.ds(pl.multiple_of(c * CHUNK, CHUNK), CHUNK)
        kb = [kn_ref[rows, lanes[h]] for h in heads]
        kq = [_dot_nt(stack(kb[h], qn_ref[rows, lanes[h]]), kb[h]) for h in heads]
        decay = [jnp.exp(jnp.where(ci >= cj, xgc_ref[rows, lanes[h]] - gr_ref[c, h:h + 1, :], -jnp.inf))
                 for h in heads]
        a = [jnp.where(ci > cj, xbeta_ref[rows, lanes[h]] * kq[h][0:CHUNK] * decay[h], 0.0) for h in heads]
        qkd = [(kq[h][CHUNK:2 * CHUNK] * decay[h]).astype(BF16) for h in heads]
        t_inv = [eye - a[h] for h in heads]
        pw = [a[h].astype(BF16) for h in heads]
        pw = [_dot(pw[h], pw[h]).astype(BF16) for h in heads]
        for _ in range(4):
            r = [_dot(stack(t_inv[h].astype(BF16), pw[h]), pw[h]) for h in heads]
            t_inv = [t_inv[h] + r[h][0:CHUNK] for h in heads]
            pw = [r[h][CHUNK:2 * CHUNK].astype(BF16) for h in heads]
        t_inv = [t_inv[h] + _dot(t_inv[h].astype(BF16), pw[h]) for h in heads]
        tb = [t_inv[h].astype(BF16) for h in heads]
        w = [_dot(tb[h], kbe_ref[rows, lanes[h]]).astype(BF16) for h in heads]
        u = [_dot(tb[h], vb_ref[rows, lanes[h]]) for h in heads]
        s = [s_ref[h] for h in heads]
        ws = [_dot(stack(w[h], qg_ref[rows, lanes[h]]), s[h].astype(BF16)) for h in heads]
        vnb = [(u[h] - ws[h][0:CHUNK]).astype(BF16) for h in heads]
        for h in heads:
            oacc_ref[rows, lanes[h]] = ws[h][CHUNK:2 * CHUNK] + _dot(qkd[h], vnb[h])
        for h in heads:
            s_ref[h] = s[h] * xgl_ref[rows, lanes[h]] + _dot_tn(kd_ref[rows, lanes[h]], vnb[h])
        return carry_unused

    lax.fori_loop(0, n_chunks, chunk_body, 0)

    o = oacc_ref[...]
    ms = _dot_exact_rhs(o * o, head_ones) * (1.0 / DN_HEAD_DIM)
    z = z_ref[...]
    o_ref[...] = (o * lax.rsqrt(ms + NORM_EPS) * dnw_ref[...] * (z * _sigmoid(z))).astype(o_ref.dtype)


def _deltanet(qkv, z, ba, bat, conv_w, alog_r, dtb_r, alog_c, dtb_c, dnw, batch, seq):
    lb = DN_LB
    nb = seq // lb
    full = lambda a: pl.BlockSpec(a.shape, lambda b, j: (0,) * a.ndim)
    row = lambda n: pl.BlockSpec((lb, n), lambda b, j: (b * nb + j, 0))
    return pl.pallas_call(
        _deltanet_kernel,
        grid=(batch, nb),
        in_specs=[row(3 * DN_WIDTH), row(DN_WIDTH), row(2 * LANES),
                  pl.BlockSpec((2 * SUBLANES, lb), lambda b, j: (0, b * nb + j)),
                  full(conv_w), full(alog_r), full(dtb_r), full(alog_c), full(dtb_c), full(dnw)],
        out_specs=row(DN_WIDTH),
        out_shape=jax.ShapeDtypeStruct((batch * seq, DN_WIDTH), BF16),
        scratch_shapes=[
            pltpu.VMEM((DN_HEADS, DN_HEAD_DIM, DN_HEAD_DIM), F32),
            pltpu.VMEM((SUBLANES, 3 * DN_WIDTH), F32),
            pltpu.VMEM((lb, DN_WIDTH), BF16),
            pltpu.VMEM((lb, DN_WIDTH), BF16),
            pltpu.VMEM((lb, DN_WIDTH), BF16),
            pltpu.VMEM((lb, DN_WIDTH), BF16),
            pltpu.VMEM((lb, DN_WIDTH), BF16),
            pltpu.VMEM((lb, DN_WIDTH), BF16),
            pltpu.VMEM((lb, DN_WIDTH), F32),
            pltpu.VMEM((lb, DN_WIDTH), F32),
            pltpu.VMEM((lb, DN_WIDTH), F32),
            pltpu.VMEM((lb // CHUNK, DN_HEADS, CHUNK), F32),
            pltpu.VMEM((lb, DN_WIDTH), F32),
        ],
        compiler_params=pltpu.CompilerParams(dimension_semantics=("arbitrary", "arbitrary"),
                                             vmem_limit_bytes=VMEM_LIMIT),
        name="deltanet",
    )(qkv, z, ba, bat, conv_w, alog_r, dtb_r, alog_c, dtb_c, dnw)


def _mix_kernel(x_ref, on_ref, u_ref, gates_ref, wupa_ref, poolw_ref, pscale_ref, wupb_ref, wout_ref,
                gffn_ref, wr_hi_ref, wr_lo_ref, br_ref,
                x1_ref, h2_ref, comb_ref, ucarry_ref):
    tm = x_ref.shape[0]
    j = pl.program_id(1)

    @pl.when(j == 0)
    def _():
        ucarry_ref[...] = jnp.zeros_like(ucarry_ref)

    u = u_ref[...]
    ext = jnp.concatenate([ucarry_ref[...], u], axis=0)
    ucarry_ref[...] = u[tm - POOL_HALO:tm]
    t_pos = j * tm + lax.broadcasted_iota(jnp.int32, (tm, POOL_GROUP_DIM), 0)
    ys = []
    for g in range(POOL_GROUPS):
        s = ext[:, g * POOL_GROUP_DIM:(g + 1) * POOL_GROUP_DIM]
        shift = 1
        while shift < POOL_WINDOWS[g]:
            s = s + pltpu.roll(s, shift, 0)
            shift *= 2
        count = jnp.minimum(t_pos + 1, POOL_WINDOWS[g]).astype(F32)
        pooled = s[POOL_HALO:] / count - u[:, g * POOL_GROUP_DIM:(g + 1) * POOL_GROUP_DIM]
        ys.append(_dot(pooled.astype(BF16), poolw_ref[g]))
    yb = jnp.concatenate(ys, axis=-1) * pscale_ref[...]
    y_b = _dot(yb.astype(BF16), wupb_ref[...])
    y_a = _dot(on_ref[...], wupa_ref[...])
    gates = gates_ref[...]
    merged = _sigmoid(gates[:, 0:D_MODEL]) * y_a + _sigmoid(gates[:, D_MODEL:2 * D_MODEL]) * y_b
    x1 = x_ref[...] + _dot(merged.astype(BF16), wout_ref[...])
    x1_ref[...] = x1

    h2 = _rms(x1, gffn_ref[...])
    h2_ref[...] = h2.astype(BF16)

    hi, lo = _split2(h2)
    logits = _dot(hi, wr_hi_ref[...]) + _dot(lo, wr_hi_ref[...]) + _dot(hi, wr_lo_ref[...]) + br_ref[...]
    lane = lax.broadcasted_iota(jnp.int32, (tm, LANES), 1)
    lg = jnp.where(lane < N_EXPERTS, logits, -jnp.inf)
    vals, sels = [], []
    for _ in range(TOP_K):
        m = jnp.max(lg, axis=-1, keepdims=True)
        idx = jnp.min(jnp.where(lg == m, lane, LANES), axis=-1, keepdims=True)
        sel = lane == idx
        vals.append(m)
        sels.append(sel)
        lg = jnp.where(sel, -jnp.inf, lg)
    es = [jnp.exp(vk - vals[0]) for vk in vals]
    denom = es[0] + es[1] + es[2] + es[3]
    comb = jnp.zeros((tm, LANES), F32)
    for ek, sel in zip(es, sels):
        comb = comb + jnp.where(sel, ek / denom, 0.0)
    comb_ref[...] = comb


def _mix(x2, on, u, gates, wupa, poolw, pscale, wupb, wout, gffn, wr_hi, wr_lo, br, batch, seq):
    tm = MIX_TM
    nb = seq // tm
    t = batch * seq
    full = lambda a: pl.BlockSpec(a.shape, lambda b, j: (0,) * a.ndim)
    row = lambda n: pl.BlockSpec((tm, n), lambda b, j: (b * nb + j, 0))
    return pl.pallas_call(
        _mix_kernel,
        grid=(batch, nb),
        in_specs=[row(D_MODEL), row(DN_WIDTH), row(POOL_WIDTH), row(2 * D_MODEL),
                  full(wupa), full(poolw), full(pscale), full(wupb), full(wout), full(gffn),
                  full(wr_hi), full(wr_lo), full(br)],
        out_specs=[row(D_MODEL), row(D_MODEL), row(LANES)],
        out_shape=[jax.ShapeDtypeStruct((t, D_MODEL), F32),
                   jax.ShapeDtypeStruct((t, D_MODEL), BF16),
                   jax.ShapeDtypeStruct((t, LANES), F32)],
        scratch_shapes=[pltpu.VMEM((POOL_HALO, POOL_WIDTH), F32)],
        compiler_params=pltpu.CompilerParams(dimension_semantics=("arbitrary", "arbitrary"),
                                             vmem_limit_bytes=VMEM_LIMIT),
        name="mix",
    )(x2, on, u, gates, wupa, poolw, pscale, wupb, wout, gffn, wr_hi, wr_lo, br)


def _moe_kernel(h2_ref, comb_ref, x1_ref, wgu_ref, bgu_ref, wd_ref, bd_ref, gfin_ref, out_ref, acc_ref):
    e = pl.program_id(1)

    @pl.when(e == 0)
    def _():
        acc_ref[...] = jnp.zeros_like(acc_ref)

    gu = _dot(h2_ref[...], wgu_ref[...]) + bgu_ref[...]
    gate = jnp.minimum(gu[:, 0:D_FF], SWIGLU_LIMIT)
    up = jnp.clip(gu[:, D_FF:2 * D_FF], -SWIGLU_LIMIT, SWIGLU_LIMIT)
    act = gate * _sigmoid(SWIGLU_ALPHA * gate) * (up + 1.0)
    y = _dot(act.astype(BF16), wd_ref[...]) + bd_ref[...]
    comb = comb_ref[...]
    lane = lax.broadcasted_iota(jnp.int32, comb.shape, 1)
    c_e = jnp.sum(jnp.where(lane == e, comb, 0.0), axis=-1, keepdims=True)
    acc_ref[...] += c_e * y

    @pl.when(e == pl.num_programs(1) - 1)
    def _():
        out_ref[...] = _rms(x1_ref[...] + acc_ref[...], gfin_ref[...])


def _moe(h2, comb, x1, wgu, bgu, wd, bd, gfin):
    t = h2.shape[0]
    tm = MOE_TM
    return pl.pallas_call(
        _moe_kernel,
        grid=(t // tm, N_EXPERTS),
        in_specs=[pl.BlockSpec((tm, D_MODEL), lambda i, e: (i, 0)),
                  pl.BlockSpec((tm, LANES), lambda i, e: (i, 0)),
                  pl.BlockSpec((tm, D_MODEL), lambda i, e: (i, 0)),
                  pl.BlockSpec((None, D_MODEL, 2 * D_FF), lambda i, e: (e, 0, 0)),
                  pl.BlockSpec((None, 1, 2 * D_FF), lambda i, e: (e, 0, 0)),
                  pl.BlockSpec((None, D_FF, D_MODEL), lambda i, e: (e, 0, 0)),
                  pl.BlockSpec((None, 1, D_MODEL), lambda i, e: (e, 0, 0)),
                  pl.BlockSpec((1, D_MODEL), lambda i, e: (0, 0))],
        out_specs=pl.BlockSpec((tm, D_MODEL), lambda i, e: (i, 0)),
        out_shape=jax.ShapeDtypeStruct((t, D_MODEL), F32),
        scratch_shapes=[pltpu.VMEM((tm, D_MODEL), F32)],
        compiler_params=pltpu.CompilerParams(dimension_semantics=("arbitrary", "arbitrary"),
                                             vmem_limit_bytes=VMEM_LIMIT),
        name="moe",
    )(h2, comb, x1, wgu, bgu, wd, bd, gfin)


def kernel(x, g_mix, w_in, conv_w, a_log, dt_bias, dn_norm, w_up_a, pool_w, pool_scale, w_up_b, w_out, g_ffn,
           w_router, b_router, w_gate_up, b_gate_up, w_down, b_down, g_final):
    batch, seq, d = x.shape
    assert d == D_MODEL and seq % DN_LB == 0 and seq % MIX_TM == 0 and (batch * seq) % MOE_TM == 0
    assert g_mix.shape[0] == 1, "one layer"
    t = batch * seq
    x2 = x.reshape(t, d)

    w = w_in[0]
    o_z = 3 * DN_WIDTH
    o_b = o_z + DN_WIDTH
    o_a = o_b + DN_HEADS
    o_u = o_a + DN_HEADS
    o_g = o_u + POOL_WIDTH
    wqkv = w[:, 0:o_z].astype(BF16)
    wz = w[:, o_z:o_b].astype(BF16)
    wu = w[:, o_u:o_g].astype(BF16)
    wg = w[:, o_g:].astype(BF16)
    w_b = w[:, o_b:o_a]
    w_a = w[:, o_a:o_u]
    pad = jnp.zeros((d, LANES - DN_HEADS), F32)
    wba = jnp.concatenate([w_b, pad, w_a, pad], axis=1).astype(BF16)
    wbat = jnp.concatenate([w_b, w_a], axis=1).T.astype(BF16)

    qkv, z, u, gates, ba, bat = _in_proj(x2, g_mix, wqkv, wz, wu, wg, wba, wbat)

    lane_pad = lambda p: jnp.pad(p.reshape(1, DN_HEADS), ((0, 0), (0, LANES - DN_HEADS)))
    on = _deltanet(qkv, z, ba, bat, conv_w[0], lane_pad(a_log[0]), lane_pad(dt_bias[0]),
                   a_log[0].reshape(DN_HEADS, 1), dt_bias[0].reshape(DN_HEADS, 1),
                   jnp.tile(dn_norm[0], DN_HEADS).reshape(1, DN_WIDTH), batch, seq)

    wr = jnp.pad(w_router[0], ((0, 0), (0, LANES - N_EXPERTS)))
    wr_hi = wr.astype(BF16)
    wr_lo = (wr - wr_hi.astype(F32)).astype(BF16)
    br = jnp.pad(b_router[0].reshape(1, N_EXPERTS), ((0, 0), (0, LANES - N_EXPERTS)))
    x1, h2, comb = _mix(x2, on, u, gates, w_up_a[0].astype(BF16), pool_w[0].astype(BF16),
                        pool_scale[0].reshape(1, POOL_WIDTH), w_up_b[0].astype(BF16), w_out[0].astype(BF16),
                        g_ffn, wr_hi, wr_lo, br, batch, seq)

    out = _moe(h2, comb, x1, w_gate_up[0].astype(BF16), b_gate_up[0].reshape(N_EXPERTS, 1, 2 * D_FF),
               w_down[0].astype(BF16), b_down[0].reshape(N_EXPERTS, 1, D_MODEL), g_final.reshape(1, D_MODEL))
    return out.reshape(batch, seq, d)
```

```python
import functools

import jax
import jax.numpy as jnp
from jax import lax
from jax.experimental import pallas as pl
from jax.experimental.pallas import tpu as pltpu

F32 = jnp.float32
BF16 = jnp.bfloat16

D_MODEL = 1024
CHUNK = 64
DN_HEADS = 8
DN_HEAD_DIM = 64
DN_WIDTH = DN_HEADS * DN_HEAD_DIM
CONV_WIDTH = 4
POOL_GROUPS = 4
POOL_WINDOWS = (2, 4, 8, 16)
POOL_WIDTH = 512
POOL_GROUP_DIM = 128
POOL_HALO = 16
N_EXPERTS = 32
TOP_K = 4
D_FF = D_MODEL
SWIGLU_LIMIT = 7.0
SWIGLU_ALPHA = 1.702
NORM_EPS = 1e-6
LANES = 128
SUBLANES = 8
VMEM_LIMIT = 56 * 1024 * 1024

IN_TM = 256
DN_LB = 256
MIX_TM = 256
MOE_TG = 512
DSP_TM = 512
CMB_TM = 256


def _dot(a, b):
    return jnp.dot(a, b, preferred_element_type=F32)


def _dot_nt(a, b):
    return lax.dot_general(a, b, (((1,), (1,)), ((), ())), preferred_element_type=F32)


def _dot_tn(a, b):
    return lax.dot_general(a, b, (((0,), (0,)), ((), ())), preferred_element_type=F32)


def _split2(x):
    hi = x.astype(BF16)
    lo = (x - hi.astype(F32)).astype(BF16)
    return hi, lo


def _dot_exact_rhs(x, m):
    hi, lo = _split2(x)
    return _dot(hi, m) + _dot(lo, m)


def _dot_exact_lhs(m, x):
    hi, lo = _split2(x)
    return _dot(m, hi) + _dot(m, lo)


def _softplus(x):
    return jnp.maximum(x, 0.0) + jnp.log1p(jnp.exp(-jnp.abs(x)))


def _sigmoid(x):
    return 1.0 / (1.0 + jnp.exp(-x))


def _rms(x, g):
    return x * lax.rsqrt(jnp.mean(x * x, axis=-1, keepdims=True) + NORM_EPS) * g


def _in_proj_kernel(x_ref, g_ref, wqkv_ref, wz_ref, wu_ref, wg_ref, wba_ref, wbat_ref,
                    qkv_ref, z_ref, u_ref, gates_ref, ba_ref, bat_ref):
    hb = _rms(x_ref[...], g_ref[...]).astype(BF16)
    qkv_ref[...] = _dot(hb, wqkv_ref[...])
    z_ref[...] = _dot(hb, wz_ref[...])
    u_ref[...] = _dot(hb, wu_ref[...])
    gates_ref[...] = _dot(hb, wg_ref[...])
    ba_ref[...] = _dot(hb, wba_ref[...])
    bat_ref[...] = _dot_nt(wbat_ref[...], hb)


def _in_proj(x2, g_mix, wqkv, wz, wu, wg, wba, wbat):
    t = x2.shape[0]
    tm = IN_TM
    full = lambda a: pl.BlockSpec(a.shape, lambda i: (0, 0))
    row = lambda n: pl.BlockSpec((tm, n), lambda i: (i, 0))
    return pl.pallas_call(
        _in_proj_kernel,
        grid=(t // tm,),
        in_specs=[row(D_MODEL), full(g_mix), full(wqkv), full(wz), full(wu), full(wg), full(wba), full(wbat)],
        out_specs=[row(3 * DN_WIDTH), row(DN_WIDTH), row(POOL_WIDTH), row(2 * D_MODEL), row(2 * LANES),
                   pl.BlockSpec((2 * SUBLANES, tm), lambda i: (0, i))],
        out_shape=[jax.ShapeDtypeStruct((t, 3 * DN_WIDTH), F32),
                   jax.ShapeDtypeStruct((t, DN_WIDTH), F32),
                   jax.ShapeDtypeStruct((t, POOL_WIDTH), F32),
                   jax.ShapeDtypeStruct((t, 2 * D_MODEL), F32),
                   jax.ShapeDtypeStruct((t, 2 * LANES), F32),
                   jax.ShapeDtypeStruct((2 * SUBLANES, t), F32)],
        compiler_params=pltpu.CompilerParams(dimension_semantics=("arbitrary",), vmem_limit_bytes=VMEM_LIMIT),
        name="in_proj",
    )(x2, g_mix, wqkv, wz, wu, wg, wba, wbat)


def _deltanet_kernel(qkv_ref, z_ref, ba_ref, bat_ref, convw_ref, alog_r_ref, dtb_r_ref, alog_c_ref, dtb_c_ref,
                     dnw_ref, o_ref,
                     s_ref, carry_ref, qn_ref, kn_ref, kbe_ref, vb_ref, qg_ref, kd_ref,
                     xbeta_ref, xgc_ref, xgl_ref, gr_ref, oacc_ref):
    lb = qkv_ref.shape[0]
    n_chunks = lb // CHUNK

    @pl.when(pl.program_id(1) == 0)
    def _():
        s_ref[...] = jnp.zeros_like(s_ref)
        carry_ref[...] = jnp.zeros_like(carry_ref)

    blk = qkv_ref[...]
    carry = carry_ref[...]
    cw = convw_ref[...]
    row8 = lax.broadcasted_iota(jnp.int32, (SUBLANES, 3 * DN_WIDTH), 0)
    acc = blk * cw[CONV_WIDTH - 1:CONV_WIDTH]
    head = blk[0:SUBLANES] * cw[CONV_WIDTH - 1:CONV_WIDTH]
    for s in range(1, CONV_WIDTH):
        w_s = cw[CONV_WIDTH - 1 - s:CONV_WIDTH - s]
        rolled = pltpu.roll(blk, s, 0)
        acc = acc + rolled * w_s
        head = head + jnp.where(row8 < s, pltpu.roll(carry, s, 0), rolled[0:SUBLANES]) * w_s
    carry_ref[...] = blk[lb - SUBLANES:lb]
    act = jnp.concatenate([head, acc[SUBLANES:lb]], axis=0)
    act = act * _sigmoid(act)
    q = act[:, 0:DN_WIDTH]
    k = act[:, DN_WIDTH:2 * DN_WIDTH]
    v = act[:, 2 * DN_WIDTH:3 * DN_WIDTH]

    hr = lax.broadcasted_iota(jnp.int32, (DN_WIDTH, DN_WIDTH), 0) // DN_HEAD_DIM
    hc = lax.broadcasted_iota(jnp.int32, (DN_WIDTH, DN_WIDTH), 1) // DN_HEAD_DIM
    head_ones = (hr == hc).astype(BF16)
    qn = q * lax.rsqrt(_dot_exact_rhs(q * q, head_ones) + NORM_EPS) * (DN_HEAD_DIM ** -0.5)
    kn = k * lax.rsqrt(_dot_exact_rhs(k * k, head_ones) + NORM_EPS)

    ba = ba_ref[...]
    beta_c = _sigmoid(ba[:, 0:LANES])
    g_c = -jnp.exp(alog_r_ref[...]) * _softplus(ba[:, LANES:2 * LANES] + dtb_r_ref[...])
    lane = lax.broadcasted_iota(jnp.int32, (lb, LANES), 1)
    g_c = jnp.where(lane < DN_HEADS, g_c, 0.0)
    bat = bat_ref[...]
    g_r = -jnp.exp(alog_c_ref[...]) * _softplus(bat[SUBLANES:2 * SUBLANES] + dtb_c_ref[...])

    rr = lax.broadcasted_iota(jnp.int32, (lb, lb), 0)
    cc = lax.broadcasted_iota(jnp.int32, (lb, lb), 1)
    same = (rr // CHUNK) == (cc // CHUNK)
    tril = (same & (cc <= rr)).astype(BF16)
    triu = (same & (rr <= cc)).astype(BF16)
    chunk_ones = same.astype(BF16)
    gc = _dot_exact_lhs(tril, g_c)
    gtot = _dot_exact_lhs(chunk_ones, g_c)
    gr = _dot_exact_rhs(g_r, triu)
    for c in range(n_chunks):
        gr_ref[c] = gr[:, c * CHUNK:(c + 1) * CHUNK]

    er = lax.broadcasted_iota(jnp.int32, (LANES, DN_WIDTH), 0)
    ec = lax.broadcasted_iota(jnp.int32, (LANES, DN_WIDTH), 1) // DN_HEAD_DIM
    expand = (er == ec).astype(BF16)
    x_beta = _dot_exact_rhs(beta_c, expand)
    x_gc = _dot_exact_rhs(gc, expand)
    x_gtot = _dot_exact_rhs(gtot, expand)
    x_eg = jnp.exp(x_gc)
    xbeta_ref[...] = x_beta
    xgc_ref[...] = x_gc
    xgl_ref[...] = jnp.exp(x_gtot)
    qn_ref[...] = qn.astype(BF16)
    kn_ref[...] = kn.astype(BF16)
    kbe_ref[...] = (kn * (x_beta * x_eg)).astype(BF16)
    vb_ref[...] = (v * x_beta).astype(BF16)
    qg_ref[...] = (qn * x_eg).astype(BF16)
    kd_ref[...] = (kn * jnp.exp(x_gtot - x_gc)).astype(BF16)

    ci = lax.broadcasted_iota(jnp.int32, (CHUNK, CHUNK), 0)
    cj = lax.broadcasted_iota(jnp.int32, (CHUNK, CHUNK), 1)
    eye = (ci == cj).astype(F32)

    heads = range(DN_HEADS)
    lanes = [pl.ds(h * DN_HEAD_DIM, DN_HEAD_DIM) for h in heads]
    stack = lambda top, bot: jnp.concatenate([top, bot], axis=0)

    def chunk_body(c, carry_unused):
        rows = pl.ds(pl.multiple_of(c * CHUNK, CHUNK), CHUNK)
        kb = [kn_ref[rows, lanes[h]] for h in heads]
        kq = [_dot_nt(stack(kb[h], qn_ref[rows, lanes[h]]), kb[h]) for h in heads]
        decay = [jnp.exp(jnp.where(ci >= cj, xgc_ref[rows, lanes[h]] - gr_ref[c, h:h + 1, :], -jnp.inf))
                 for h in heads]
        a = [jnp.where(ci > cj, xbeta_ref[rows, lanes[h]] * kq[h][0:CHUNK] * decay[h], 0.0) for h in heads]
        qkd = [(kq[h][CHUNK:2 * CHUNK] * decay[h]).astype(BF16) for h in heads]
        t_inv = [eye - a[h] for h in heads]
        pw = [a[h].astype(BF16) for h in heads]
        pw = [_dot(pw[h], pw[h]).astype(BF16) for h in heads]
        for _ in range(4):
            r = [_dot(stack(t_inv[h].astype(BF16), pw[h]), pw[h]) for h in heads]
            t_inv = [t_inv[h] + r[h][0:CHUNK] for h in heads]
            pw = [r[h][CHUNK:2 * CHUNK].astype(BF16) for h in heads]
        t_inv = [t_inv[h] + _dot(t_inv[h].astype(BF16), pw[h]) for h in heads]
        tb = [t_inv[h].astype(BF16) for h in heads]
        w = [_dot(tb[h], kbe_ref[rows, lanes[h]]).astype(BF16) for h in heads]
        u = [_dot(tb[h], vb_ref[rows, lanes[h]]) for h in heads]
        s = [s_ref[h] for h in heads]
        ws = [_dot(stack(w[h], qg_ref[rows, lanes[h]]), s[h].astype(BF16)) for h in heads]
        vnb = [(u[h] - ws[h][0:CHUNK]).astype(BF16) for h in heads]
        for h in heads:
            oacc_ref[rows, lanes[h]] = ws[h][CHUNK:2 * CHUNK] + _dot(qkd[h], vnb[h])
        for h in heads:
            s_ref[h] = s[h] * xgl_ref[rows, lanes[h]] + _dot_tn(kd_ref[rows, lanes[h]], vnb[h])
        return carry_unused

    lax.fori_loop(0, n_chunks, chunk_body, 0)

    o = oacc_ref[...]
    ms = _dot_exact_rhs(o * o, head_ones) * (1.0 / DN_HEAD_DIM)
    z = z_ref[...]
    o_ref[...] = (o * lax.rsqrt(ms + NORM_EPS) * dnw_ref[...] * (z * _sigmoid(z))).astype(o_ref.dtype)


def _deltanet(qkv, z, ba, bat, conv_w, alog_r, dtb_r, alog_c, dtb_c, dnw, batch, seq):
    lb = DN_LB
    nb = seq // lb
    full = lambda a: pl.BlockSpec(a.shape, lambda b, j: (0,) * a.ndim)
    row = lambda n: pl.BlockSpec((lb, n), lambda b, j: (b * nb + j, 0))
    return pl.pallas_call(
        _deltanet_kernel,
        grid=(batch, nb),
        in_specs=[row(3 * DN_WIDTH), row(DN_WIDTH), row(2 * LANES),
                  pl.BlockSpec((2 * SUBLANES, lb), lambda b, j: (0, b * nb + j)),
                  full(conv_w), full(alog_r), full(dtb_r), full(alog_c), full(dtb_c), full(dnw)],
        out_specs=row(DN_WIDTH),
        out_shape=jax.ShapeDtypeStruct((batch * seq, DN_WIDTH), BF16),
        scratch_shapes=[
            pltpu.VMEM((DN_HEADS, DN_HEAD_DIM, DN_HEAD_DIM), F32),
            pltpu.VMEM((SUBLANES, 3 * DN_WIDTH), F32),
            pltpu.VMEM((lb, DN_WIDTH), BF16),
            pltpu.VMEM((lb, DN_WIDTH), BF16),
            pltpu.VMEM((lb, DN_WIDTH), BF16),
            pltpu.VMEM((lb, DN_WIDTH), BF16),
            pltpu.VMEM((lb, DN_WIDTH), BF16),
            pltpu.VMEM((lb, DN_WIDTH), BF16),
            pltpu.VMEM((lb, DN_WIDTH), F32),
            pltpu.VMEM((lb, DN_WIDTH), F32),
            pltpu.VMEM((lb, DN_WIDTH), F32),
            pltpu.VMEM((lb // CHUNK, DN_HEADS, CHUNK), F32),
            pltpu.VMEM((lb, DN_WIDTH), F32),
        ],
        compiler_params=pltpu.CompilerParams(dimension_semantics=("arbitrary", "arbitrary"),
                                             vmem_limit_bytes=VMEM_LIMIT),
        name="deltanet",
    )(qkv, z, ba, bat, conv_w, alog_r, dtb_r, alog_c, dtb_c, dnw)


def _mix_kernel(x_ref, on_ref, u_ref, gates_ref, wupa_ref, poolw_ref, pscale_ref, wupb_ref, wout_ref,
                gffn_ref, wr_hi_ref, wr_lo_ref, br_ref,
                x1_ref, h2_ref, route_ref, counts_ref, ucarry_ref, ecount_ref):
    tm = x_ref.shape[0]
    j = pl.program_id(1)

    @pl.when(j == 0)
    def _():
        ucarry_ref[...] = jnp.zeros_like(ucarry_ref)

    @pl.when((j == 0) & (pl.program_id(0) == 0))
    def _():
        ecount_ref[...] = jnp.zeros_like(ecount_ref)

    u = u_ref[...]
    ext = jnp.concatenate([ucarry_ref[...], u], axis=0)
    ucarry_ref[...] = u[tm - POOL_HALO:tm]
    t_pos = j * tm + lax.broadcasted_iota(jnp.int32, (tm, POOL_GROUP_DIM), 0)
    ys = []
    for g in range(POOL_GROUPS):
        s = ext[:, g * POOL_GROUP_DIM:(g + 1) * POOL_GROUP_DIM]
        shift = 1
        while shift < POOL_WINDOWS[g]:
            s = s + pltpu.roll(s, shift, 0)
            shift *= 2
        count = jnp.minimum(t_pos + 1, POOL_WINDOWS[g]).astype(F32)
        pooled = s[POOL_HALO:] / count - u[:, g * POOL_GROUP_DIM:(g + 1) * POOL_GROUP_DIM]
        ys.append(_dot(pooled.astype(BF16), poolw_ref[g]))
    yb = jnp.concatenate(ys, axis=-1) * pscale_ref[...]
    y_b = _dot(yb.astype(BF16), wupb_ref[...])
    y_a = _dot(on_ref[...], wupa_ref[...])
    gates = gates_ref[...]
    merged = _sigmoid(gates[:, 0:D_MODEL]) * y_a + _sigmoid(gates[:, D_MODEL:2 * D_MODEL]) * y_b
    x1 = x_ref[...] + _dot(merged.astype(BF16), wout_ref[...])
    x1_ref[...] = x1

    h2 = _rms(x1, gffn_ref[...])
    h2_ref[...] = h2

    hi, lo = _split2(h2)
    logits = _dot(hi, wr_hi_ref[...]) + _dot(lo, wr_hi_ref[...]) + _dot(hi, wr_lo_ref[...]) + br_ref[...]
    lane = lax.broadcasted_iota(jnp.int32, (tm, LANES), 1)
    lg = jnp.where(lane < N_EXPERTS, logits, -jnp.inf)
    vals, idxs, sels = [], [], []
    for _ in range(TOP_K):
        m = jnp.max(lg, axis=-1, keepdims=True)
        idx = jnp.min(jnp.where(lg == m, lane, LANES), axis=-1, keepdims=True)
        sel = lane == idx
        vals.append(m)
        idxs.append(idx)
        sels.append(sel)
        lg = jnp.where(sel, -jnp.inf, lg)
    es = [jnp.exp(vk - vals[0]) for vk in vals]
    denom = es[0] + es[1] + es[2] + es[3]

    chosen = jnp.zeros((tm, LANES), F32)
    for sel in sels:
        chosen = chosen + jnp.where(sel, 1.0, 0.0)
    ri = lax.broadcasted_iota(jnp.int32, (tm, tm), 0)
    rj = lax.broadcasted_iota(jnp.int32, (tm, tm), 1)
    before = (rj < ri).astype(BF16)
    pos = ecount_ref[...] + _dot(before, chosen.astype(BF16))
    ecount_ref[...] = ecount_ref[...] + jnp.sum(chosen, axis=0, keepdims=True)
    counts_ref[...] = ecount_ref[...]

    route = jnp.zeros((tm, LANES), F32)
    for kk in range(TOP_K):
        rank = jnp.sum(jnp.where(sels[kk], pos, 0.0), axis=-1, keepdims=True)
        route = jnp.where(lane == kk, rank, route)
        route = jnp.where(lane == TOP_K + kk, idxs[kk].astype(F32), route)
        route = jnp.where(lane == 2 * TOP_K + kk, es[kk] / denom, route)
    route_ref[...] = route


def _mix(x2, on, u, gates, wupa, poolw, pscale, wupb, wout, gffn, wr_hi, wr_lo, br, batch, seq):
    tm = MIX_TM
    nb = seq // tm
    t = batch * seq
    full = lambda a: pl.BlockSpec(a.shape, lambda b, j: (0,) * a.ndim)
    row = lambda n: pl.BlockSpec((tm, n), lambda b, j: (b * nb + j, 0))
    return pl.pallas_call(
        _mix_kernel,
        grid=(batch, nb),
        in_specs=[row(D_MODEL), row(DN_WIDTH), row(POOL_WIDTH), row(2 * D_MODEL),
                  full(wupa), full(poolw), full(pscale), full(wupb), full(wout), full(gffn),
                  full(wr_hi), full(wr_lo), full(br)],
        out_specs=[row(D_MODEL), row(D_MODEL), row(LANES), pl.BlockSpec((1, LANES), lambda b, j: (0, 0))],
        out_shape=[jax.ShapeDtypeStruct((t, D_MODEL), F32),
                   jax.ShapeDtypeStruct((t, D_MODEL), F32),
                   jax.ShapeDtypeStruct((t, LANES), F32),
                   jax.ShapeDtypeStruct((1, LANES), F32)],
        scratch_shapes=[pltpu.VMEM((POOL_HALO, POOL_WIDTH), F32), pltpu.VMEM((1, LANES), F32)],
        compiler_params=pltpu.CompilerParams(dimension_semantics=("arbitrary", "arbitrary"),
                                             vmem_limit_bytes=VMEM_LIMIT),
        name="mix",
    )(x2, on, u, gates, wupa, poolw, pscale, wupb, wout, gffn, wr_hi, wr_lo, br)


def _routing_tables(route, counts, tg, n_tiles):
    cnt = counts[0, 0:N_EXPERTS].astype(jnp.int32)
    padded = (cnt + (tg - 1)) // tg * tg
    ends = jnp.cumsum(padded)
    offs = ends - padded
    rank = route[:, 0:TOP_K].astype(jnp.int32)
    eidx = route[:, TOP_K:2 * TOP_K].astype(jnp.int32)
    dest = offs[eidx] + rank
    n_used = ends[N_EXPERTS - 1] // tg
    tile_start = jnp.arange(n_tiles, dtype=jnp.int32) * tg
    tile_expert = jnp.sum((tile_start[:, None] >= ends[None, :]).astype(jnp.int32), axis=1)
    tile_expert = jnp.minimum(tile_expert, N_EXPERTS - 1)
    last = tile_expert[jnp.maximum(n_used - 1, 0)]
    tile_expert = jnp.where(jnp.arange(n_tiles) < n_used, tile_expert, last)
    pad_start = offs + cnt
    pad_len = padded - cnt
    return (dest, tile_expert.astype(jnp.int32), n_used.reshape(1).astype(jnp.int32),
            pad_start.astype(jnp.int32), pad_len.astype(jnp.int32))


def _dispatch_kernel(pad_start_ref, pad_len_ref, nu_ref, dest_ref, h2_ref, xs_ref, zero_ref, sem, zsem):
    tm = h2_ref.shape[0]
    tg = zero_ref.shape[0]
    n_tiles = xs_ref.shape[0] // tg

    @pl.when(pl.program_id(0) == 0)
    def _():
        zero_ref[...] = jnp.zeros_like(zero_ref)

        def pad_copy(d):
            return pltpu.make_async_copy(zero_ref.at[pl.ds(0, 1)], xs_ref.at[pl.ds(d, 1)], zsem)

        def tile_copy(i):
            return pltpu.make_async_copy(zero_ref, xs_ref.at[pl.ds(pl.multiple_of(i * tg, tg), tg)], zsem)

        def per_expert(e, c):
            lax.fori_loop(0, pad_len_ref[e], lambda r, cc: (pad_copy(pad_start_ref[e] + r).start(), cc)[1], 0)
            lax.fori_loop(0, pad_len_ref[e], lambda r, cc: (pad_copy(0).wait(), cc)[1], 0)
            return c

        lax.fori_loop(0, N_EXPERTS, per_expert, 0)
        lax.fori_loop(nu_ref[0], n_tiles, lambda i, cc: (tile_copy(i).start(), cc)[1], 0)
        lax.fori_loop(nu_ref[0], n_tiles, lambda i, cc: (tile_copy(0).wait(), cc)[1], 0)

    def row_copy(r, d):
        return pltpu.make_async_copy(h2_ref.at[pl.ds(r, 1)], xs_ref.at[pl.ds(d, 1)], sem)

    def start(r, c):
        for kk in range(TOP_K):
            row_copy(r, dest_ref[0, 0, r * TOP_K + kk]).start()
        return c

    lax.fori_loop(0, tm, start, 0)
    for kk in range(TOP_K):
        pltpu.make_async_copy(h2_ref, xs_ref.at[pl.ds(0, tm)], sem).wait()


def _dispatch(pad_start, pad_len, n_used, dest3, h2, n_rows):
    t = h2.shape[0]
    tm = DSP_TM
    return pl.pallas_call(
        _dispatch_kernel,
        grid_spec=pltpu.PrefetchScalarGridSpec(
            num_scalar_prefetch=3,
            grid=(t // tm,),
            in_specs=[pl.BlockSpec((1, 1, tm * TOP_K), lambda i, ps, pn, nu: (i, 0, 0), memory_space=pltpu.SMEM),
                      pl.BlockSpec((tm, D_MODEL), lambda i, ps, pn, nu: (i, 0))],
            out_specs=pl.BlockSpec(memory_space=pl.ANY),
            scratch_shapes=[pltpu.VMEM((MOE_TG, D_MODEL), F32), pltpu.SemaphoreType.DMA,
                            pltpu.SemaphoreType.DMA]),
        out_shape=jax.ShapeDtypeStruct((n_rows, D_MODEL), F32),
        compiler_params=pltpu.CompilerParams(dimension_semantics=("arbitrary",), vmem_limit_bytes=VMEM_LIMIT),
        name="dispatch",
    )(pad_start, pad_len, n_used, dest3, h2)


def _experts_kernel(te_ref, nu_ref, xs_ref, wgu_ref, bgu_ref, wd_ref, bd_ref, ys_ref):
    i = pl.program_id(0)

    @pl.when(i < nu_ref[0])
    def _():
        gu = _dot(xs_ref[...].astype(BF16), wgu_ref[...]) + bgu_ref[...]
        gate = jnp.minimum(gu[:, 0:D_FF], SWIGLU_LIMIT)
        up = jnp.clip(gu[:, D_FF:2 * D_FF], -SWIGLU_LIMIT, SWIGLU_LIMIT)
        act = gate * _sigmoid(SWIGLU_ALPHA * gate) * (up + 1.0)
        ys_ref[...] = _dot(act.astype(BF16), wd_ref[...]) + bd_ref[...]

    @pl.when(i >= nu_ref[0])
    def _():
        ys_ref[...] = jnp.zeros_like(ys_ref)


def _experts(tile_expert, n_used, xs, wgu, bgu, wd, bd):
    tg = MOE_TG
    n_tiles = xs.shape[0] // tg
    tile = lambda i, te, nu: (jnp.minimum(i, nu[0] - 1), 0)
    expert = lambda i, te, nu: (te[i], 0, 0)
    return pl.pallas_call(
        _experts_kernel,
        grid_spec=pltpu.PrefetchScalarGridSpec(
            num_scalar_prefetch=2,
            grid=(n_tiles,),
            in_specs=[pl.BlockSpec((tg, D_MODEL), tile),
                      pl.BlockSpec((None, D_MODEL, 2 * D_FF), expert),
                      pl.BlockSpec((None, 1, 2 * D_FF), expert),
                      pl.BlockSpec((None, D_FF, D_MODEL), expert),
                      pl.BlockSpec((None, 1, D_MODEL), expert)],
            out_specs=pl.BlockSpec((tg, D_MODEL), lambda i, te, nu: (i, 0))),
        out_shape=jax.ShapeDtypeStruct((xs.shape[0], D_MODEL), F32),
        compiler_params=pltpu.CompilerParams(dimension_semantics=("arbitrary",), vmem_limit_bytes=VMEM_LIMIT),
        name="experts",
    )(tile_expert, n_used, xs, wgu, bgu, wd, bd)


def _combine_kernel(dest_ref, route_ref, x1_ref, gfin_ref, ys_ref, out_ref, ybuf_ref, sem):
    tm = x1_ref.shape[0]

    def row_copy(r, kk, d):
        return pltpu.make_async_copy(ys_ref.at[pl.ds(d, 1)], ybuf_ref.at[kk, pl.ds(r, 1)], sem)

    def start(r, c):
        for kk in range(TOP_K):
            row_copy(r, kk, dest_ref[0, 0, r * TOP_K + kk]).start()
        return c

    lax.fori_loop(0, tm, start, 0)
    for kk in range(TOP_K):
        pltpu.make_async_copy(ys_ref.at[pl.ds(0, tm)], ybuf_ref.at[kk], sem).wait()
    route = route_ref[...]
    acc = x1_ref[...]
    for kk in range(TOP_K):
        acc = acc + route[:, 2 * TOP_K + kk:2 * TOP_K + kk + 1] * ybuf_ref[kk]
    out_ref[...] = _rms(acc, gfin_ref[...])


def _combine(dest3, route, x1, gfin, ys):
    t = x1.shape[0]
    tm = CMB_TM
    return pl.pallas_call(
        _combine_kernel,
        grid=(t // tm,),
        in_specs=[pl.BlockSpec((1, 1, tm * TOP_K), lambda i: (i, 0, 0), memory_space=pltpu.SMEM),
                  pl.BlockSpec((tm, LANES), lambda i: (i, 0)),
                  pl.BlockSpec((tm, D_MODEL), lambda i: (i, 0)),
                  pl.BlockSpec((1, D_MODEL), lambda i: (0, 0)),
                  pl.BlockSpec(memory_space=pl.ANY)],
        out_specs=pl.BlockSpec((tm, D_MODEL), lambda i: (i, 0)),
        out_shape=jax.ShapeDtypeStruct((t, D_MODEL), F32),
        scratch_shapes=[pltpu.VMEM((TOP_K, tm, D_MODEL), F32), pltpu.SemaphoreType.DMA],
        compiler_params=pltpu.CompilerParams(dimension_semantics=("arbitrary",), vmem_limit_bytes=VMEM_LIMIT),
        name="combine",
    )(dest3, route, x1, gfin, ys)


def kernel(x, g_mix, w_in, conv_w, a_log, dt_bias, dn_norm, w_up_a, pool_w, pool_scale, w_up_b, w_out, g_ffn,
           w_router, b_router, w_gate_up, b_gate_up, w_down, b_down, g_final):
    batch, seq, d = x.shape
    assert d == D_MODEL and seq % DN_LB == 0 and seq % MIX_TM == 0
    assert (batch * seq) % DSP_TM == 0 and (batch * seq * TOP_K) % MOE_TG == 0
    assert g_mix.shape[0] == 1, "one layer"
    t = batch * seq
    x2 = x.reshape(t, d)

    w = w_in[0]
    o_z = 3 * DN_WIDTH
    o_b = o_z + DN_WIDTH
    o_a = o_b + DN_HEADS
    o_u = o_a + DN_HEADS
    o_g = o_u + POOL_WIDTH
    wqkv = w[:, 0:o_z].astype(BF16)
    wz = w[:, o_z:o_b].astype(BF16)
    wu = w[:, o_u:o_g].astype(BF16)
    wg = w[:, o_g:].astype(BF16)
    w_b = w[:, o_b:o_a]
    w_a = w[:, o_a:o_u]
    pad = jnp.zeros((d, LANES - DN_HEADS), F32)
    wba = jnp.concatenate([w_b, pad, w_a, pad], axis=1).astype(BF16)
    wbat = jnp.concatenate([w_b, w_a], axis=1).T.astype(BF16)

    qkv, z, u, gates, ba, bat = _in_proj(x2, g_mix, wqkv, wz, wu, wg, wba, wbat)

    lane_pad = lambda p: jnp.pad(p.reshape(1, DN_HEADS), ((0, 0), (0, LANES - DN_HEADS)))
    on = _deltanet(qkv, z, ba, bat, conv_w[0], lane_pad(a_log[0]), lane_pad(dt_bias[0]),
                   a_log[0].reshape(DN_HEADS, 1), dt_bias[0].reshape(DN_HEADS, 1),
                   jnp.tile(dn_norm[0], DN_HEADS).reshape(1, DN_WIDTH), batch, seq)

    wr = jnp.pad(w_router[0], ((0, 0), (0, LANES - N_EXPERTS)))
    wr_hi = wr.astype(BF16)
    wr_lo = (wr - wr_hi.astype(F32)).astype(BF16)
    br = jnp.pad(b_router[0].reshape(1, N_EXPERTS), ((0, 0), (0, LANES - N_EXPERTS)))
    x1, h2, route, counts = _mix(x2, on, u, gates, w_up_a[0].astype(BF16), pool_w[0].astype(BF16),
                                  pool_scale[0].reshape(1, POOL_WIDTH), w_up_b[0].astype(BF16),
                                  w_out[0].astype(BF16), g_ffn, wr_hi, wr_lo, br, batch, seq)

    n_tiles = t * TOP_K // MOE_TG + N_EXPERTS
    dest, tile_expert, n_used, pad_start, pad_len = _routing_tables(route, counts, MOE_TG, n_tiles)
    xs = _dispatch(pad_start, pad_len, n_used, dest.reshape(t // DSP_TM, 1, DSP_TM * TOP_K), h2, n_tiles * MOE_TG)
    ys = _experts(tile_expert, n_used, xs, w_gate_up[0].astype(BF16),
                  b_gate_up[0].reshape(N_EXPERTS, 1, 2 * D_FF), w_down[0].astype(BF16),
                  b_down[0].reshape(N_EXPERTS, 1, D_MODEL))
    out = _combine(dest.reshape(t // CMB_TM, 1, CMB_TM * TOP_K), route, x1, g_final.reshape(1, D_MODEL), ys)
    return out.reshape(batch, seq, d)
```

```python
import functools

import jax
import jax.numpy as jnp
from jax import lax
from jax.experimental import pallas as pl
from jax.experimental.pallas import tpu as pltpu

F32 = jnp.float32
BF16 = jnp.bfloat16

D_MODEL = 1024
CHUNK = 64
DN_HEADS = 8
DN_HEAD_DIM = 64
DN_WIDTH = DN_HEADS * DN_HEAD_DIM
CONV_WIDTH = 4
POOL_GROUPS = 4
POOL_WINDOWS = (2, 4, 8, 16)
POOL_WIDTH = 512
POOL_GROUP_DIM = 128
POOL_HALO = 16
N_EXPERTS = 32
TOP_K = 4
D_FF = D_MODEL
SWIGLU_LIMIT = 7.0
SWIGLU_ALPHA = 1.702
NORM_EPS = 1e-6
LANES = 128
SUBLANES = 8
VMEM_LIMIT = 56 * 1024 * 1024

IN_TM = 256
DN_LB = 256
INTRA_GROUP = 2
MIX_TM = 256
MOE_TG = 512
DSP_TM = 512
CMB_TM = 256


def _dot(a, b):
    return jnp.dot(a, b, preferred_element_type=F32)


def _dot_nt(a, b):
    return lax.dot_general(a, b, (((1,), (1,)), ((), ())), preferred_element_type=F32)


def _dot_tn(a, b):
    return lax.dot_general(a, b, (((0,), (0,)), ((), ())), preferred_element_type=F32)


def _split2(x):
    hi = x.astype(BF16)
    lo = (x - hi.astype(F32)).astype(BF16)
    return hi, lo


def _dot_exact_rhs(x, m):
    hi, lo = _split2(x)
    return _dot(hi, m) + _dot(lo, m)


def _dot_exact_lhs(m, x):
    hi, lo = _split2(x)
    return _dot(m, hi) + _dot(m, lo)


def _softplus(x):
    return jnp.maximum(x, 0.0) + jnp.log1p(jnp.exp(-jnp.abs(x)))


def _sigmoid(x):
    return 1.0 / (1.0 + jnp.exp(-x))


def _rms(x, g):
    return x * lax.rsqrt(jnp.mean(x * x, axis=-1, keepdims=True) + NORM_EPS) * g


ROW_SUB = D_MODEL // LANES


def _row_tile(ref, r):
    return ref.at[pl.ds(pl.multiple_of(r * ROW_SUB, ROW_SUB), ROW_SUB)]


def _store_row_tiles(ref, x):
    for j in range(ROW_SUB):
        ref[pl.ds(j, x.shape[0], stride=ROW_SUB), :] = x[:, j * LANES:(j + 1) * LANES]


def _load_row_tiles(ref):
    n = ref.shape[0] // ROW_SUB
    return jnp.concatenate([ref[pl.ds(j, n, stride=ROW_SUB), :] for j in range(ROW_SUB)], axis=1)


def _in_proj_kernel(x_ref, g_ref, wqkv_ref, wz_ref, wu_ref, wg_ref, wba_ref, wbat_ref,
                    qkv_ref, z_ref, u_ref, gates_ref, ba_ref, bat_ref):
    hb = _rms(x_ref[...], g_ref[...]).astype(BF16)
    qkv_ref[...] = _dot(hb, wqkv_ref[...])
    z_ref[...] = _dot(hb, wz_ref[...])
    u_ref[...] = _dot(hb, wu_ref[...])
    gates_ref[...] = _dot(hb, wg_ref[...])
    ba_ref[...] = _dot(hb, wba_ref[...])
    bat_ref[...] = _dot_nt(wbat_ref[...], hb)


def _in_proj(x2, g_mix, wqkv, wz, wu, wg, wba, wbat):
    t = x2.shape[0]
    tm = IN_TM
    full = lambda a: pl.BlockSpec(a.shape, lambda i: (0, 0))
    row = lambda n: pl.BlockSpec((tm, n), lambda i: (i, 0))
    return pl.pallas_call(
        _in_proj_kernel,
        grid=(t // tm,),
        in_specs=[row(D_MODEL), full(g_mix), full(wqkv), full(wz), full(wu), full(wg), full(wba), full(wbat)],
        out_specs=[row(3 * DN_WIDTH), row(DN_WIDTH), row(POOL_WIDTH), row(2 * D_MODEL), row(2 * LANES),
                   pl.BlockSpec((2 * SUBLANES, tm), lambda i: (0, i))],
        out_shape=[jax.ShapeDtypeStruct((t, 3 * DN_WIDTH), F32),
                   jax.ShapeDtypeStruct((t, DN_WIDTH), F32),
                   jax.ShapeDtypeStruct((t, POOL_WIDTH), F32),
                   jax.ShapeDtypeStruct((t, 2 * D_MODEL), F32),
                   jax.ShapeDtypeStruct((t, 2 * LANES), F32),
                   jax.ShapeDtypeStruct((2 * SUBLANES, t), F32)],
        compiler_params=pltpu.CompilerParams(dimension_semantics=("arbitrary",), vmem_limit_bytes=VMEM_LIMIT),
        name="in_proj",
    )(x2, g_mix, wqkv, wz, wu, wg, wba, wbat)


def _deltanet_kernel(qkv_ref, z_ref, ba_ref, bat_ref, convw_ref, alog_r_ref, dtb_r_ref, alog_c_ref, dtb_c_ref,
                     dnw_ref, o_ref,
                     s_ref, carry_ref, qn_ref, kn_ref, kbe_ref, vb_ref, qg_ref, kd_ref,
                     xbeta_ref, xgc_ref, xgl_ref, gr_ref, oacc_ref, qkd_ref, wq_ref, u_ref, kdt_ref, gl_ref):
    lb = qkv_ref.shape[0]
    n_chunks = lb // CHUNK

    @pl.when(pl.program_id(1) == 0)
    def _():
        s_ref[...] = jnp.zeros_like(s_ref)
        carry_ref[0:SUBLANES, :] = jnp.zeros((SUBLANES, carry_ref.shape[1]), F32)

    blk = qkv_ref[...]
    cw = convw_ref[...]
    carry_ref[SUBLANES:SUBLANES + lb, :] = blk
    act = blk * cw[CONV_WIDTH - 1:CONV_WIDTH]
    for s in range(1, CONV_WIDTH):
        act = act + carry_ref[pl.ds(SUBLANES - s, lb), :] * cw[CONV_WIDTH - 1 - s:CONV_WIDTH - s]
    carry_ref[0:SUBLANES, :] = blk[lb - SUBLANES:lb]
    act = act * _sigmoid(act)
    q = act[:, 0:DN_WIDTH]
    k = act[:, DN_WIDTH:2 * DN_WIDTH]
    v = act[:, 2 * DN_WIDTH:3 * DN_WIDTH]

    hr = lax.broadcasted_iota(jnp.int32, (DN_WIDTH, DN_WIDTH), 0) // DN_HEAD_DIM
    hc = lax.broadcasted_iota(jnp.int32, (DN_WIDTH, DN_WIDTH), 1) // DN_HEAD_DIM
    head_ones = (hr == hc).astype(BF16)
    qn = q * lax.rsqrt(_dot_exact_rhs(q * q, head_ones) + NORM_EPS) * (DN_HEAD_DIM ** -0.5)
    kn = k * lax.rsqrt(_dot_exact_rhs(k * k, head_ones) + NORM_EPS)

    ba = ba_ref[...]
    beta_c = _sigmoid(ba[:, 0:LANES])
    g_c = -jnp.exp(alog_r_ref[...]) * _softplus(ba[:, LANES:2 * LANES] + dtb_r_ref[...])
    lane = lax.broadcasted_iota(jnp.int32, (lb, LANES), 1)
    g_c = jnp.where(lane < DN_HEADS, g_c, 0.0)
    bat = bat_ref[...]
    g_r = -jnp.exp(alog_c_ref[...]) * _softplus(bat[SUBLANES:2 * SUBLANES] + dtb_c_ref[...])

    rr = lax.broadcasted_iota(jnp.int32, (lb, lb), 0)
    cc = lax.broadcasted_iota(jnp.int32, (lb, lb), 1)
    same = (rr // CHUNK) == (cc // CHUNK)
    tril = (same & (cc <= rr)).astype(BF16)
    triu = (same & (rr <= cc)).astype(BF16)
    chunk_ones = same.astype(BF16)
    gc = _dot_exact_lhs(tril, g_c)
    gtot = _dot_exact_lhs(chunk_ones, g_c)
    gr = _dot_exact_rhs(g_r, triu)
    for c in range(n_chunks):
        gr_ref[c] = gr[:, c * CHUNK:(c + 1) * CHUNK]

    er = lax.broadcasted_iota(jnp.int32, (LANES, DN_WIDTH), 0)
    ec = lax.broadcasted_iota(jnp.int32, (LANES, DN_WIDTH), 1) // DN_HEAD_DIM
    expand = (er == ec).astype(BF16)
    x_beta = _dot_exact_rhs(beta_c, expand)
    x_gc = _dot_exact_rhs(gc, expand)
    x_gtot = _dot_exact_rhs(gtot, expand)
    x_eg = jnp.exp(x_gc)
    xbeta_ref[...] = x_beta
    xgc_ref[...] = x_gc
    xgl_ref[...] = jnp.exp(x_gtot)
    qn_ref[...] = qn.astype(BF16)
    kn_ref[...] = kn.astype(BF16)
    kbe_ref[...] = (kn * (x_beta * x_eg)).astype(BF16)
    vb_ref[...] = (v * x_beta).astype(BF16)
    qg_ref[...] = (qn * x_eg).astype(BF16)
    kd_ref[...] = (kn * jnp.exp(x_gtot - x_gc)).astype(BF16)

    ci = lax.broadcasted_iota(jnp.int32, (CHUNK, CHUNK), 0)
    cj = lax.broadcasted_iota(jnp.int32, (CHUNK, CHUNK), 1)
    eye = (ci == cj).astype(F32)

    heads = range(DN_HEADS)
    lanes = [pl.ds(h * DN_HEAD_DIM, DN_HEAD_DIM) for h in heads]
    stack = lambda top, bot: jnp.concatenate([top, bot], axis=0)

    chunk_rows = lambda c: pl.ds(pl.multiple_of(c * CHUNK, CHUNK), CHUNK)

    def within_chunk_body(g, carry_unused):
        chains = [(g * INTRA_GROUP + cc, h) for cc in range(INTRA_GROUP) for h in heads]
        ids = range(len(chains))
        rows = [chunk_rows(c) for c, _ in chains]
        ln = [lanes[h] for _, h in chains]
        kb = [kn_ref[rows[i], ln[i]] for i in ids]
        kq = [_dot_nt(stack(kb[i], qn_ref[rows[i], ln[i]]), kb[i]) for i in ids]
        decay = [jnp.exp(jnp.where(ci >= cj, xgc_ref[rows[i], ln[i]] - gr_ref[chains[i][0], h:h + 1, :], -jnp.inf))
                 for i, (_, h) in enumerate(chains)]
        a = [jnp.where(ci > cj, xbeta_ref[rows[i], ln[i]] * kq[i][0:CHUNK] * decay[i], 0.0) for i in ids]
        slot = [c * DN_HEADS + h for c, h in chains]
        for i in ids:
            qkd_ref[slot[i]] = (kq[i][CHUNK:2 * CHUNK] * decay[i]).astype(BF16)
            wq_ref[slot[i], CHUNK:2 * CHUNK, :] = qg_ref[rows[i], ln[i]]
            kdt_ref[slot[i]] = kd_ref[rows[i], ln[i]].T.astype(BF16)
            gl_ref[slot[i]] = xgl_ref[rows[i], ln[i]]
        t_inv = [eye - a[i] for i in ids]
        pw = [a[i].astype(BF16) for i in ids]
        pw = [_dot(pw[i], pw[i]).astype(BF16) for i in ids]
        for _ in range(4):
            r = [_dot(stack(t_inv[i].astype(BF16), pw[i]), pw[i]) for i in ids]
            t_inv = [t_inv[i] + r[i][0:CHUNK] for i in ids]
            pw = [r[i][CHUNK:2 * CHUNK].astype(BF16) for i in ids]
        t_inv = [t_inv[i] + _dot(t_inv[i].astype(BF16), pw[i]) for i in ids]
        tb = [t_inv[i].astype(BF16) for i in ids]
        for i in ids:
            wq_ref[slot[i], 0:CHUNK, :] = _dot(tb[i], kbe_ref[rows[i], ln[i]]).astype(BF16)
        for i in ids:
            u_ref[slot[i]] = _dot(tb[i], vb_ref[rows[i], ln[i]])
        return carry_unused

    def state_body(c, carry_unused):
        rows = chunk_rows(c)
        slot = [c * DN_HEADS + h for h in heads]
        s = [s_ref[h] for h in heads]
        ws = [_dot(wq_ref[slot[h]], s[h].astype(BF16)) for h in heads]
        vnb = [(u_ref[slot[h]] - ws[h][0:CHUNK]).astype(BF16) for h in heads]
        for h in heads:
            oacc_ref[h, rows, :] = ws[h][CHUNK:2 * CHUNK] + _dot(qkd_ref[slot[h]], vnb[h])
        for h in heads:
            s_ref[h] = s[h] * gl_ref[slot[h]] + _dot(kdt_ref[slot[h]], vnb[h])
        return carry_unused

    lax.fori_loop(0, n_chunks // INTRA_GROUP, within_chunk_body, 0)
    lax.fori_loop(0, n_chunks, state_body, 0)

    o = jnp.concatenate([oacc_ref[h] for h in heads], axis=1)
    ms = _dot_exact_rhs(o * o, head_ones) * (1.0 / DN_HEAD_DIM)
    z = z_ref[...]
    o_ref[...] = (o * lax.rsqrt(ms + NORM_EPS) * dnw_ref[...] * (z * _sigmoid(z))).astype(o_ref.dtype)


def _deltanet(qkv, z, ba, bat, conv_w, alog_r, dtb_r, alog_c, dtb_c, dnw, batch, seq):
    lb = DN_LB
    nb = seq // lb
    slots = lb // CHUNK * DN_HEADS
    full = lambda a: pl.BlockSpec(a.shape, lambda b, j: (0,) * a.ndim)
    row = lambda n: pl.BlockSpec((lb, n), lambda b, j: (b * nb + j, 0))
    return pl.pallas_call(
        _deltanet_kernel,
        grid=(batch, nb),
        in_specs=[row(3 * DN_WIDTH), row(DN_WIDTH), row(2 * LANES),
                  pl.BlockSpec((2 * SUBLANES, lb), lambda b, j: (0, b * nb + j)),
                  full(conv_w), full(alog_r), full(dtb_r), full(alog_c), full(dtb_c), full(dnw)],
        out_specs=row(DN_WIDTH),
        out_shape=jax.ShapeDtypeStruct((batch * seq, DN_WIDTH), BF16),
        scratch_shapes=[
            pltpu.VMEM((DN_HEADS, DN_HEAD_DIM, DN_HEAD_DIM), F32),
            pltpu.VMEM((SUBLANES + lb, 3 * DN_WIDTH), F32),
            pltpu.VMEM((lb, DN_WIDTH), BF16),
            pltpu.VMEM((lb, DN_WIDTH), BF16),
            pltpu.VMEM((lb, DN_WIDTH), BF16),
            pltpu.VMEM((lb, DN_WIDTH), BF16),
            pltpu.VMEM((lb, DN_WIDTH), BF16),
            pltpu.VMEM((lb, DN_WIDTH), BF16),
            pltpu.VMEM((lb, DN_WIDTH), F32),
            pltpu.VMEM((lb, DN_WIDTH), F32),
            pltpu.VMEM((lb, DN_WIDTH), F32),
            pltpu.VMEM((lb // CHUNK, DN_HEADS, CHUNK), F32),
            pltpu.VMEM((DN_HEADS, lb, DN_HEAD_DIM), F32),
            pltpu.VMEM((slots, CHUNK, CHUNK), BF16),
            pltpu.VMEM((slots, 2 * CHUNK, DN_HEAD_DIM), BF16),
            pltpu.VMEM((slots, CHUNK, DN_HEAD_DIM), F32),
            pltpu.VMEM((slots, DN_HEAD_DIM, CHUNK), BF16),
            pltpu.VMEM((slots, DN_HEAD_DIM, DN_HEAD_DIM), F32),
        ],
        compiler_params=pltpu.CompilerParams(dimension_semantics=("arbitrary", "arbitrary"),
                                             vmem_limit_bytes=VMEM_LIMIT),
        name="deltanet",
    )(qkv, z, ba, bat, conv_w, alog_r, dtb_r, alog_c, dtb_c, dnw)


def _mix_kernel(x_ref, on_ref, u_ref, gates_ref, wupa_ref, poolw_ref, pscale_ref, wupb_ref, wout_ref,
                gffn_ref, wr_hi_ref, wr_lo_ref, br_ref,
                x1_ref, h2_ref, route_ref, counts_ref, ucarry_ref, ecount_ref):
    tm = x_ref.shape[0]
    j = pl.program_id(1)

    @pl.when(j == 0)
    def _():
        ucarry_ref[...] = jnp.zeros_like(ucarry_ref)

    @pl.when((j == 0) & (pl.program_id(0) == 0))
    def _():
        ecount_ref[...] = jnp.zeros_like(ecount_ref)

    u = u_ref[...]
    ext = jnp.concatenate([ucarry_ref[...], u], axis=0)
    ucarry_ref[...] = u[tm - POOL_HALO:tm]
    t_pos = j * tm + lax.broadcasted_iota(jnp.int32, (tm, POOL_GROUP_DIM), 0)
    ys = []
    for g in range(POOL_GROUPS):
        s = ext[:, g * POOL_GROUP_DIM:(g + 1) * POOL_GROUP_DIM]
        shift = 1
        while shift < POOL_WINDOWS[g]:
            s = s + pltpu.roll(s, shift, 0)
            shift *= 2
        count = jnp.minimum(t_pos + 1, POOL_WINDOWS[g]).astype(F32)
        pooled = s[POOL_HALO:] / count - u[:, g * POOL_GROUP_DIM:(g + 1) * POOL_GROUP_DIM]
        ys.append(_dot(pooled.astype(BF16), poolw_ref[g]))
    yb = jnp.concatenate(ys, axis=-1) * pscale_ref[...]
    y_b = _dot(yb.astype(BF16), wupb_ref[...])
    y_a = _dot(on_ref[...], wupa_ref[...])
    gates = gates_ref[...]
    merged = _sigmoid(gates[:, 0:D_MODEL]) * y_a + _sigmoid(gates[:, D_MODEL:2 * D_MODEL]) * y_b
    x1 = x_ref[...] + _dot(merged.astype(BF16), wout_ref[...])
    x1_ref[...] = x1

    h2 = _rms(x1, gffn_ref[...])
    _store_row_tiles(h2_ref, h2)

    hi, lo = _split2(h2)
    logits = _dot(hi, wr_hi_ref[...]) + _dot(lo, wr_hi_ref[...]) + _dot(hi, wr_lo_ref[...]) + br_ref[...]
    lane = lax.broadcasted_iota(jnp.int32, (tm, LANES), 1)
    lg = jnp.where(lane < N_EXPERTS, logits, -jnp.inf)
    vals, idxs, sels = [], [], []
    for _ in range(TOP_K):
        m = jnp.max(lg, axis=-1, keepdims=True)
        idx = jnp.min(jnp.where(lg == m, lane, LANES), axis=-1, keepdims=True)
        sel = lane == idx
        vals.append(m)
        idxs.append(idx)
        sels.append(sel)
        lg = jnp.where(sel, -jnp.inf, lg)
    es = [jnp.exp(vk - vals[0]) for vk in vals]
    denom = es[0] + es[1] + es[2] + es[3]

    chosen = jnp.zeros((tm, LANES), F32)
    for sel in sels:
        chosen = chosen + jnp.where(sel, 1.0, 0.0)
    ri = lax.broadcasted_iota(jnp.int32, (tm, tm), 0)
    rj = lax.broadcasted_iota(jnp.int32, (tm, tm), 1)
    before = (rj < ri).astype(BF16)
    pos = ecount_ref[...] + _dot(before, chosen.astype(BF16))
    ecount_ref[...] = ecount_ref[...] + jnp.sum(chosen, axis=0, keepdims=True)
    counts_ref[...] = ecount_ref[...]

    route = jnp.zeros((tm, LANES), F32)
    for kk in range(TOP_K):
        rank = jnp.sum(jnp.where(sels[kk], pos, 0.0), axis=-1, keepdims=True)
        route = jnp.where(lane == kk, rank, route)
        route = jnp.where(lane == TOP_K + kk, idxs[kk].astype(F32), route)
        route = jnp.where(lane == 2 * TOP_K + kk, es[kk] / denom, route)
    route_ref[...] = route


def _mix(x2, on, u, gates, wupa, poolw, pscale, wupb, wout, gffn, wr_hi, wr_lo, br, batch, seq):
    tm = MIX_TM
    nb = seq // tm
    t = batch * seq
    full = lambda a: pl.BlockSpec(a.shape, lambda b, j: (0,) * a.ndim)
    row = lambda n: pl.BlockSpec((tm, n), lambda b, j: (b * nb + j, 0))
    return pl.pallas_call(
        _mix_kernel,
        grid=(batch, nb),
        in_specs=[row(D_MODEL), row(DN_WIDTH), row(POOL_WIDTH), row(2 * D_MODEL),
                  full(wupa), full(poolw), full(pscale), full(wupb), full(wout), full(gffn),
                  full(wr_hi), full(wr_lo), full(br)],
        out_specs=[row(D_MODEL), pl.BlockSpec((tm * ROW_SUB, LANES), lambda b, j: (b * nb + j, 0)), row(LANES),
                   pl.BlockSpec((1, LANES), lambda b, j: (0, 0))],
        out_shape=[jax.ShapeDtypeStruct((t, D_MODEL), F32),
                   jax.ShapeDtypeStruct((t * ROW_SUB, LANES), F32),
                   jax.ShapeDtypeStruct((t, LANES), F32),
                   jax.ShapeDtypeStruct((1, LANES), F32)],
        scratch_shapes=[pltpu.VMEM((POOL_HALO, POOL_WIDTH), F32), pltpu.VMEM((1, LANES), F32)],
        compiler_params=pltpu.CompilerParams(dimension_semantics=("arbitrary", "arbitrary"),
                                             vmem_limit_bytes=VMEM_LIMIT),
        name="mix",
    )(x2, on, u, gates, wupa, poolw, pscale, wupb, wout, gffn, wr_hi, wr_lo, br)


def _routing_tables(route, counts, tg, n_tiles):
    cnt = counts[0, 0:N_EXPERTS].astype(jnp.int32)
    padded = (cnt + (tg - 1)) // tg * tg
    ends = jnp.cumsum(padded)
    offs = ends - padded
    rank = route[:, 0:TOP_K].astype(jnp.int32)
    eidx = route[:, TOP_K:2 * TOP_K].astype(jnp.int32)
    dest = offs[eidx] + rank
    n_used = ends[N_EXPERTS - 1] // tg
    tile_start = jnp.arange(n_tiles, dtype=jnp.int32) * tg
    tile_expert = jnp.sum((tile_start[:, None] >= ends[None, :]).astype(jnp.int32), axis=1)
    tile_expert = jnp.minimum(tile_expert, N_EXPERTS - 1)
    last = tile_expert[jnp.maximum(n_used - 1, 0)]
    tile_expert = jnp.where(jnp.arange(n_tiles) < n_used, tile_expert, last)
    pad_start = offs + cnt
    pad_len = padded - cnt
    return (dest, tile_expert.astype(jnp.int32), n_used.reshape(1).astype(jnp.int32),
            pad_start.astype(jnp.int32), pad_len.astype(jnp.int32))


def _dispatch_kernel(pad_start_ref, pad_len_ref, nu_ref, dest_ref, h2_ref, xs_ref, zero_ref, sem, zsem):
    tm = h2_ref.shape[0] // ROW_SUB
    tg = zero_ref.shape[0]
    n_tiles = xs_ref.shape[0] // tg

    @pl.when(pl.program_id(0) == 0)
    def _():
        zero_ref[...] = jnp.zeros_like(zero_ref)

        def pad_copy(d):
            return pltpu.make_async_copy(_row_tile(zero_ref, 0), _row_tile(xs_ref, d), zsem)

        def tile_copy(i):
            return pltpu.make_async_copy(zero_ref, xs_ref.at[pl.ds(pl.multiple_of(i * tg, tg), tg)], zsem)

        def per_expert(e, c):
            lax.fori_loop(0, pad_len_ref[e], lambda r, cc: (pad_copy(pad_start_ref[e] + r).start(), cc)[1], 0)
            lax.fori_loop(0, pad_len_ref[e], lambda r, cc: (pad_copy(0).wait(), cc)[1], 0)
            return c

        lax.fori_loop(0, N_EXPERTS, per_expert, 0)
        lax.fori_loop(nu_ref[0], n_tiles, lambda i, cc: (tile_copy(i).start(), cc)[1], 0)
        lax.fori_loop(nu_ref[0], n_tiles, lambda i, cc: (tile_copy(0).wait(), cc)[1], 0)

    def row_copy(r, d):
        return pltpu.make_async_copy(_row_tile(h2_ref, r), _row_tile(xs_ref, d), sem)

    def start(r, c):
        for kk in range(TOP_K):
            row_copy(r, dest_ref[0, 0, r * TOP_K + kk]).start(priority=kk % 2)
        return c

    lax.fori_loop(0, tm, start, 0)
    for kk in range(TOP_K):
        pltpu.make_async_copy(h2_ref, xs_ref.at[pl.ds(0, tm * ROW_SUB)], sem).wait()


def _dispatch(pad_start, pad_len, n_used, dest3, h2, n_rows):
    t = h2.shape[0] // ROW_SUB
    tm = DSP_TM
    return pl.pallas_call(
        _dispatch_kernel,
        grid_spec=pltpu.PrefetchScalarGridSpec(
            num_scalar_prefetch=3,
            grid=(t // tm,),
            in_specs=[pl.BlockSpec((1, 1, tm * TOP_K), lambda i, ps, pn, nu: (i, 0, 0), memory_space=pltpu.SMEM),
                      pl.BlockSpec((tm * ROW_SUB, LANES), lambda i, ps, pn, nu: (i, 0))],
            out_specs=pl.BlockSpec(memory_space=pl.ANY),
            scratch_shapes=[pltpu.VMEM((MOE_TG * ROW_SUB, LANES), F32), pltpu.SemaphoreType.DMA,
                            pltpu.SemaphoreType.DMA]),
        out_shape=jax.ShapeDtypeStruct((n_rows * ROW_SUB, LANES), F32),
        compiler_params=pltpu.CompilerParams(dimension_semantics=("arbitrary",), vmem_limit_bytes=VMEM_LIMIT),
        name="dispatch",
    )(pad_start, pad_len, n_used, dest3, h2)


def _experts_kernel(te_ref, nu_ref, xs_ref, wgu_ref, bgu_ref, wd_ref, bd_ref, ys_ref, wgu_bf_ref, wd_bf_ref):
    i = pl.program_id(0)

    @pl.when((i == 0) | (te_ref[i] != te_ref[jnp.maximum(i - 1, 0)]))
    def _():
        wgu_bf_ref[...] = wgu_ref[...].astype(BF16)
        wd_bf_ref[...] = wd_ref[...].astype(BF16)

    @pl.when(i < nu_ref[0])
    def _():
        gu = _dot(_load_row_tiles(xs_ref).astype(BF16), wgu_bf_ref[...]) + bgu_ref[...]
        gate = jnp.minimum(gu[:, 0:D_FF], SWIGLU_LIMIT)
        up = jnp.clip(gu[:, D_FF:2 * D_FF], -SWIGLU_LIMIT, SWIGLU_LIMIT)
        act = gate * _sigmoid(SWIGLU_ALPHA * gate) * (up + 1.0)
        _store_row_tiles(ys_ref, _dot(act.astype(BF16), wd_bf_ref[...]) + bd_ref[...])

    @pl.when(i >= nu_ref[0])
    def _():
        ys_ref[...] = jnp.zeros_like(ys_ref)


def _experts(tile_expert, n_used, xs, wgu, bgu, wd, bd):
    tg = MOE_TG * ROW_SUB
    n_tiles = xs.shape[0] // tg
    tile = lambda i, te, nu: (jnp.minimum(i, nu[0] - 1), 0)
    expert = lambda i, te, nu: (te[i], 0, 0)
    return pl.pallas_call(
        _experts_kernel,
        grid_spec=pltpu.PrefetchScalarGridSpec(
            num_scalar_prefetch=2,
            grid=(n_tiles,),
            in_specs=[pl.BlockSpec((tg, LANES), tile),
                      pl.BlockSpec((None, D_MODEL, 2 * D_FF), expert),
                      pl.BlockSpec((None, 1, 2 * D_FF), expert),
                      pl.BlockSpec((None, D_FF, D_MODEL), expert),
                      pl.BlockSpec((None, 1, D_MODEL), expert)],
            out_specs=pl.BlockSpec((tg, LANES), lambda i, te, nu: (i, 0)),
            scratch_shapes=[pltpu.VMEM((D_MODEL, 2 * D_FF), BF16), pltpu.VMEM((D_FF, D_MODEL), BF16)]),
        out_shape=jax.ShapeDtypeStruct(xs.shape, F32),
        compiler_params=pltpu.CompilerParams(dimension_semantics=("arbitrary",), vmem_limit_bytes=VMEM_LIMIT),
        name="experts",
    )(tile_expert, n_used, xs, wgu, bgu, wd, bd)


def _combine_kernel(dest_ref, route_ref, x1_ref, gfin_ref, ys_ref, out_ref, ybuf_ref, sem):
    tm = x1_ref.shape[0]

    def row_copy(r, kk, d):
        return pltpu.make_async_copy(_row_tile(ys_ref, d), _row_tile(ybuf_ref.at[kk], r), sem)

    def start(r, c):
        for kk in range(TOP_K):
            row_copy(r, kk, dest_ref[0, 0, r * TOP_K + kk]).start(priority=kk % 2)
        return c

    lax.fori_loop(0, tm, start, 0)
    for kk in range(TOP_K):
        pltpu.make_async_copy(ys_ref.at[pl.ds(0, tm * ROW_SUB)], ybuf_ref.at[kk], sem).wait()
    route = route_ref[...]
    acc = x1_ref[...]
    for kk in range(TOP_K):
        acc = acc + route[:, 2 * TOP_K + kk:2 * TOP_K + kk + 1] * _load_row_tiles(ybuf_ref.at[kk])
    out_ref[...] = _rms(acc, gfin_ref[...])


def _combine(dest3, route, x1, gfin, ys):
    t = x1.shape[0]
    tm = CMB_TM
    return pl.pallas_call(
        _combine_kernel,
        grid=(t // tm,),
        in_specs=[pl.BlockSpec((1, 1, tm * TOP_K), lambda i: (i, 0, 0), memory_space=pltpu.SMEM),
                  pl.BlockSpec((tm, LANES), lambda i: (i, 0)),
                  pl.BlockSpec((tm, D_MODEL), lambda i: (i, 0)),
                  pl.BlockSpec((1, D_MODEL), lambda i: (0, 0)),
                  pl.BlockSpec(memory_space=pl.ANY)],
        out_specs=pl.BlockSpec((tm, D_MODEL), lambda i: (i, 0)),
        out_shape=jax.ShapeDtypeStruct((t, D_MODEL), F32),
        scratch_shapes=[pltpu.VMEM((TOP_K, tm * ROW_SUB, LANES), F32), pltpu.SemaphoreType.DMA],
        compiler_params=pltpu.CompilerParams(dimension_semantics=("arbitrary",), vmem_limit_bytes=VMEM_LIMIT),
        name="combine",
    )(dest3, route, x1, gfin, ys)


def kernel(x, g_mix, w_in, conv_w, a_log, dt_bias, dn_norm, w_up_a, pool_w, pool_scale, w_up_b, w_out, g_ffn,
           w_router, b_router, w_gate_up, b_gate_up, w_down, b_down, g_final):
    batch, seq, d = x.shape
    assert d == D_MODEL and seq % DN_LB == 0 and seq % MIX_TM == 0
    assert (batch * seq) % DSP_TM == 0 and (batch * seq * TOP_K) % MOE_TG == 0
    assert g_mix.shape[0] == 1, "one layer"
    t = batch * seq
    x2 = x.reshape(t, d)

    w = w_in[0]
    o_z = 3 * DN_WIDTH
    o_b = o_z + DN_WIDTH
    o_a = o_b + DN_HEADS
    o_u = o_a + DN_HEADS
    o_g = o_u + POOL_WIDTH
    wqkv = w[:, 0:o_z].astype(BF16)
    wz = w[:, o_z:o_b].astype(BF16)
    wu = w[:, o_u:o_g].astype(BF16)
    wg = w[:, o_g:].astype(BF16)
    w_b = w[:, o_b:o_a]
    w_a = w[:, o_a:o_u]
    pad = jnp.zeros((d, LANES - DN_HEADS), F32)
    wba = jnp.concatenate([w_b, pad, w_a, pad], axis=1).astype(BF16)
    wbat = jnp.concatenate([w_b, w_a], axis=1).T.astype(BF16)

    qkv, z, u, gates, ba, bat = _in_proj(x2, g_mix, wqkv, wz, wu, wg, wba, wbat)

    lane_pad = lambda p: jnp.pad(p.reshape(1, DN_HEADS), ((0, 0), (0, LANES - DN_HEADS)))
    on = _deltanet(qkv, z, ba, bat, conv_w[0], lane_pad(a_log[0]), lane_pad(dt_bias[0]),
                   a_log[0].reshape(DN_HEADS, 1), dt_bias[0].reshape(DN_HEADS, 1),
                   jnp.tile(dn_norm[0], DN_HEADS).reshape(1, DN_WIDTH), batch, seq)

    wr = jnp.pad(w_router[0], ((0, 0), (0, LANES - N_EXPERTS)))
    wr_hi = wr.astype(BF16)
    wr_lo = (wr - wr_hi.astype(F32)).astype(BF16)
    br = jnp.pad(b_router[0].reshape(1, N_EXPERTS), ((0, 0), (0, LANES - N_EXPERTS)))
    x1, h2, route, counts = _mix(x2, on, u, gates, w_up_a[0].astype(BF16), pool_w[0].astype(BF16),
                                  pool_scale[0].reshape(1, POOL_WIDTH), w_up_b[0].astype(BF16),
                                  w_out[0].astype(BF16), g_ffn, wr_hi, wr_lo, br, batch, seq)

    n_tiles = t * TOP_K // MOE_TG + N_EXPERTS
    dest, tile_expert, n_used, pad_start, pad_len = _routing_tables(route, counts, MOE_TG, n_tiles)
    xs = _dispatch(pad_start, pad_len, n_used, dest.reshape(t // DSP_TM, 1, DSP_TM * TOP_K), h2, n_tiles * MOE_TG)
    ys = _experts(tile_expert, n_used, xs, w_gate_up[0], b_gate_up[0].reshape(N_EXPERTS, 1, 2 * D_FF),
                  w_down[0], b_down[0].reshape(N_EXPERTS, 1, D_MODEL))
    out = _combine(dest.reshape(t // CMB_TM, 1, CMB_TM * TOP_K), route, x1, g_final.reshape(1, D_MODEL), ys)
    return out.reshape(batch, seq, d)
```

```python
import functools

import jax
import jax.numpy as jnp
from jax import lax
from jax.experimental import pallas as pl
from jax.experimental.pallas import tpu as pltpu

F32 = jnp.float32
BF16 = jnp.bfloat16

D_MODEL = 1024
CHUNK = 64
DN_HEADS = 8
DN_HEAD_DIM = 64
DN_WIDTH = DN_HEADS * DN_HEAD_DIM
CONV_WIDTH = 4
POOL_GROUPS = 4
POOL_WINDOWS = (2, 4, 8, 16)
POOL_WIDTH = 512
POOL_GROUP_DIM = 128
POOL_HALO = 16
N_EXPERTS = 32
TOP_K = 4
D_FF = D_MODEL
SWIGLU_LIMIT = 7.0
SWIGLU_ALPHA = 1.702
NORM_EPS = 1e-6
LANES = 128
SUBLANES = 8
VMEM_LIMIT = 56 * 1024 * 1024

IN_TM = 256
DN_LB = 256
INTRA_GROUP = 2
MIX_TM = 256
MOE_TG = 512
DSP_TM = 512
CMB_TM = 256
DMA_UNROLL = 8


def _dot(a, b):
    return jnp.dot(a, b, preferred_element_type=F32)


def _dot_nt(a, b):
    return lax.dot_general(a, b, (((1,), (1,)), ((), ())), preferred_element_type=F32)


def _dot_tn(a, b):
    return lax.dot_general(a, b, (((0,), (0,)), ((), ())), preferred_element_type=F32)


def _split2(x):
    hi = x.astype(BF16)
    lo = (x - hi.astype(F32)).astype(BF16)
    return hi, lo


def _dot_exact_rhs(x, m):
    hi, lo = _split2(x)
    return _dot(hi, m) + _dot(lo, m)


def _dot_exact_lhs(m, x):
    hi, lo = _split2(x)
    return _dot(m, hi) + _dot(m, lo)


def _softplus(x):
    return jnp.maximum(x, 0.0) + jnp.log1p(jnp.exp(-jnp.abs(x)))


def _sigmoid(x):
    return 1.0 / (1.0 + jnp.exp(-x))


def _rms(x, g):
    return x * lax.rsqrt(jnp.mean(x * x, axis=-1, keepdims=True) + NORM_EPS) * g


ROW_SUB = D_MODEL // LANES


def _row_tile(ref, r):
    return ref.at[pl.ds(pl.multiple_of(r * ROW_SUB, ROW_SUB), ROW_SUB)]


def _store_row_tiles(ref, x):
    for j in range(ROW_SUB):
        ref[pl.ds(j, x.shape[0], stride=ROW_SUB), :] = x[:, j * LANES:(j + 1) * LANES]


def _load_row_tiles(ref):
    n = ref.shape[0] // ROW_SUB
    return jnp.concatenate([ref[pl.ds(j, n, stride=ROW_SUB), :] for j in range(ROW_SUB)], axis=1)


def _in_proj_kernel(x_ref, g_ref, wqkv_ref, wz_ref, wu_ref, wg_ref, wba_ref, wbat_ref, convw_ref,
                    qkv_ref, z_ref, u_ref, gates_ref, ba_ref, bat_ref, hist_ref, *, steps_per_seq):
    tm = x_ref.shape[0]
    hb = _rms(x_ref[...], g_ref[...]).astype(BF16)

    @pl.when(pl.program_id(0) % steps_per_seq == 0)
    def _():
        hist_ref[0:SUBLANES, :] = jnp.zeros((SUBLANES, hist_ref.shape[1]), F32)

    blk = _dot(hb, wqkv_ref[...])
    cw = convw_ref[...]
    hist_ref[SUBLANES:SUBLANES + tm, :] = blk
    act = blk * cw[CONV_WIDTH - 1:CONV_WIDTH]
    for s in range(1, CONV_WIDTH):
        act = act + hist_ref[pl.ds(SUBLANES - s, tm), :] * cw[CONV_WIDTH - 1 - s:CONV_WIDTH - s]
    hist_ref[0:SUBLANES, :] = blk[tm - SUBLANES:tm]
    qkv_ref[...] = act * _sigmoid(act)

    z_ref[...] = _dot(hb, wz_ref[...])
    u_ref[...] = _dot(hb, wu_ref[...])
    gates_ref[...] = _dot(hb, wg_ref[...])
    ba_ref[...] = _dot(hb, wba_ref[...])
    bat_ref[...] = _dot_nt(wbat_ref[...], hb)


def _in_proj(x2, g_mix, wqkv, wz, wu, wg, wba, wbat, conv_w, seq):
    t = x2.shape[0]
    tm = IN_TM
    full = lambda a: pl.BlockSpec(a.shape, lambda i: (0, 0))
    row = lambda n: pl.BlockSpec((tm, n), lambda i: (i, 0))
    return pl.pallas_call(
        functools.partial(_in_proj_kernel, steps_per_seq=seq // tm),
        grid=(t // tm,),
        in_specs=[row(D_MODEL), full(g_mix), full(wqkv), full(wz), full(wu), full(wg), full(wba), full(wbat),
                  full(conv_w)],
        out_specs=[row(3 * DN_WIDTH), row(DN_WIDTH), row(POOL_WIDTH), row(2 * D_MODEL), row(2 * LANES),
                   pl.BlockSpec((2 * SUBLANES, tm), lambda i: (0, i))],
        out_shape=[jax.ShapeDtypeStruct((t, 3 * DN_WIDTH), F32),
                   jax.ShapeDtypeStruct((t, DN_WIDTH), F32),
                   jax.ShapeDtypeStruct((t, POOL_WIDTH), F32),
                   jax.ShapeDtypeStruct((t, 2 * D_MODEL), F32),
                   jax.ShapeDtypeStruct((t, 2 * LANES), F32),
                   jax.ShapeDtypeStruct((2 * SUBLANES, t), F32)],
        scratch_shapes=[pltpu.VMEM((SUBLANES + tm, 3 * DN_WIDTH), F32)],
        compiler_params=pltpu.CompilerParams(dimension_semantics=("arbitrary",), vmem_limit_bytes=VMEM_LIMIT),
        name="in_proj",
    )(x2, g_mix, wqkv, wz, wu, wg, wba, wbat, conv_w)


def _deltanet_kernel(qkv_ref, z_ref, ba_ref, bat_ref, alog_r_ref, dtb_r_ref, alog_c_ref, dtb_c_ref, dnw_ref,
                     head_ones_ref, tril_ref, triu_ref, chunk_ones_ref, expand_ref, o_ref,
                     s_ref, qn_ref, kn_ref, kbe_ref, vb_ref, qg_ref, kd_ref,
                     xbeta_ref, xgc_ref, xgl_ref, gr_ref, oacc_ref, qkd_ref, wq_ref, u_ref, kdt_ref, gl_ref):
    lb = qkv_ref.shape[0]
    n_chunks = lb // CHUNK

    @pl.when(pl.program_id(1) == 0)
    def _():
        s_ref[...] = jnp.zeros_like(s_ref)

    q = qkv_ref[:, 0:DN_WIDTH]
    k = qkv_ref[:, DN_WIDTH:2 * DN_WIDTH]
    v = qkv_ref[:, 2 * DN_WIDTH:3 * DN_WIDTH]

    head_ones = head_ones_ref[...]
    qn = q * lax.rsqrt(_dot_exact_rhs(q * q, head_ones) + NORM_EPS) * (DN_HEAD_DIM ** -0.5)
    kn = k * lax.rsqrt(_dot_exact_rhs(k * k, head_ones) + NORM_EPS)

    ba = ba_ref[...]
    beta_c = _sigmoid(ba[:, 0:LANES])
    g_c = -jnp.exp(alog_r_ref[...]) * _softplus(ba[:, LANES:2 * LANES] + dtb_r_ref[...])
    lane = lax.broadcasted_iota(jnp.int32, (lb, LANES), 1)
    g_c = jnp.where(lane < DN_HEADS, g_c, 0.0)
    bat = bat_ref[...]
    g_r = -jnp.exp(alog_c_ref[...]) * _softplus(bat[SUBLANES:2 * SUBLANES] + dtb_c_ref[...])

    gc = _dot_exact_lhs(tril_ref[...], g_c)
    gtot = _dot_exact_lhs(chunk_ones_ref[...], g_c)
    gr = _dot_exact_rhs(g_r, triu_ref[...])
    for c in range(n_chunks):
        gr_ref[c] = gr[:, c * CHUNK:(c + 1) * CHUNK]

    expand = expand_ref[...]
    x_beta = _dot_exact_rhs(beta_c, expand)
    x_gc = _dot_exact_rhs(gc, expand)
    x_gtot = _dot_exact_rhs(gtot, expand)
    x_eg = jnp.exp(x_gc)
    xbeta_ref[...] = x_beta
    xgc_ref[...] = x_gc
    xgl_ref[...] = jnp.exp(x_gtot)
    qn_ref[...] = qn.astype(BF16)
    kn_ref[...] = kn.astype(BF16)
    kbe_ref[...] = (kn * (x_beta * x_eg)).astype(BF16)
    vb_ref[...] = (v * x_beta).astype(BF16)
    qg_ref[...] = (qn * x_eg).astype(BF16)
    kd_ref[...] = (kn * jnp.exp(x_gtot - x_gc)).astype(BF16)

    ci = lax.broadcasted_iota(jnp.int32, (CHUNK, CHUNK), 0)
    cj = lax.broadcasted_iota(jnp.int32, (CHUNK, CHUNK), 1)
    eye = (ci == cj).astype(F32)

    heads = range(DN_HEADS)
    lanes = [pl.ds(h * DN_HEAD_DIM, DN_HEAD_DIM) for h in heads]
    stack = lambda top, bot: jnp.concatenate([top, bot], axis=0)

    chunk_rows = lambda c: pl.ds(pl.multiple_of(c * CHUNK, CHUNK), CHUNK)

    def within_chunk_body(g, carry_unused):
        chains = [(g * INTRA_GROUP + cc, h) for cc in range(INTRA_GROUP) for h in heads]
        ids = range(len(chains))
        rows = [chunk_rows(c) for c, _ in chains]
        ln = [lanes[h] for _, h in chains]
        kb = [kn_ref[rows[i], ln[i]] for i in ids]
        kq = [_dot_nt(stack(kb[i], qn_ref[rows[i], ln[i]]), kb[i]) for i in ids]
        decay = [jnp.exp(jnp.where(ci >= cj, xgc_ref[rows[i], ln[i]] - gr_ref[chains[i][0], h:h + 1, :], -jnp.inf))
                 for i, (_, h) in enumerate(chains)]
        a = [jnp.where(ci > cj, xbeta_ref[rows[i], ln[i]] * kq[i][0:CHUNK] * decay[i], 0.0) for i in ids]
        slot = [c * DN_HEADS + h for c, h in chains]
        for i in ids:
            qkd_ref[slot[i]] = (kq[i][CHUNK:2 * CHUNK] * decay[i]).astype(BF16)
            wq_ref[slot[i], CHUNK:2 * CHUNK, :] = qg_ref[rows[i], ln[i]]
            kdt_ref[slot[i]] = kd_ref[rows[i], ln[i]].T.astype(BF16)
            gl_ref[slot[i]] = xgl_ref[rows[i], ln[i]]
        t_inv = [eye - a[i] for i in ids]
        pw = [a[i].astype(BF16) for i in ids]
        pw = [_dot(pw[i], pw[i]).astype(BF16) for i in ids]
        for _ in range(4):
            r = [_dot(stack(t_inv[i].astype(BF16), pw[i]), pw[i]) for i in ids]
            t_inv = [t_inv[i] + r[i][0:CHUNK] for i in ids]
            pw = [r[i][CHUNK:2 * CHUNK].astype(BF16) for i in ids]
        t_inv = [t_inv[i] + _dot(t_inv[i].astype(BF16), pw[i]) for i in ids]
        tb = [t_inv[i].astype(BF16) for i in ids]
        for i in ids:
            wq_ref[slot[i], 0:CHUNK, :] = _dot(tb[i], kbe_ref[rows[i], ln[i]]).astype(BF16)
        for i in ids:
            u_ref[slot[i]] = _dot(tb[i], vb_ref[rows[i], ln[i]])
        return carry_unused

    def state_body(c, carry_unused):
        rows = chunk_rows(c)
        slot = [c * DN_HEADS + h for h in heads]
        s = [s_ref[h] for h in heads]
        ws = [_dot(wq_ref[slot[h]], s[h].astype(BF16)) for h in heads]
        vnb = [(u_ref[slot[h]] - ws[h][0:CHUNK]).astype(BF16) for h in heads]
        for h in heads:
            oacc_ref[h, rows, :] = ws[h][CHUNK:2 * CHUNK] + _dot(qkd_ref[slot[h]], vnb[h])
        for h in heads:
            s_ref[h] = s[h] * gl_ref[slot[h]] + _dot(kdt_ref[slot[h]], vnb[h])
        return carry_unused

    lax.fori_loop(0, n_chunks // INTRA_GROUP, within_chunk_body, 0)
    lax.fori_loop(0, n_chunks, state_body, 0)

    o = jnp.concatenate([oacc_ref[h] for h in heads], axis=1)
    ms = _dot_exact_rhs(o * o, head_ones) * (1.0 / DN_HEAD_DIM)
    z = z_ref[...]
    o_ref[...] = (o * lax.rsqrt(ms + NORM_EPS) * dnw_ref[...] * (z * _sigmoid(z))).astype(o_ref.dtype)


def _deltanet_masks(lb):
    head = jnp.arange(DN_WIDTH) // DN_HEAD_DIM
    head_ones = head[:, None] == head[None, :]
    pos = jnp.arange(lb)
    same = (pos[:, None] // CHUNK) == (pos[None, :] // CHUNK)
    tril = same & (pos[None, :] <= pos[:, None])
    triu = same & (pos[:, None] <= pos[None, :])
    expand = jnp.arange(LANES)[:, None] == head[None, :]
    return tuple(m.astype(BF16) for m in (head_ones, tril, triu, same, expand))


def _deltanet(qkv, z, ba, bat, alog_r, dtb_r, alog_c, dtb_c, dnw, batch, seq):
    lb = DN_LB
    nb = seq // lb
    slots = lb // CHUNK * DN_HEADS
    masks = _deltanet_masks(lb)
    full = lambda a: pl.BlockSpec(a.shape, lambda b, j: (0,) * a.ndim)
    row = lambda n: pl.BlockSpec((lb, n), lambda b, j: (b * nb + j, 0))
    return pl.pallas_call(
        _deltanet_kernel,
        grid=(batch, nb),
        in_specs=[row(3 * DN_WIDTH), row(DN_WIDTH), row(2 * LANES),
                  pl.BlockSpec((2 * SUBLANES, lb), lambda b, j: (0, b * nb + j)),
                  full(alog_r), full(dtb_r), full(alog_c), full(dtb_c), full(dnw)] + [full(m) for m in masks],
        out_specs=row(DN_WIDTH),
        out_shape=jax.ShapeDtypeStruct((batch * seq, DN_WIDTH), BF16),
        scratch_shapes=[
            pltpu.VMEM((DN_HEADS, DN_HEAD_DIM, DN_HEAD_DIM), F32),
            pltpu.VMEM((lb, DN_WIDTH), BF16),
            pltpu.VMEM((lb, DN_WIDTH), BF16),
            pltpu.VMEM((lb, DN_WIDTH), BF16),
            pltpu.VMEM((lb, DN_WIDTH), BF16),
            pltpu.VMEM((lb, DN_WIDTH), BF16),
            pltpu.VMEM((lb, DN_WIDTH), BF16),
            pltpu.VMEM((lb, DN_WIDTH), F32),
            pltpu.VMEM((lb, DN_WIDTH), F32),
            pltpu.VMEM((lb, DN_WIDTH), F32),
            pltpu.VMEM((lb // CHUNK, DN_HEADS, CHUNK), F32),
            pltpu.VMEM((DN_HEADS, lb, DN_HEAD_DIM), F32),
            pltpu.VMEM((slots, CHUNK, CHUNK), BF16),
            pltpu.VMEM((slots, 2 * CHUNK, DN_HEAD_DIM), BF16),
            pltpu.VMEM((slots, CHUNK, DN_HEAD_DIM), F32),
            pltpu.VMEM((slots, DN_HEAD_DIM, CHUNK), BF16),
            pltpu.VMEM((slots, DN_HEAD_DIM, DN_HEAD_DIM), F32),
        ],
        compiler_params=pltpu.CompilerParams(dimension_semantics=("arbitrary", "arbitrary"),
                                             vmem_limit_bytes=VMEM_LIMIT),
        name="deltanet",
    )(qkv, z, ba, bat, alog_r, dtb_r, alog_c, dtb_c, dnw, *masks)


def _mix_kernel(x_ref, on_ref, u_ref, gates_ref, wupa_ref, poolw_ref, pscale_ref, wupb_ref, wout_ref,
                gffn_ref, wr_hi_ref, wr_lo_ref, br_ref,
                x1_ref, h2_ref, route_ref, counts_ref, ucarry_ref, ecount_ref):
    tm = x_ref.shape[0]
    j = pl.program_id(1)

    @pl.when(j == 0)
    def _():
        ucarry_ref[...] = jnp.zeros_like(ucarry_ref)

    @pl.when((j == 0) & (pl.program_id(0) == 0))
    def _():
        ecount_ref[...] = jnp.zeros_like(ecount_ref)

    u = u_ref[...]
    ext = jnp.concatenate([ucarry_ref[...], u], axis=0)
    ucarry_ref[...] = u[tm - POOL_HALO:tm]
    t_pos = j * tm + lax.broadcasted_iota(jnp.int32, (tm, POOL_GROUP_DIM), 0)
    ys = []
    for g in range(POOL_GROUPS):
        s = ext[:, g * POOL_GROUP_DIM:(g + 1) * POOL_GROUP_DIM]
        shift = 1
        while shift < POOL_WINDOWS[g]:
            s = s + pltpu.roll(s, shift, 0)
            shift *= 2
        count = jnp.minimum(t_pos + 1, POOL_WINDOWS[g]).astype(F32)
        pooled = s[POOL_HALO:] / count - u[:, g * POOL_GROUP_DIM:(g + 1) * POOL_GROUP_DIM]
        ys.append(_dot(pooled.astype(BF16), poolw_ref[g]))
    yb = jnp.concatenate(ys, axis=-1) * pscale_ref[...]
    y_b = _dot(yb.astype(BF16), wupb_ref[...])
    y_a = _dot(on_ref[...], wupa_ref[...])
    gates = gates_ref[...]
    merged = _sigmoid(gates[:, 0:D_MODEL]) * y_a + _sigmoid(gates[:, D_MODEL:2 * D_MODEL]) * y_b
    x1 = x_ref[...] + _dot(merged.astype(BF16), wout_ref[...])
    x1_ref[...] = x1

    h2 = _rms(x1, gffn_ref[...])
    _store_row_tiles(h2_ref, h2)

    hi, lo = _split2(h2)
    logits = _dot(hi, wr_hi_ref[...]) + _dot(lo, wr_hi_ref[...]) + _dot(hi, wr_lo_ref[...]) + br_ref[...]
    lane = lax.broadcasted_iota(jnp.int32, (tm, LANES), 1)
    lg = jnp.where(lane < N_EXPERTS, logits, -jnp.inf)
    vals, idxs, sels = [], [], []
    for _ in range(TOP_K):
        m = jnp.max(lg, axis=-1, keepdims=True)
        idx = jnp.min(jnp.where(lg == m, lane, LANES), axis=-1, keepdims=True)
        sel = lane == idx
        vals.append(m)
        idxs.append(idx)
        sels.append(sel)
        lg = jnp.where(sel, -jnp.inf, lg)
    es = [jnp.exp(vk - vals[0]) for vk in vals]
    denom = es[0] + es[1] + es[2] + es[3]

    chosen = jnp.zeros((tm, LANES), F32)
    for sel in sels:
        chosen = chosen + jnp.where(sel, 1.0, 0.0)
    ri = lax.broadcasted_iota(jnp.int32, (tm, tm), 0)
    rj = lax.broadcasted_iota(jnp.int32, (tm, tm), 1)
    before = (rj < ri).astype(BF16)
    pos = ecount_ref[...] + _dot(before, chosen.astype(BF16))
    ecount_ref[...] = ecount_ref[...] + jnp.sum(chosen, axis=0, keepdims=True)
    counts_ref[...] = ecount_ref[...]

    route = jnp.zeros((tm, LANES), F32)
    for kk in range(TOP_K):
        rank = jnp.sum(jnp.where(sels[kk], pos, 0.0), axis=-1, keepdims=True)
        route = jnp.where(lane == kk, rank, route)
        route = jnp.where(lane == TOP_K + kk, idxs[kk].astype(F32), route)
        route = jnp.where(lane == 2 * TOP_K + kk, es[kk] / denom, route)
    route_ref[...] = route


def _mix(x2, on, u, gates, wupa, poolw, pscale, wupb, wout, gffn, wr_hi, wr_lo, br, batch, seq):
    tm = MIX_TM
    nb = seq // tm
    t = batch * seq
    full = lambda a: pl.BlockSpec(a.shape, lambda b, j: (0,) * a.ndim)
    row = lambda n: pl.BlockSpec((tm, n), lambda b, j: (b * nb + j, 0))
    return pl.pallas_call(
        _mix_kernel,
        grid=(batch, nb),
        in_specs=[row(D_MODEL), row(DN_WIDTH), row(POOL_WIDTH), row(2 * D_MODEL),
                  full(wupa), full(poolw), full(pscale), full(wupb), full(wout), full(gffn),
                  full(wr_hi), full(wr_lo), full(br)],
        out_specs=[row(D_MODEL), pl.BlockSpec((tm * ROW_SUB, LANES), lambda b, j: (b * nb + j, 0)), row(LANES),
                   pl.BlockSpec((1, LANES), lambda b, j: (0, 0))],
        out_shape=[jax.ShapeDtypeStruct((t, D_MODEL), F32),
                   jax.ShapeDtypeStruct((t * ROW_SUB, LANES), F32),
                   jax.ShapeDtypeStruct((t, LANES), F32),
                   jax.ShapeDtypeStruct((1, LANES), F32)],
        scratch_shapes=[pltpu.VMEM((POOL_HALO, POOL_WIDTH), F32), pltpu.VMEM((1, LANES), F32)],
        compiler_params=pltpu.CompilerParams(dimension_semantics=("arbitrary", "arbitrary"),
                                             vmem_limit_bytes=VMEM_LIMIT),
        name="mix",
    )(x2, on, u, gates, wupa, poolw, pscale, wupb, wout, gffn, wr_hi, wr_lo, br)


def _routing_tables(route, counts, tg, n_tiles):
    cnt = counts[0, 0:N_EXPERTS].astype(jnp.int32)
    padded = (cnt + (tg - 1)) // tg * tg
    ends = jnp.cumsum(padded)
    offs = ends - padded
    rank = route[:, 0:TOP_K].astype(jnp.int32)
    eidx = route[:, TOP_K:2 * TOP_K].astype(jnp.int32)
    experts = jnp.arange(N_EXPERTS, dtype=jnp.int32)
    dest = jnp.sum(jnp.where(eidx[..., None] == experts, offs, 0), axis=-1) + rank
    n_used = ends[N_EXPERTS - 1] // tg
    tile_start = jnp.arange(n_tiles, dtype=jnp.int32) * tg
    tile_expert = jnp.sum((tile_start[:, None] >= ends[None, :]).astype(jnp.int32), axis=1)
    tile_expert = jnp.minimum(tile_expert, N_EXPERTS - 1)
    last = tile_expert[jnp.maximum(n_used - 1, 0)]
    tile_expert = jnp.where(jnp.arange(n_tiles) < n_used, tile_expert, last)
    pad_start = offs + cnt
    pad_len = padded - cnt
    return (dest, tile_expert.astype(jnp.int32), n_used.reshape(1).astype(jnp.int32),
            pad_start.astype(jnp.int32), pad_len.astype(jnp.int32))


def _dispatch_kernel(pad_start_ref, pad_len_ref, nu_ref, dest_ref, h2_ref, xs_ref, zero_ref, sem, zsem):
    tm = h2_ref.shape[0] // ROW_SUB
    tg = zero_ref.shape[0]
    n_tiles = xs_ref.shape[0] // tg

    @pl.when(pl.program_id(0) == 0)
    def _():
        zero_ref[...] = jnp.zeros_like(zero_ref)

        def pad_copy(d):
            return pltpu.make_async_copy(_row_tile(zero_ref, 0), _row_tile(xs_ref, d), zsem)

        def tile_copy(i):
            return pltpu.make_async_copy(zero_ref, xs_ref.at[pl.ds(pl.multiple_of(i * tg, tg), tg)], zsem)

        def per_expert(e, c):
            lax.fori_loop(0, pad_len_ref[e], lambda r, cc: (pad_copy(pad_start_ref[e] + r).start(), cc)[1], 0)
            lax.fori_loop(0, pad_len_ref[e], lambda r, cc: (pad_copy(0).wait(), cc)[1], 0)
            return c

        lax.fori_loop(0, N_EXPERTS, per_expert, 0)
        lax.fori_loop(nu_ref[0], n_tiles, lambda i, cc: (tile_copy(i).start(), cc)[1], 0)
        lax.fori_loop(nu_ref[0], n_tiles, lambda i, cc: (tile_copy(0).wait(), cc)[1], 0)

    def row_copy(r, d):
        return pltpu.make_async_copy(_row_tile(h2_ref, r), _row_tile(xs_ref, d), sem)

    def start(r, c):
        for kk in range(TOP_K):
            row_copy(r, dest_ref[0, 0, r * TOP_K + kk]).start(priority=kk % 2)
        return c

    lax.fori_loop(0, tm, start, 0, unroll=DMA_UNROLL)
    for kk in range(TOP_K):
        pltpu.make_async_copy(h2_ref, xs_ref.at[pl.ds(0, tm * ROW_SUB)], sem).wait()


def _dispatch(pad_start, pad_len, n_used, dest3, h2, n_rows):
    t = h2.shape[0] // ROW_SUB
    tm = DSP_TM
    return pl.pallas_call(
        _dispatch_kernel,
        grid_spec=pltpu.PrefetchScalarGridSpec(
            num_scalar_prefetch=3,
            grid=(t // tm,),
            in_specs=[pl.BlockSpec((1, 1, tm * TOP_K), lambda i, ps, pn, nu: (i, 0, 0), memory_space=pltpu.SMEM),
                      pl.BlockSpec((tm * ROW_SUB, LANES), lambda i, ps, pn, nu: (i, 0))],
            out_specs=pl.BlockSpec(memory_space=pl.ANY),
            scratch_shapes=[pltpu.VMEM((MOE_TG * ROW_SUB, LANES), F32), pltpu.SemaphoreType.DMA,
                            pltpu.SemaphoreType.DMA]),
        out_shape=jax.ShapeDtypeStruct((n_rows * ROW_SUB, LANES), F32),
        compiler_params=pltpu.CompilerParams(dimension_semantics=("arbitrary",), vmem_limit_bytes=VMEM_LIMIT),
        name="dispatch",
    )(pad_start, pad_len, n_used, dest3, h2)


def _experts_kernel(te_ref, nu_ref, xs_ref, wgu_ref, bgu_ref, wd_ref, bd_ref, ys_ref, wgu_bf_ref, wd_bf_ref):
    i = pl.program_id(0)

    @pl.when((i == 0) | (te_ref[i] != te_ref[jnp.maximum(i - 1, 0)]))
    def _():
        wgu_bf_ref[...] = wgu_ref[...].astype(BF16)
        wd_bf_ref[...] = wd_ref[...].astype(BF16)

    @pl.when(i < nu_ref[0])
    def _():
        gu = _dot(_load_row_tiles(xs_ref).astype(BF16), wgu_bf_ref[...]) + bgu_ref[...]
        gate = jnp.minimum(gu[:, 0:D_FF], SWIGLU_LIMIT)
        up = jnp.clip(gu[:, D_FF:2 * D_FF], -SWIGLU_LIMIT, SWIGLU_LIMIT)
        act = gate * _sigmoid(SWIGLU_ALPHA * gate) * (up + 1.0)
        _store_row_tiles(ys_ref, _dot(act.astype(BF16), wd_bf_ref[...]) + bd_ref[...])

    @pl.when(i >= nu_ref[0])
    def _():
        ys_ref[...] = jnp.zeros_like(ys_ref)


def _experts(tile_expert, n_used, xs, wgu, bgu, wd, bd):
    tg = MOE_TG * ROW_SUB
    n_tiles = xs.shape[0] // tg
    tile = lambda i, te, nu: (jnp.minimum(i, nu[0] - 1), 0)
    expert = lambda i, te, nu: (te[i], 0, 0)
    return pl.pallas_call(
        _experts_kernel,
        grid_spec=pltpu.PrefetchScalarGridSpec(
            num_scalar_prefetch=2,
            grid=(n_tiles,),
            in_specs=[pl.BlockSpec((tg, LANES), tile),
                      pl.BlockSpec((None, D_MODEL, 2 * D_FF), expert),
                      pl.BlockSpec((None, 1, 2 * D_FF), expert),
                      pl.BlockSpec((None, D_FF, D_MODEL), expert),
                      pl.BlockSpec((None, 1, D_MODEL), expert)],
            out_specs=pl.BlockSpec((tg, LANES), lambda i, te, nu: (i, 0)),
            scratch_shapes=[pltpu.VMEM((D_MODEL, 2 * D_FF), BF16), pltpu.VMEM((D_FF, D_MODEL), BF16)]),
        out_shape=jax.ShapeDtypeStruct(xs.shape, F32),
        compiler_params=pltpu.CompilerParams(dimension_semantics=("arbitrary",), vmem_limit_bytes=VMEM_LIMIT),
        name="experts",
    )(tile_expert, n_used, xs, wgu, bgu, wd, bd)


def _combine_kernel(dest_ref, dest_next_ref, route_ref, x1_ref, gfin_ref, ys_ref, out_ref, ybuf_ref, sem):
    i = pl.program_id(0)
    tm = x1_ref.shape[0]

    def gather_rows(dref, buf):
        def start(r, c):
            for kk in range(TOP_K):
                pltpu.make_async_copy(_row_tile(ys_ref, dref[0, 0, r * TOP_K + kk]),
                                      _row_tile(ybuf_ref.at[buf, kk], r), sem.at[buf]).start(priority=kk % 2)
            return c
        lax.fori_loop(0, tm, start, 0, unroll=DMA_UNROLL)

    @pl.when(i == 0)
    def _():
        gather_rows(dest_ref, 0)

    @pl.when(i + 1 < pl.num_programs(0))
    def _():
        gather_rows(dest_next_ref, (i + 1) % 2)

    buf = i % 2
    for kk in range(TOP_K):
        pltpu.make_async_copy(ys_ref.at[pl.ds(0, tm * ROW_SUB)], ybuf_ref.at[buf, kk], sem.at[buf]).wait()
    route = route_ref[...]
    acc = x1_ref[...]
    for kk in range(TOP_K):
        acc = acc + route[:, 2 * TOP_K + kk:2 * TOP_K + kk + 1] * _load_row_tiles(ybuf_ref.at[buf, kk])
    out_ref[...] = _rms(acc, gfin_ref[...])


def _combine(dest3, route, x1, gfin, ys):
    t = x1.shape[0]
    tm = CMB_TM
    last = t // tm - 1
    return pl.pallas_call(
        _combine_kernel,
        grid=(t // tm,),
        in_specs=[pl.BlockSpec((1, 1, tm * TOP_K), lambda i: (i, 0, 0), memory_space=pltpu.SMEM),
                  pl.BlockSpec((1, 1, tm * TOP_K), lambda i: (jnp.minimum(i + 1, last), 0, 0),
                               memory_space=pltpu.SMEM),
                  pl.BlockSpec((tm, LANES), lambda i: (i, 0)),
                  pl.BlockSpec((tm, D_MODEL), lambda i: (i, 0)),
                  pl.BlockSpec((1, D_MODEL), lambda i: (0, 0)),
                  pl.BlockSpec(memory_space=pl.ANY)],
        out_specs=pl.BlockSpec((tm, D_MODEL), lambda i: (i, 0)),
        out_shape=jax.ShapeDtypeStruct((t, D_MODEL), F32),
        scratch_shapes=[pltpu.VMEM((2, TOP_K, tm * ROW_SUB, LANES), F32), pltpu.SemaphoreType.DMA((2,))],
        compiler_params=pltpu.CompilerParams(dimension_semantics=("arbitrary",), vmem_limit_bytes=VMEM_LIMIT),
        name="combine",
    )(dest3, dest3, route, x1, gfin, ys)


def kernel(x, g_mix, w_in, conv_w, a_log, dt_bias, dn_norm, w_up_a, pool_w, pool_scale, w_up_b, w_out, g_ffn,
           w_router, b_router, w_gate_up, b_gate_up, w_down, b_down, g_final):
    batch, seq, d = x.shape
    assert d == D_MODEL and seq % DN_LB == 0 and seq % MIX_TM == 0
    assert (batch * seq) % DSP_TM == 0 and (batch * seq * TOP_K) % MOE_TG == 0
    assert g_mix.shape[0] == 1, "one layer"
    t = batch * seq
    x2 = x.reshape(t, d)

    w = w_in[0]
    o_z = 3 * DN_WIDTH
    o_b = o_z + DN_WIDTH
    o_a = o_b + DN_HEADS
    o_u = o_a + DN_HEADS
    o_g = o_u + POOL_WIDTH
    wqkv = w[:, 0:o_z].astype(BF16)
    wz = w[:, o_z:o_b].astype(BF16)
    wu = w[:, o_u:o_g].astype(BF16)
    wg = w[:, o_g:].astype(BF16)
    w_b = w[:, o_b:o_a]
    w_a = w[:, o_a:o_u]
    pad = jnp.zeros((d, LANES - DN_HEADS), F32)
    wba = jnp.concatenate([w_b, pad, w_a, pad], axis=1).astype(BF16)
    wbat = jnp.concatenate([w_b, w_a], axis=1).T.astype(BF16)

    qkv, z, u, gates, ba, bat = _in_proj(x2, g_mix, wqkv, wz, wu, wg, wba, wbat, conv_w[0], seq)

    lane_pad = lambda p: jnp.pad(p.reshape(1, DN_HEADS), ((0, 0), (0, LANES - DN_HEADS)))
    on = _deltanet(qkv, z, ba, bat, lane_pad(a_log[0]), lane_pad(dt_bias[0]),
                   a_log[0].reshape(DN_HEADS, 1), dt_bias[0].reshape(DN_HEADS, 1),
                   jnp.tile(dn_norm[0], DN_HEADS).reshape(1, DN_WIDTH), batch, seq)

    wr = jnp.pad(w_router[0], ((0, 0), (0, LANES - N_EXPERTS)))
    wr_hi = wr.astype(BF16)
    wr_lo = (wr - wr_hi.astype(F32)).astype(BF16)
    br = jnp.pad(b_router[0].reshape(1, N_EXPERTS), ((0, 0), (0, LANES - N_EXPERTS)))
    x1, h2, route, counts = _mix(x2, on, u, gates, w_up_a[0].astype(BF16), pool_w[0].astype(BF16),
                                  pool_scale[0].reshape(1, POOL_WIDTH), w_up_b[0].astype(BF16),
                                  w_out[0].astype(BF16), g_ffn, wr_hi, wr_lo, br, batch, seq)

    n_tiles = t * TOP_K // MOE_TG + N_EXPERTS
    dest, tile_expert, n_used, pad_start, pad_len = _routing_tables(route, counts, MOE_TG, n_tiles)
    xs = _dispatch(pad_start, pad_len, n_used, dest.reshape(t // DSP_TM, 1, DSP_TM * TOP_K), h2, n_tiles * MOE_TG)
    ys = _experts(tile_expert, n_used, xs, w_gate_up[0], b_gate_up[0].reshape(N_EXPERTS, 1, 2 * D_FF),
                  w_down[0], b_down[0].reshape(N_EXPERTS, 1, D_MODEL))
    out = _combine(dest.reshape(t // CMB_TM, 1, CMB_TM * TOP_K), route, x1, g_final.reshape(1, D_MODEL), ys)
    return out.reshape(batch, seq, d)
```

```python
import functools

import jax
import jax.numpy as jnp
from jax import lax
from jax.experimental import pallas as pl
from jax.experimental.pallas import tpu as pltpu

F32 = jnp.float32
BF16 = jnp.bfloat16

D_MODEL = 1024
CHUNK = 64
DN_HEADS = 8
DN_HEAD_DIM = 64
DN_WIDTH = DN_HEADS * DN_HEAD_DIM
CONV_WIDTH = 4
POOL_GROUPS = 4
POOL_WINDOWS = (2, 4, 8, 16)
POOL_WIDTH = 512
POOL_GROUP_DIM = 128
POOL_HALO = 16
N_EXPERTS = 32
TOP_K = 4
D_FF = D_MODEL
SWIGLU_LIMIT = 7.0
SWIGLU_ALPHA = 1.702
NORM_EPS = 1e-6
LANES = 128
SUBLANES = 8
VMEM_LIMIT = 56 * 1024 * 1024

IN_TM = 256
DN_LB = 256
INTRA_GROUP = 2
MIX_TM = 512
MIX_SPLIT = 2
MOE_TG = 512
DSP_TM = 1024
CMB_TM = 512
DMA_UNROLL = 8


def _dot(a, b):
    return jnp.dot(a, b, preferred_element_type=F32)


def _dot_nt(a, b):
    return lax.dot_general(a, b, (((1,), (1,)), ((), ())), preferred_element_type=F32)


def _dot_tn(a, b):
    return lax.dot_general(a, b, (((0,), (0,)), ((), ())), preferred_element_type=F32)


def _split2(x):
    hi = x.astype(BF16)
    lo = (x - hi.astype(F32)).astype(BF16)
    return hi, lo


def _dot_exact_rhs(x, m):
    hi, lo = _split2(x)
    return _dot(hi, m) + _dot(lo, m)


def _dot_exact_lhs(m, x):
    hi, lo = _split2(x)
    return _dot(m, hi) + _dot(m, lo)


def _softplus(x):
    return jnp.maximum(x, 0.0) + jnp.log1p(jnp.exp(-jnp.abs(x)))


def _sigmoid(x):
    return 1.0 / (1.0 + jnp.exp(-x))


def _rms(x, g):
    return x * lax.rsqrt(jnp.mean(x * x, axis=-1, keepdims=True) + NORM_EPS) * g


ROW_SUB = D_MODEL // LANES


def _row_tile(ref, r):
    return ref.at[pl.ds(pl.multiple_of(r * ROW_SUB, ROW_SUB), ROW_SUB)]


def _store_row_tiles(ref, x):
    for j in range(ROW_SUB):
        ref[pl.ds(j, x.shape[0], stride=ROW_SUB), :] = x[:, j * LANES:(j + 1) * LANES]


def _load_row_tiles(ref):
    n = ref.shape[0] // ROW_SUB
    return jnp.concatenate([ref[pl.ds(j, n, stride=ROW_SUB), :] for j in range(ROW_SUB)], axis=1)


def _in_proj_kernel(x_ref, g_ref, wqkv_ref, wz_ref, wu_ref, wg_ref, wba_ref, wbat_ref, convw_ref,
                    qkv_ref, z_ref, u_ref, gates_ref, ba_ref, bat_ref, hist_ref, *, steps_per_seq):
    tm = x_ref.shape[0]
    hb = _rms(x_ref[...], g_ref[...]).astype(BF16)

    @pl.when(pl.program_id(0) % steps_per_seq == 0)
    def _():
        hist_ref[0:SUBLANES, :] = jnp.zeros((SUBLANES, hist_ref.shape[1]), F32)

    blk = _dot(hb, wqkv_ref[...])
    cw = convw_ref[...]
    hist_ref[SUBLANES:SUBLANES + tm, :] = blk
    act = blk * cw[CONV_WIDTH - 1:CONV_WIDTH]
    for s in range(1, CONV_WIDTH):
        act = act + hist_ref[pl.ds(SUBLANES - s, tm), :] * cw[CONV_WIDTH - 1 - s:CONV_WIDTH - s]
    hist_ref[0:SUBLANES, :] = blk[tm - SUBLANES:tm]
    qkv_ref[...] = act * _sigmoid(act)

    z_ref[...] = _dot(hb, wz_ref[...])
    u_ref[...] = _dot(hb, wu_ref[...])
    gates_ref[...] = _dot(hb, wg_ref[...])
    ba_ref[...] = _dot(hb, wba_ref[...])
    bat_ref[...] = _dot_nt(wbat_ref[...], hb)


def _in_proj(x2, g_mix, wqkv, wz, wu, wg, wba, wbat, conv_w, seq):
    t = x2.shape[0]
    tm = IN_TM
    full = lambda a: pl.BlockSpec(a.shape, lambda i: (0, 0))
    row = lambda n: pl.BlockSpec((tm, n), lambda i: (i, 0))
    return pl.pallas_call(
        functools.partial(_in_proj_kernel, steps_per_seq=seq // tm),
        grid=(t // tm,),
        in_specs=[row(D_MODEL), full(g_mix), full(wqkv), full(wz), full(wu), full(wg), full(wba), full(wbat),
                  full(conv_w)],
        out_specs=[row(3 * DN_WIDTH), row(DN_WIDTH), row(POOL_WIDTH), row(2 * D_MODEL), row(2 * LANES),
                   pl.BlockSpec((2 * SUBLANES, tm), lambda i: (0, i))],
        out_shape=[jax.ShapeDtypeStruct((t, 3 * DN_WIDTH), F32),
                   jax.ShapeDtypeStruct((t, DN_WIDTH), F32),
                   jax.ShapeDtypeStruct((t, POOL_WIDTH), F32),
                   jax.ShapeDtypeStruct((t, 2 * D_MODEL), F32),
                   jax.ShapeDtypeStruct((t, 2 * LANES), F32),
                   jax.ShapeDtypeStruct((2 * SUBLANES, t), F32)],
        scratch_shapes=[pltpu.VMEM((SUBLANES + tm, 3 * DN_WIDTH), F32)],
        compiler_params=pltpu.CompilerParams(dimension_semantics=("arbitrary",), vmem_limit_bytes=VMEM_LIMIT),
        name="in_proj",
    )(x2, g_mix, wqkv, wz, wu, wg, wba, wbat, conv_w)


def _deltanet_kernel(qkv_ref, z_ref, ba_ref, bat_ref, alog_r_ref, dtb_r_ref, alog_c_ref, dtb_c_ref, dnw_ref,
                     head_ones_ref, tril_ref, triu_ref, chunk_ones_ref, expand_ref, o_ref,
                     s_ref, qn_ref, kn_ref, kbe_ref, vb_ref, qg_ref, kd_ref,
                     xbeta_ref, xgc_ref, xgl_ref, gr_ref, oacc_ref, qkd_ref, wq_ref, u_ref, kdt_ref, gl_ref):
    lb = qkv_ref.shape[0]
    n_chunks = lb // CHUNK

    @pl.when(pl.program_id(1) == 0)
    def _():
        s_ref[...] = jnp.zeros_like(s_ref)

    q = qkv_ref[:, 0:DN_WIDTH]
    k = qkv_ref[:, DN_WIDTH:2 * DN_WIDTH]
    v = qkv_ref[:, 2 * DN_WIDTH:3 * DN_WIDTH]

    head_ones = head_ones_ref[...]
    qn = q * lax.rsqrt(_dot_exact_rhs(q * q, head_ones) + NORM_EPS) * (DN_HEAD_DIM ** -0.5)
    kn = k * lax.rsqrt(_dot_exact_rhs(k * k, head_ones) + NORM_EPS)

    ba = ba_ref[...]
    beta_c = _sigmoid(ba[:, 0:LANES])
    g_c = -jnp.exp(alog_r_ref[...]) * _softplus(ba[:, LANES:2 * LANES] + dtb_r_ref[...])
    lane = lax.broadcasted_iota(jnp.int32, (lb, LANES), 1)
    g_c = jnp.where(lane < DN_HEADS, g_c, 0.0)
    bat = bat_ref[...]
    g_r = -jnp.exp(alog_c_ref[...]) * _softplus(bat[SUBLANES:2 * SUBLANES] + dtb_c_ref[...])

    gc = _dot_exact_lhs(tril_ref[...], g_c)
    gtot = _dot_exact_lhs(chunk_ones_ref[...], g_c)
    gr = _dot_exact_rhs(g_r, triu_ref[...])
    for c in range(n_chunks):
        gr_ref[c] = gr[:, c * CHUNK:(c + 1) * CHUNK]

    expand = expand_ref[...]
    x_beta = _dot_exact_rhs(beta_c, expand)
    x_gc = _dot_exact_rhs(gc, expand)
    x_gtot = _dot_exact_rhs(gtot, expand)
    x_eg = jnp.exp(x_gc)
    xbeta_ref[...] = x_beta
    xgc_ref[...] = x_gc
    xgl_ref[...] = jnp.exp(x_gtot)
    qn_ref[...] = qn.astype(BF16)
    kn_ref[...] = kn.astype(BF16)
    kbe_ref[...] = (kn * (x_beta * x_eg)).astype(BF16)
    vb_ref[...] = (v * x_beta).astype(BF16)
    qg_ref[...] = (qn * x_eg).astype(BF16)
    kd_ref[...] = (kn * jnp.exp(x_gtot - x_gc)).astype(BF16)

    ci = lax.broadcasted_iota(jnp.int32, (CHUNK, CHUNK), 0)
    cj = lax.broadcasted_iota(jnp.int32, (CHUNK, CHUNK), 1)
    eye = (ci == cj).astype(F32)

    heads = range(DN_HEADS)
    lanes = [pl.ds(h * DN_HEAD_DIM, DN_HEAD_DIM) for h in heads]
    stack = lambda top, bot: jnp.concatenate([top, bot], axis=0)

    chunk_rows = lambda c: pl.ds(pl.multiple_of(c * CHUNK, CHUNK), CHUNK)

    def within_chunk_body(g, carry_unused):
        chains = [(g * INTRA_GROUP + cc, h) for cc in range(INTRA_GROUP) for h in heads]
        ids = range(len(chains))
        rows = [chunk_rows(c) for c, _ in chains]
        ln = [lanes[h] for _, h in chains]
        kb = [kn_ref[rows[i], ln[i]] for i in ids]
        kq = [_dot_nt(stack(kb[i], qn_ref[rows[i], ln[i]]), kb[i]) for i in ids]
        decay = [jnp.exp(jnp.where(ci >= cj, xgc_ref[rows[i], ln[i]] - gr_ref[chains[i][0], h:h + 1, :], -jnp.inf))
                 for i, (_, h) in enumerate(chains)]
        a = [jnp.where(ci > cj, xbeta_ref[rows[i], ln[i]] * kq[i][0:CHUNK] * decay[i], 0.0) for i in ids]
        slot = [c * DN_HEADS + h for c, h in chains]
        for i in ids:
            qkd_ref[slot[i]] = (kq[i][CHUNK:2 * CHUNK] * decay[i]).astype(BF16)
            wq_ref[slot[i], CHUNK:2 * CHUNK, :] = qg_ref[rows[i], ln[i]]
            kdt_ref[slot[i]] = kd_ref[rows[i], ln[i]].T.astype(BF16)
            gl_ref[slot[i]] = xgl_ref[rows[i], ln[i]]
        t_inv = [eye - a[i] for i in ids]
        pw = [a[i].astype(BF16) for i in ids]
        pw = [_dot(pw[i], pw[i]).astype(BF16) for i in ids]
        for _ in range(4):
            r = [_dot(stack(t_inv[i].astype(BF16), pw[i]), pw[i]) for i in ids]
            t_inv = [t_inv[i] + r[i][0:CHUNK] for i in ids]
            pw = [r[i][CHUNK:2 * CHUNK].astype(BF16) for i in ids]
        t_inv = [t_inv[i] + _dot(t_inv[i].astype(BF16), pw[i]) for i in ids]
        tb = [t_inv[i].astype(BF16) for i in ids]
        for i in ids:
            wq_ref[slot[i], 0:CHUNK, :] = _dot(tb[i], kbe_ref[rows[i], ln[i]]).astype(BF16)
        for i in ids:
            u_ref[slot[i]] = _dot(tb[i], vb_ref[rows[i], ln[i]])
        return carry_unused

    def state_body(c, carry_unused):
        rows = chunk_rows(c)
        slot = [c * DN_HEADS + h for h in heads]
        s = [s_ref[h] for h in heads]
        ws = [_dot(wq_ref[slot[h]], s[h].astype(BF16)) for h in heads]
        vnb = [(u_ref[slot[h]] - ws[h][0:CHUNK]).astype(BF16) for h in heads]
        for h in heads:
            oacc_ref[h, rows, :] = ws[h][CHUNK:2 * CHUNK] + _dot(qkd_ref[slot[h]], vnb[h])
        for h in heads:
            s_ref[h] = s[h] * gl_ref[slot[h]] + _dot(kdt_ref[slot[h]], vnb[h])
        return carry_unused

    lax.fori_loop(0, n_chunks // INTRA_GROUP, within_chunk_body, 0)
    lax.fori_loop(0, n_chunks, state_body, 0)

    o = jnp.concatenate([oacc_ref[h] for h in heads], axis=1)
    ms = _dot_exact_rhs(o * o, head_ones) * (1.0 / DN_HEAD_DIM)
    z = z_ref[...]
    o_ref[...] = (o * lax.rsqrt(ms + NORM_EPS) * dnw_ref[...] * (z * _sigmoid(z))).astype(o_ref.dtype)


def _deltanet_masks(lb):
    head = jnp.arange(DN_WIDTH) // DN_HEAD_DIM
    head_ones = head[:, None] == head[None, :]
    pos = jnp.arange(lb)
    same = (pos[:, None] // CHUNK) == (pos[None, :] // CHUNK)
    tril = same & (pos[None, :] <= pos[:, None])
    triu = same & (pos[:, None] <= pos[None, :])
    expand = jnp.arange(LANES)[:, None] == head[None, :]
    return tuple(m.astype(BF16) for m in (head_ones, tril, triu, same, expand))


def _deltanet(qkv, z, ba, bat, alog_r, dtb_r, alog_c, dtb_c, dnw, batch, seq):
    lb = DN_LB
    nb = seq // lb
    slots = lb // CHUNK * DN_HEADS
    masks = _deltanet_masks(lb)
    full = lambda a: pl.BlockSpec(a.shape, lambda b, j: (0,) * a.ndim)
    row = lambda n: pl.BlockSpec((lb, n), lambda b, j: (b * nb + j, 0))
    return pl.pallas_call(
        _deltanet_kernel,
        grid=(batch, nb),
        in_specs=[row(3 * DN_WIDTH), row(DN_WIDTH), row(2 * LANES),
                  pl.BlockSpec((2 * SUBLANES, lb), lambda b, j: (0, b * nb + j)),
                  full(alog_r), full(dtb_r), full(alog_c), full(dtb_c), full(dnw)] + [full(m) for m in masks],
        out_specs=row(DN_WIDTH),
        out_shape=jax.ShapeDtypeStruct((batch * seq, DN_WIDTH), BF16),
        scratch_shapes=[
            pltpu.VMEM((DN_HEADS, DN_HEAD_DIM, DN_HEAD_DIM), F32),
            pltpu.VMEM((lb, DN_WIDTH), BF16),
            pltpu.VMEM((lb, DN_WIDTH), BF16),
            pltpu.VMEM((lb, DN_WIDTH), BF16),
            pltpu.VMEM((lb, DN_WIDTH), BF16),
            pltpu.VMEM((lb, DN_WIDTH), BF16),
            pltpu.VMEM((lb, DN_WIDTH), BF16),
            pltpu.VMEM((lb, DN_WIDTH), F32),
            pltpu.VMEM((lb, DN_WIDTH), F32),
            pltpu.VMEM((lb, DN_WIDTH), F32),
            pltpu.VMEM((lb // CHUNK, DN_HEADS, CHUNK), F32),
            pltpu.VMEM((DN_HEADS, lb, DN_HEAD_DIM), F32),
            pltpu.VMEM((slots, CHUNK, CHUNK), BF16),
            pltpu.VMEM((slots, 2 * CHUNK, DN_HEAD_DIM), BF16),
            pltpu.VMEM((slots, CHUNK, DN_HEAD_DIM), F32),
            pltpu.VMEM((slots, DN_HEAD_DIM, CHUNK), BF16),
            pltpu.VMEM((slots, DN_HEAD_DIM, DN_HEAD_DIM), F32),
        ],
        compiler_params=pltpu.CompilerParams(dimension_semantics=("arbitrary", "arbitrary"),
                                             vmem_limit_bytes=VMEM_LIMIT),
        name="deltanet",
    )(qkv, z, ba, bat, alog_r, dtb_r, alog_c, dtb_c, dnw, *masks)


def _mix_kernel(x_ref, on_ref, u_ref, gates_ref, wupa_ref, poolw_ref, pscale_ref, wupb_ref, wout_ref,
                gffn_ref, wr_hi_ref, wr_lo_ref, br_ref,
                x1_ref, h2_ref, route_ref, counts_ref, ucarry_ref, ecount_ref):
    tm = x_ref.shape[0]
    j = pl.program_id(1)

    @pl.when(j == 0)
    def _():
        ucarry_ref[...] = jnp.zeros_like(ucarry_ref)

    @pl.when((j == 0) & (pl.program_id(0) == 0))
    def _():
        ecount_ref[...] = jnp.zeros_like(ecount_ref)

    n = tm // MIX_SPLIT
    lane = lax.broadcasted_iota(jnp.int32, (n, LANES), 1)
    ri = lax.broadcasted_iota(jnp.int32, (n, n), 0)
    rj = lax.broadcasted_iota(jnp.int32, (n, n), 1)
    before = (rj < ri).astype(BF16)
    picks = []
    for p in range(MIX_SPLIT):
        rows = pl.ds(p * n, n)

        u = u_ref[rows, :]
        halo = ucarry_ref[...] if p == 0 else u_ref[p * n - POOL_HALO:p * n, :]
        ext = jnp.concatenate([halo, u], axis=0)
        t_pos = j * tm + p * n + lax.broadcasted_iota(jnp.int32, (n, POOL_GROUP_DIM), 0)
        ys = []
        for g in range(POOL_GROUPS):
            s = ext[:, g * POOL_GROUP_DIM:(g + 1) * POOL_GROUP_DIM]
            shift = 1
            while shift < POOL_WINDOWS[g]:
                s = s + pltpu.roll(s, shift, 0)
                shift *= 2
            count = jnp.minimum(t_pos + 1, POOL_WINDOWS[g]).astype(F32)
            pooled = s[POOL_HALO:] / count - u[:, g * POOL_GROUP_DIM:(g + 1) * POOL_GROUP_DIM]
            ys.append(_dot(pooled.astype(BF16), poolw_ref[g]))
        yb = jnp.concatenate(ys, axis=-1) * pscale_ref[...]
        y_b = _dot(yb.astype(BF16), wupb_ref[...])
        y_a = _dot(on_ref[rows, :], wupa_ref[...])
        merged = (_sigmoid(gates_ref[rows, 0:D_MODEL]) * y_a
                  + _sigmoid(gates_ref[rows, D_MODEL:2 * D_MODEL]) * y_b)
        x1 = x_ref[rows, :] + _dot(merged.astype(BF16), wout_ref[...])
        x1_ref[rows, :] = x1

        h2 = _rms(x1, gffn_ref[...])
        _store_row_tiles(h2_ref.at[pl.ds(p * n * ROW_SUB, n * ROW_SUB)], h2)

        hi, lo = _split2(h2)
        logits = _dot(hi, wr_hi_ref[...]) + _dot(lo, wr_hi_ref[...]) + _dot(hi, wr_lo_ref[...]) + br_ref[...]
        lg = jnp.where(lane < N_EXPERTS, logits, -jnp.inf)
        vals, idxs, sels = [], [], []
        for _ in range(TOP_K):
            m = jnp.max(lg, axis=-1, keepdims=True)
            idx = jnp.min(jnp.where(lg == m, lane, LANES), axis=-1, keepdims=True)
            sel = lane == idx
            vals.append(m)
            idxs.append(idx)
            sels.append(sel)
            lg = jnp.where(sel, -jnp.inf, lg)
        es = [jnp.exp(vk - vals[0]) for vk in vals]
        denom = es[0] + es[1] + es[2] + es[3]
        picks.append((rows, sels, idxs, [e / denom for e in es]))

    for rows, sels, idxs, weights in picks:
        chosen = jnp.zeros((n, LANES), F32)
        for sel in sels:
            chosen = chosen + jnp.where(sel, 1.0, 0.0)
        pos = ecount_ref[...] + _dot(before, chosen.astype(BF16))
        ecount_ref[...] = ecount_ref[...] + jnp.sum(chosen, axis=0, keepdims=True)

        route = jnp.zeros((n, LANES), F32)
        for kk in range(TOP_K):
            rank = jnp.sum(jnp.where(sels[kk], pos, 0.0), axis=-1, keepdims=True)
            route = jnp.where(lane == kk, rank, route)
            route = jnp.where(lane == TOP_K + kk, idxs[kk].astype(F32), route)
            route = jnp.where(lane == 2 * TOP_K + kk, weights[kk], route)
        route_ref[rows, :] = route

    ucarry_ref[...] = u_ref[tm - POOL_HALO:tm, :]
    counts_ref[...] = ecount_ref[...]


def _mix(x2, on, u, gates, wupa, poolw, pscale, wupb, wout, gffn, wr_hi, wr_lo, br, batch, seq):
    tm = MIX_TM
    nb = seq // tm
    t = batch * seq
    full = lambda a: pl.BlockSpec(a.shape, lambda b, j: (0,) * a.ndim)
    row = lambda n: pl.BlockSpec((tm, n), lambda b, j: (b * nb + j, 0))
    return pl.pallas_call(
        _mix_kernel,
        grid=(batch, nb),
        in_specs=[row(D_MODEL), row(DN_WIDTH), row(POOL_WIDTH), row(2 * D_MODEL),
                  full(wupa), full(poolw), full(pscale), full(wupb), full(wout), full(gffn),
                  full(wr_hi), full(wr_lo), full(br)],
        out_specs=[row(D_MODEL), pl.BlockSpec((tm * ROW_SUB, LANES), lambda b, j: (b * nb + j, 0)), row(LANES),
                   pl.BlockSpec((1, LANES), lambda b, j: (0, 0))],
        out_shape=[jax.ShapeDtypeStruct((t, D_MODEL), F32),
                   jax.ShapeDtypeStruct((t * ROW_SUB, LANES), F32),
                   jax.ShapeDtypeStruct((t, LANES), F32),
                   jax.ShapeDtypeStruct((1, LANES), F32)],
        scratch_shapes=[pltpu.VMEM((POOL_HALO, POOL_WIDTH), F32), pltpu.VMEM((1, LANES), F32)],
        compiler_params=pltpu.CompilerParams(dimension_semantics=("arbitrary", "arbitrary"),
                                             vmem_limit_bytes=VMEM_LIMIT),
        name="mix",
    )(x2, on, u, gates, wupa, poolw, pscale, wupb, wout, gffn, wr_hi, wr_lo, br)


def _routing_tables(route, counts, tg, n_tiles):
    cnt = counts[0, 0:N_EXPERTS].astype(jnp.int32)
    padded = (cnt + (tg - 1)) // tg * tg
    ends = jnp.cumsum(padded)
    offs = ends - padded
    rank = route[:, 0:TOP_K].astype(jnp.int32)
    eidx = route[:, TOP_K:2 * TOP_K].astype(jnp.int32)
    experts = jnp.arange(N_EXPERTS, dtype=jnp.int32)
    dest = jnp.sum(jnp.where(eidx[..., None] == experts, offs, 0), axis=-1) + rank
    n_used = ends[N_EXPERTS - 1] // tg
    tile_start = jnp.arange(n_tiles, dtype=jnp.int32) * tg
    tile_expert = jnp.sum((tile_start[:, None] >= ends[None, :]).astype(jnp.int32), axis=1)
    tile_expert = jnp.minimum(tile_expert, N_EXPERTS - 1)
    last = tile_expert[jnp.maximum(n_used - 1, 0)]
    tile_expert = jnp.where(jnp.arange(n_tiles) < n_used, tile_expert, last)
    pad_start = offs + cnt
    pad_len = padded - cnt
    later = (experts[None, :] > experts[:, None]) & (cnt[None, :] > 0)
    next_expert = jnp.min(jnp.where(later, experts[None, :], N_EXPERTS), axis=1)
    next_expert = jnp.where(next_expert == N_EXPERTS, experts, next_expert)
    return (dest, tile_expert.astype(jnp.int32), next_expert.astype(jnp.int32),
            n_used.reshape(1).astype(jnp.int32), pad_start.astype(jnp.int32), pad_len.astype(jnp.int32))


def _dispatch_kernel(pad_start_ref, pad_len_ref, nu_ref, dest_ref, h2_ref, xs_ref, zero_ref, sem, zsem):
    tm = h2_ref.shape[0] // ROW_SUB
    tg = zero_ref.shape[0]
    n_tiles = xs_ref.shape[0] // tg

    @pl.when(pl.program_id(0) == 0)
    def _():
        zero_ref[...] = jnp.zeros_like(zero_ref)

        def pad_copy(d):
            return pltpu.make_async_copy(_row_tile(zero_ref, 0), _row_tile(xs_ref, d), zsem)

        def tile_copy(i):
            return pltpu.make_async_copy(zero_ref, xs_ref.at[pl.ds(pl.multiple_of(i * tg, tg), tg)], zsem)

        def per_expert(e, c):
            lax.fori_loop(0, pad_len_ref[e], lambda r, cc: (pad_copy(pad_start_ref[e] + r).start(), cc)[1], 0)
            lax.fori_loop(0, pad_len_ref[e], lambda r, cc: (pad_copy(0).wait(), cc)[1], 0)
            return c

        lax.fori_loop(0, N_EXPERTS, per_expert, 0)
        lax.fori_loop(nu_ref[0], n_tiles, lambda i, cc: (tile_copy(i).start(), cc)[1], 0)
        lax.fori_loop(nu_ref[0], n_tiles, lambda i, cc: (tile_copy(0).wait(), cc)[1], 0)

    def row_copy(r, d):
        return pltpu.make_async_copy(_row_tile(h2_ref, r), _row_tile(xs_ref, d), sem)

    def start(r, c):
        for kk in range(TOP_K):
            row_copy(r, dest_ref[0, 0, r * TOP_K + kk]).start(priority=kk % 2)
        return c

    lax.fori_loop(0, tm, start, 0, unroll=DMA_UNROLL)
    for kk in range(TOP_K):
        pltpu.make_async_copy(h2_ref, xs_ref.at[pl.ds(0, tm * ROW_SUB)], sem).wait()


def _dispatch(pad_start, pad_len, n_used, dest3, h2, n_rows):
    t = h2.shape[0] // ROW_SUB
    tm = DSP_TM
    return pl.pallas_call(
        _dispatch_kernel,
        grid_spec=pltpu.PrefetchScalarGridSpec(
            num_scalar_prefetch=3,
            grid=(t // tm,),
            in_specs=[pl.BlockSpec((1, 1, tm * TOP_K), lambda i, ps, pn, nu: (i, 0, 0), memory_space=pltpu.SMEM),
                      pl.BlockSpec((tm * ROW_SUB, LANES), lambda i, ps, pn, nu: (i, 0))],
            out_specs=pl.BlockSpec(memory_space=pl.ANY),
            scratch_shapes=[pltpu.VMEM((MOE_TG * ROW_SUB, LANES), F32), pltpu.SemaphoreType.DMA,
                            pltpu.SemaphoreType.DMA]),
        out_shape=jax.ShapeDtypeStruct((n_rows * ROW_SUB, LANES), F32),
        compiler_params=pltpu.CompilerParams(dimension_semantics=("arbitrary",), vmem_limit_bytes=VMEM_LIMIT),
        name="dispatch",
    )(pad_start, pad_len, n_used, dest3, h2)


def _experts_kernel(te_ref, nx_ref, nu_ref, xs_ref, wgu_hbm_ref, bgu_ref, wd_hbm_ref, bd_ref, ys_ref,
                    wgu_f32_ref, wd_f32_ref, wgu_bf_ref, wd_bf_ref, sem):
    i = pl.program_id(0)
    e = te_ref[i]

    def weight_copies(ex):
        return (pltpu.make_async_copy(wgu_hbm_ref.at[ex], wgu_f32_ref, sem.at[0]),
                pltpu.make_async_copy(wd_hbm_ref.at[ex], wd_f32_ref, sem.at[1]))

    @pl.when(i == 0)
    def _():
        for cp in weight_copies(e):
            cp.start()

    @pl.when((i == 0) | (e != te_ref[jnp.maximum(i - 1, 0)]))
    def _():
        for cp in weight_copies(e):
            cp.wait()
        wgu_bf_ref[...] = wgu_f32_ref[...].astype(BF16)
        wd_bf_ref[...] = wd_f32_ref[...].astype(BF16)

        @pl.when(nx_ref[e] != e)
        def _():
            for cp in weight_copies(nx_ref[e]):
                cp.start()

    @pl.when(i < nu_ref[0])
    def _():
        gu = _dot(_load_row_tiles(xs_ref).astype(BF16), wgu_bf_ref[...]) + bgu_ref[...]
        gate = jnp.minimum(gu[:, 0:D_FF], SWIGLU_LIMIT)
        up = jnp.clip(gu[:, D_FF:2 * D_FF], -SWIGLU_LIMIT, SWIGLU_LIMIT)
        act = gate * _sigmoid(SWIGLU_ALPHA * gate) * (up + 1.0)
        _store_row_tiles(ys_ref, _dot(act.astype(BF16), wd_bf_ref[...]) + bd_ref[...])

    @pl.when(i >= nu_ref[0])
    def _():
        ys_ref[...] = jnp.zeros_like(ys_ref)


def _experts(tile_expert, next_expert, n_used, xs, wgu, bgu, wd, bd):
    tg = MOE_TG * ROW_SUB
    n_tiles = xs.shape[0] // tg
    tile = lambda i, te, nx, nu: (jnp.minimum(i, nu[0] - 1), 0)
    expert = lambda i, te, nx, nu: (te[i], 0, 0)
    return pl.pallas_call(
        _experts_kernel,
        grid_spec=pltpu.PrefetchScalarGridSpec(
            num_scalar_prefetch=3,
            grid=(n_tiles,),
            in_specs=[pl.BlockSpec((tg, LANES), tile),
                      pl.BlockSpec(memory_space=pl.ANY),
                      pl.BlockSpec((None, 1, 2 * D_FF), expert),
                      pl.BlockSpec(memory_space=pl.ANY),
                      pl.BlockSpec((None, 1, D_MODEL), expert)],
            out_specs=pl.BlockSpec((tg, LANES), lambda i, te, nx, nu: (i, 0)),
            scratch_shapes=[pltpu.VMEM((D_MODEL, 2 * D_FF), F32), pltpu.VMEM((D_FF, D_MODEL), F32),
                            pltpu.VMEM((D_MODEL, 2 * D_FF), BF16), pltpu.VMEM((D_FF, D_MODEL), BF16),
                            pltpu.SemaphoreType.DMA((2,))]),
        out_shape=jax.ShapeDtypeStruct(xs.shape, F32),
        compiler_params=pltpu.CompilerParams(dimension_semantics=("arbitrary",), vmem_limit_bytes=VMEM_LIMIT),
        name="experts",
    )(tile_expert, next_expert, n_used, xs, wgu, bgu, wd, bd)


def _combine_kernel(dest_ref, dest_next_ref, route_ref, x1_ref, gfin_ref, ys_ref, out_ref, ybuf_ref, sem):
    i = pl.program_id(0)
    tm = x1_ref.shape[0]

    def gather_rows(dref, buf):
        def start(r, c):
            for kk in range(TOP_K):
                pltpu.make_async_copy(_row_tile(ys_ref, dref[0, 0, r * TOP_K + kk]),
                                      _row_tile(ybuf_ref.at[buf, kk], r), sem.at[buf]).start(priority=kk % 2)
            return c
        lax.fori_loop(0, tm, start, 0, unroll=DMA_UNROLL)

    @pl.when(i == 0)
    def _():
        gather_rows(dest_ref, 0)

    @pl.when(i + 1 < pl.num_programs(0))
    def _():
        gather_rows(dest_next_ref, (i + 1) % 2)

    buf = i % 2
    for kk in range(TOP_K):
        pltpu.make_async_copy(ys_ref.at[pl.ds(0, tm * ROW_SUB)], ybuf_ref.at[buf, kk], sem.at[buf]).wait()
    route = route_ref[...]
    acc = x1_ref[...]
    for kk in range(TOP_K):
        acc = acc + route[:, 2 * TOP_K + kk:2 * TOP_K + kk + 1] * _load_row_tiles(ybuf_ref.at[buf, kk])
    out_ref[...] = _rms(acc, gfin_ref[...])


def _combine(dest3, route, x1, gfin, ys):
    t = x1.shape[0]
    tm = CMB_TM
    last = t // tm - 1
    return pl.pallas_call(
        _combine_kernel,
        grid=(t // tm,),
        in_specs=[pl.BlockSpec((1, 1, tm * TOP_K), lambda i: (i, 0, 0), memory_space=pltpu.SMEM),
                  pl.BlockSpec((1, 1, tm * TOP_K), lambda i: (jnp.minimum(i + 1, last), 0, 0),
                               memory_space=pltpu.SMEM),
                  pl.BlockSpec((tm, LANES), lambda i: (i, 0)),
                  pl.BlockSpec((tm, D_MODEL), lambda i: (i, 0)),
                  pl.BlockSpec((1, D_MODEL), lambda i: (0, 0)),
                  pl.BlockSpec(memory_space=pl.ANY)],
        out_specs=pl.BlockSpec((tm, D_MODEL), lambda i: (i, 0)),
        out_shape=jax.ShapeDtypeStruct((t, D_MODEL), F32),
        scratch_shapes=[pltpu.VMEM((2, TOP_K, tm * ROW_SUB, LANES), F32), pltpu.SemaphoreType.DMA((2,))],
        compiler_params=pltpu.CompilerParams(dimension_semantics=("arbitrary",), vmem_limit_bytes=VMEM_LIMIT),
        name="combine",
    )(dest3, dest3, route, x1, gfin, ys)


def kernel(x, g_mix, w_in, conv_w, a_log, dt_bias, dn_norm, w_up_a, pool_w, pool_scale, w_up_b, w_out, g_ffn,
           w_router, b_router, w_gate_up, b_gate_up, w_down, b_down, g_final):
    batch, seq, d = x.shape
    assert d == D_MODEL and seq % DN_LB == 0 and seq % MIX_TM == 0
    assert (batch * seq) % DSP_TM == 0 and (batch * seq * TOP_K) % MOE_TG == 0
    assert g_mix.shape[0] == 1, "one layer"
    t = batch * seq
    x2 = x.reshape(t, d)

    w = w_in[0]
    o_z = 3 * DN_WIDTH
    o_b = o_z + DN_WIDTH
    o_a = o_b + DN_HEADS
    o_u = o_a + DN_HEADS
    o_g = o_u + POOL_WIDTH
    wqkv = w[:, 0:o_z].astype(BF16)
    wz = w[:, o_z:o_b].astype(BF16)
    wu = w[:, o_u:o_g].astype(BF16)
    wg = w[:, o_g:].astype(BF16)
    w_b = w[:, o_b:o_a]
    w_a = w[:, o_a:o_u]
    pad = jnp.zeros((d, LANES - DN_HEADS), F32)
    wba = jnp.concatenate([w_b, pad, w_a, pad], axis=1).astype(BF16)
    wbat = jnp.concatenate([w_b, w_a], axis=1).T.astype(BF16)

    qkv, z, u, gates, ba, bat = _in_proj(x2, g_mix, wqkv, wz, wu, wg, wba, wbat, conv_w[0], seq)

    lane_pad = lambda p: jnp.pad(p.reshape(1, DN_HEADS), ((0, 0), (0, LANES - DN_HEADS)))
    on = _deltanet(qkv, z, ba, bat, lane_pad(a_log[0]), lane_pad(dt_bias[0]),
                   a_log[0].reshape(DN_HEADS, 1), dt_bias[0].reshape(DN_HEADS, 1),
                   jnp.tile(dn_norm[0], DN_HEADS).reshape(1, DN_WIDTH), batch, seq)

    wr = jnp.pad(w_router[0], ((0, 0), (0, LANES - N_EXPERTS)))
    wr_hi = wr.astype(BF16)
    wr_lo = (wr - wr_hi.astype(F32)).astype(BF16)
    br = jnp.pad(b_router[0].reshape(1, N_EXPERTS), ((0, 0), (0, LANES - N_EXPERTS)))
    x1, h2, route, counts = _mix(x2, on, u, gates, w_up_a[0].astype(BF16), pool_w[0].astype(BF16),
                                  pool_scale[0].reshape(1, POOL_WIDTH), w_up_b[0].astype(BF16),
                                  w_out[0].astype(BF16), g_ffn, wr_hi, wr_lo, br, batch, seq)

    n_tiles = t * TOP_K // MOE_TG + N_EXPERTS
    dest, tile_expert, next_expert, n_used, pad_start, pad_len = _routing_tables(route, counts, MOE_TG, n_tiles)
    xs = _dispatch(pad_start, pad_len, n_used, dest.reshape(t // DSP_TM, 1, DSP_TM * TOP_K), h2, n_tiles * MOE_TG)
    ys = _experts(tile_expert, next_expert, n_used, xs, w_gate_up[0],
                  b_gate_up[0].reshape(N_EXPERTS, 1, 2 * D_FF), w_down[0], b_down[0].reshape(N_EXPERTS, 1, D_MODEL))
    out = _combine(dest.reshape(t // CMB_TM, 1, CMB_TM * TOP_K), route, x1, g_final.reshape(1, D_MODEL), ys)
    return out.reshape(batch, seq, d)
```

```python
import functools

import jax
import jax.numpy as jnp
from jax import lax
from jax.experimental import pallas as pl
from jax.experimental.pallas import tpu as pltpu

F32 = jnp.float32
BF16 = jnp.bfloat16

D_MODEL = 1024
CHUNK = 64
DN_HEADS = 8
DN_HEAD_DIM = 64
DN_WIDTH = DN_HEADS * DN_HEAD_DIM
CONV_WIDTH = 4
POOL_GROUPS = 4
POOL_WINDOWS = (2, 4, 8, 16)
POOL_WIDTH = 512
POOL_GROUP_DIM = 128
POOL_HALO = 16
N_EXPERTS = 32
TOP_K = 4
D_FF = D_MODEL
SWIGLU_LIMIT = 7.0
SWIGLU_ALPHA = 1.702
NORM_EPS = 1e-6
LANES = 128
SUBLANES = 8
VMEM_LIMIT = 56 * 1024 * 1024

IN_TM = 256
DN_LB = 256
INTRA_GROUP = 2
MIX_TM = 512
MIX_SPLIT = 2
MOE_TG = 512
DSP_TM = 1024
CMB_TM = 512
DMA_UNROLL = 8


def _dot(a, b):
    return jnp.dot(a, b, preferred_element_type=F32)


def _dot_nt(a, b):
    return lax.dot_general(a, b, (((1,), (1,)), ((), ())), preferred_element_type=F32)


def _dot_tn(a, b):
    return lax.dot_general(a, b, (((0,), (0,)), ((), ())), preferred_element_type=F32)


def _split2(x):
    hi = x.astype(BF16)
    lo = (x - hi.astype(F32)).astype(BF16)
    return hi, lo


def _dot_exact_rhs(x, m):
    hi, lo = _split2(x)
    return _dot(hi, m) + _dot(lo, m)


def _dot_exact_lhs(m, x):
    hi, lo = _split2(x)
    return _dot(m, hi) + _dot(m, lo)


def _softplus(x):
    return jnp.maximum(x, 0.0) + jnp.log1p(jnp.exp(-jnp.abs(x)))


def _sigmoid(x):
    return 1.0 / (1.0 + jnp.exp(-x))


def _rms(x, g):
    return x * lax.rsqrt(jnp.mean(x * x, axis=-1, keepdims=True) + NORM_EPS) * g


ROW_SUB = D_MODEL // LANES


def _row_tile(ref, r):
    return ref.at[pl.ds(pl.multiple_of(r * ROW_SUB, ROW_SUB), ROW_SUB)]


def _store_row_tiles(ref, x):
    for j in range(ROW_SUB):
        ref[pl.ds(j, x.shape[0], stride=ROW_SUB), :] = x[:, j * LANES:(j + 1) * LANES]


def _load_row_tiles(ref):
    n = ref.shape[0] // ROW_SUB
    return jnp.concatenate([ref[pl.ds(j, n, stride=ROW_SUB), :] for j in range(ROW_SUB)], axis=1)


def _in_proj_kernel(x_ref, g_ref, wqkv_ref, wz_ref, wu_ref, wg_ref, wba_ref, wbat_ref, convw_ref,
                    qkv_ref, z_ref, u_ref, gates_ref, ba_ref, bat_ref, hist_ref, *, steps_per_seq):
    tm = x_ref.shape[0]
    hb = _rms(x_ref[...], g_ref[...]).astype(BF16)

    @pl.when(pl.program_id(0) % steps_per_seq == 0)
    def _():
        hist_ref[0:SUBLANES, :] = jnp.zeros((SUBLANES, hist_ref.shape[1]), F32)

    blk = _dot(hb, wqkv_ref[...])
    cw = convw_ref[...]
    hist_ref[SUBLANES:SUBLANES + tm, :] = blk
    act = blk * cw[CONV_WIDTH - 1:CONV_WIDTH]
    for s in range(1, CONV_WIDTH):
        act = act + hist_ref[pl.ds(SUBLANES - s, tm), :] * cw[CONV_WIDTH - 1 - s:CONV_WIDTH - s]
    hist_ref[0:SUBLANES, :] = blk[tm - SUBLANES:tm]
    qkv_ref[...] = act * _sigmoid(act)

    z_ref[...] = _dot(hb, wz_ref[...])
    u_ref[...] = _dot(hb, wu_ref[...])
    gates_ref[...] = _dot(hb, wg_ref[...])
    ba_ref[...] = _dot(hb, wba_ref[...])
    bat_ref[...] = _dot_nt(wbat_ref[...], hb)


def _in_proj(x2, g_mix, wqkv, wz, wu, wg, wba, wbat, conv_w, seq):
    t = x2.shape[0]
    tm = IN_TM
    full = lambda a: pl.BlockSpec(a.shape, lambda i: (0, 0))
    row = lambda n: pl.BlockSpec((tm, n), lambda i: (i, 0))
    return pl.pallas_call(
        functools.partial(_in_proj_kernel, steps_per_seq=seq // tm),
        grid=(t // tm,),
        in_specs=[row(D_MODEL), full(g_mix), full(wqkv), full(wz), full(wu), full(wg), full(wba), full(wbat),
                  full(conv_w)],
        out_specs=[row(3 * DN_WIDTH), row(DN_WIDTH), row(POOL_WIDTH), row(2 * D_MODEL), row(2 * LANES),
                   pl.BlockSpec((2 * SUBLANES, tm), lambda i: (0, i))],
        out_shape=[jax.ShapeDtypeStruct((t, 3 * DN_WIDTH), F32),
                   jax.ShapeDtypeStruct((t, DN_WIDTH), F32),
                   jax.ShapeDtypeStruct((t, POOL_WIDTH), F32),
                   jax.ShapeDtypeStruct((t, 2 * D_MODEL), F32),
                   jax.ShapeDtypeStruct((t, 2 * LANES), F32),
                   jax.ShapeDtypeStruct((2 * SUBLANES, t), F32)],
        scratch_shapes=[pltpu.VMEM((SUBLANES + tm, 3 * DN_WIDTH), F32)],
        compiler_params=pltpu.CompilerParams(dimension_semantics=("arbitrary",), vmem_limit_bytes=VMEM_LIMIT),
        name="in_proj",
    )(x2, g_mix, wqkv, wz, wu, wg, wba, wbat, conv_w)


def _deltanet_kernel(qkv_ref, z_ref, ba_ref, bat_ref, alog_r_ref, dtb_r_ref, alog_c_ref, dtb_c_ref, dnw_ref,
                     head_ones_ref, tril_ref, triu_ref, chunk_ones_ref, expand_ref, o_ref,
                     s_ref, qn_ref, kn_ref, kbe_ref, vb_ref, qg_ref, kd_ref,
                     xbeta_ref, xgc_ref, xgl_ref, gr_ref, oacc_ref, qkd_ref, wq_ref, u_ref, kdt_ref, gl_ref):
    lb = qkv_ref.shape[0]
    n_chunks = lb // CHUNK

    @pl.when(pl.program_id(1) == 0)
    def _():
        s_ref[...] = jnp.zeros_like(s_ref)

    q = qkv_ref[:, 0:DN_WIDTH]
    k = qkv_ref[:, DN_WIDTH:2 * DN_WIDTH]
    v = qkv_ref[:, 2 * DN_WIDTH:3 * DN_WIDTH]

    head_ones = head_ones_ref[...]
    qn = q * lax.rsqrt(_dot_exact_rhs(q * q, head_ones) + NORM_EPS) * (DN_HEAD_DIM ** -0.5)
    kn = k * lax.rsqrt(_dot_exact_rhs(k * k, head_ones) + NORM_EPS)

    ba = ba_ref[...]
    beta_c = _sigmoid(ba[:, 0:LANES])
    g_c = -jnp.exp(alog_r_ref[...]) * _softplus(ba[:, LANES:2 * LANES] + dtb_r_ref[...])
    lane = lax.broadcasted_iota(jnp.int32, (lb, LANES), 1)
    g_c = jnp.where(lane < DN_HEADS, g_c, 0.0)
    bat = bat_ref[...]
    g_r = -jnp.exp(alog_c_ref[...]) * _softplus(bat[SUBLANES:2 * SUBLANES] + dtb_c_ref[...])

    gc = _dot_exact_lhs(tril_ref[...], g_c)
    gtot = _dot_exact_lhs(chunk_ones_ref[...], g_c)
    gr = _dot_exact_rhs(g_r, triu_ref[...])
    for c in range(n_chunks):
        gr_ref[c] = gr[:, c * CHUNK:(c + 1) * CHUNK]

    expand = expand_ref[...]
    x_beta = _dot_exact_rhs(beta_c, expand)
    x_gc = _dot_exact_rhs(gc, expand)
    x_gtot = _dot_exact_rhs(gtot, expand)
    x_eg = jnp.exp(x_gc)
    xbeta_ref[...] = x_beta
    xgc_ref[...] = x_gc
    xgl_ref[...] = jnp.exp(x_gtot)
    qn_ref[...] = qn.astype(BF16)
    kn_ref[...] = kn.astype(BF16)
    kbe_ref[...] = (kn * (x_beta * x_eg)).astype(BF16)
    vb_ref[...] = (v * x_beta).astype(BF16)
    qg_ref[...] = (qn * x_eg).astype(BF16)
    kd_ref[...] = (kn * jnp.exp(x_gtot - x_gc)).astype(BF16)

    ci = lax.broadcasted_iota(jnp.int32, (CHUNK, CHUNK), 0)
    cj = lax.broadcasted_iota(jnp.int32, (CHUNK, CHUNK), 1)
    eye = (ci == cj).astype(F32)

    heads = range(DN_HEADS)
    lanes = [pl.ds(h * DN_HEAD_DIM, DN_HEAD_DIM) for h in heads]
    stack = lambda top, bot: jnp.concatenate([top, bot], axis=0)

    chunk_rows = lambda c: pl.ds(c * CHUNK, CHUNK)

    def within_chunks(chunk_ids):
        chains = [(c, h) for c in chunk_ids for h in heads]
        ids = range(len(chains))
        rows = [chunk_rows(c) for c, _ in chains]
        ln = [lanes[h] for _, h in chains]
        kb = [kn_ref[rows[i], ln[i]] for i in ids]
        kq = [_dot_nt(stack(kb[i], qn_ref[rows[i], ln[i]]), kb[i]) for i in ids]
        yield
        decay = [jnp.exp(jnp.where(ci >= cj, xgc_ref[rows[i], ln[i]] - gr_ref[c, h:h + 1, :], -jnp.inf))
                 for i, (c, h) in enumerate(chains)]
        a = [jnp.where(ci > cj, xbeta_ref[rows[i], ln[i]] * kq[i][0:CHUNK] * decay[i], 0.0) for i in ids]
        slot = [c * DN_HEADS + h for c, h in chains]
        for i in ids:
            qkd_ref[slot[i]] = (kq[i][CHUNK:2 * CHUNK] * decay[i]).astype(BF16)
            wq_ref[slot[i], CHUNK:2 * CHUNK, :] = qg_ref[rows[i], ln[i]]
            kdt_ref[slot[i]] = kd_ref[rows[i], ln[i]].T.astype(BF16)
            gl_ref[slot[i]] = xgl_ref[rows[i], ln[i]]
        t_inv = [eye - a[i] for i in ids]
        pw = [a[i].astype(BF16) for i in ids]
        pw = [_dot(pw[i], pw[i]).astype(BF16) for i in ids]
        yield
        for _ in range(4):
            r = [_dot(stack(t_inv[i].astype(BF16), pw[i]), pw[i]) for i in ids]
            yield
            t_inv = [t_inv[i] + r[i][0:CHUNK] for i in ids]
            pw = [r[i][CHUNK:2 * CHUNK].astype(BF16) for i in ids]
        t_inv = [t_inv[i] + _dot(t_inv[i].astype(BF16), pw[i]) for i in ids]
        yield
        tb = [t_inv[i].astype(BF16) for i in ids]
        for i in ids:
            wq_ref[slot[i], 0:CHUNK, :] = _dot(tb[i], kbe_ref[rows[i], ln[i]]).astype(BF16)
        yield
        for i in ids:
            u_ref[slot[i]] = _dot(tb[i], vb_ref[rows[i], ln[i]])
        yield

    def state_chunks(chunk_ids):
        for c in chunk_ids:
            rows = chunk_rows(c)
            slot = [c * DN_HEADS + h for h in heads]
            s = [s_ref[h] for h in heads]
            ws = [_dot(wq_ref[slot[h]], s[h].astype(BF16)) for h in heads]
            yield
            vnb = [(u_ref[slot[h]] - ws[h][0:CHUNK]).astype(BF16) for h in heads]
            for h in heads:
                oacc_ref[h, rows, :] = ws[h][CHUNK:2 * CHUNK] + _dot(qkd_ref[slot[h]], vnb[h])
            yield
            for h in heads:
                s_ref[h] = s[h] * gl_ref[slot[h]] + _dot(kdt_ref[slot[h]], vnb[h])
            yield

    groups = [list(range(g, min(g + INTRA_GROUP, n_chunks))) for g in range(0, n_chunks, INTRA_GROUP)]
    pending = iter(())
    for grp in groups:
        for _ in within_chunks(grp):
            next(pending, None)
        for _ in pending:
            pass
        pending = state_chunks(grp)
    for _ in pending:
        pass

    o = jnp.concatenate([oacc_ref[h] for h in heads], axis=1)
    ms = _dot_exact_rhs(o * o, head_ones) * (1.0 / DN_HEAD_DIM)
    z = z_ref[...]
    o_ref[...] = (o * lax.rsqrt(ms + NORM_EPS) * dnw_ref[...] * (z * _sigmoid(z))).astype(o_ref.dtype)


def _deltanet_masks(lb):
    head = jnp.arange(DN_WIDTH) // DN_HEAD_DIM
    head_ones = head[:, None] == head[None, :]
    pos = jnp.arange(lb)
    same = (pos[:, None] // CHUNK) == (pos[None, :] // CHUNK)
    tril = same & (pos[None, :] <= pos[:, None])
    triu = same & (pos[:, None] <= pos[None, :])
    expand = jnp.arange(LANES)[:, None] == head[None, :]
    return tuple(m.astype(BF16) for m in (head_ones, tril, triu, same, expand))


def _deltanet(qkv, z, ba, bat, alog_r, dtb_r, alog_c, dtb_c, dnw, batch, seq):
    lb = DN_LB
    nb = seq // lb
    slots = lb // CHUNK * DN_HEADS
    masks = _deltanet_masks(lb)
    full = lambda a: pl.BlockSpec(a.shape, lambda b, j: (0,) * a.ndim)
    row = lambda n: pl.BlockSpec((lb, n), lambda b, j: (b * nb + j, 0))
    return pl.pallas_call(
        _deltanet_kernel,
        grid=(batch, nb),
        in_specs=[row(3 * DN_WIDTH), row(DN_WIDTH), row(2 * LANES),
                  pl.BlockSpec((2 * SUBLANES, lb), lambda b, j: (0, b * nb + j)),
                  full(alog_r), full(dtb_r), full(alog_c), full(dtb_c), full(dnw)] + [full(m) for m in masks],
        out_specs=row(DN_WIDTH),
        out_shape=jax.ShapeDtypeStruct((batch * seq, DN_WIDTH), BF16),
        scratch_shapes=[
            pltpu.VMEM((DN_HEADS, DN_HEAD_DIM, DN_HEAD_DIM), F32),
            pltpu.VMEM((lb, DN_WIDTH), BF16),
            pltpu.VMEM((lb, DN_WIDTH), BF16),
            pltpu.VMEM((lb, DN_WIDTH), BF16),
            pltpu.VMEM((lb, DN_WIDTH), BF16),
            pltpu.VMEM((lb, DN_WIDTH), BF16),
            pltpu.VMEM((lb, DN_WIDTH), BF16),
            pltpu.VMEM((lb, DN_WIDTH), F32),
            pltpu.VMEM((lb, DN_WIDTH), F32),
            pltpu.VMEM((lb, DN_WIDTH), F32),
            pltpu.VMEM((lb // CHUNK, DN_HEADS, CHUNK), F32),
            pltpu.VMEM((DN_HEADS, lb, DN_HEAD_DIM), F32),
            pltpu.VMEM((slots, CHUNK, CHUNK), BF16),
            pltpu.VMEM((slots, 2 * CHUNK, DN_HEAD_DIM), BF16),
            pltpu.VMEM((slots, CHUNK, DN_HEAD_DIM), F32),
            pltpu.VMEM((slots, DN_HEAD_DIM, CHUNK), BF16),
            pltpu.VMEM((slots, DN_HEAD_DIM, DN_HEAD_DIM), F32),
        ],
        compiler_params=pltpu.CompilerParams(dimension_semantics=("arbitrary", "arbitrary"),
                                             vmem_limit_bytes=VMEM_LIMIT),
        name="deltanet",
    )(qkv, z, ba, bat, alog_r, dtb_r, alog_c, dtb_c, dnw, *masks)


def _mix_kernel(x_ref, on_ref, u_ref, gates_ref, wupa_ref, poolw_ref, pscale_ref, wupb_ref, wout_ref,
                gffn_ref, wr_hi_ref, wr_lo_ref, br_ref,
                x1_ref, h2_ref, route_ref, counts_ref, ucarry_ref, ecount_ref):
    tm = x_ref.shape[0]
    j = pl.program_id(1)

    @pl.when(j == 0)
    def _():
        ucarry_ref[...] = jnp.zeros_like(ucarry_ref)

    @pl.when((j == 0) & (pl.program_id(0) == 0))
    def _():
        ecount_ref[...] = jnp.zeros_like(ecount_ref)

    n = tm // MIX_SPLIT
    lane = lax.broadcasted_iota(jnp.int32, (n, LANES), 1)
    ri = lax.broadcasted_iota(jnp.int32, (n, n), 0)
    rj = lax.broadcasted_iota(jnp.int32, (n, n), 1)
    before = (rj < ri).astype(BF16)
    picks = []
    for p in range(MIX_SPLIT):
        rows = pl.ds(p * n, n)

        u = u_ref[rows, :]
        halo = ucarry_ref[...] if p == 0 else u_ref[p * n - POOL_HALO:p * n, :]
        ext = jnp.concatenate([halo, u], axis=0)
        t_pos = j * tm + p * n + lax.broadcasted_iota(jnp.int32, (n, POOL_GROUP_DIM), 0)
        ys = []
        for g in range(POOL_GROUPS):
            s = ext[:, g * POOL_GROUP_DIM:(g + 1) * POOL_GROUP_DIM]
            shift = 1
            while shift < POOL_WINDOWS[g]:
                s = s + pltpu.roll(s, shift, 0)
                shift *= 2
            count = jnp.minimum(t_pos + 1, POOL_WINDOWS[g]).astype(F32)
            pooled = s[POOL_HALO:] / count - u[:, g * POOL_GROUP_DIM:(g + 1) * POOL_GROUP_DIM]
            ys.append(_dot(pooled.astype(BF16), poolw_ref[g]))
        yb = jnp.concatenate(ys, axis=-1) * pscale_ref[...]
        y_b = _dot(yb.astype(BF16), wupb_ref[...])
        y_a = _dot(on_ref[rows, :], wupa_ref[...])
        merged = (_sigmoid(gates_ref[rows, 0:D_MODEL]) * y_a
                  + _sigmoid(gates_ref[rows, D_MODEL:2 * D_MODEL]) * y_b)
        x1 = x_ref[rows, :] + _dot(merged.astype(BF16), wout_ref[...])
        x1_ref[rows, :] = x1

        h2 = _rms(x1, gffn_ref[...])
        _store_row_tiles(h2_ref.at[pl.ds(p * n * ROW_SUB, n * ROW_SUB)], h2)

        hi, lo = _split2(h2)
        logits = _dot(hi, wr_hi_ref[...]) + _dot(lo, wr_hi_ref[...]) + _dot(hi, wr_lo_ref[...]) + br_ref[...]
        lg = jnp.where(lane < N_EXPERTS, logits, -jnp.inf)
        vals, idxs, sels = [], [], []
        for _ in range(TOP_K):
            m = jnp.max(lg, axis=-1, keepdims=True)
            idx = jnp.min(jnp.where(lg == m, lane, LANES), axis=-1, keepdims=True)
            sel = lane == idx
            vals.append(m)
            idxs.append(idx)
            sels.append(sel)
            lg = jnp.where(sel, -jnp.inf, lg)
        es = [jnp.exp(vk - vals[0]) for vk in vals]
        denom = es[0] + es[1] + es[2] + es[3]
        picks.append((rows, sels, idxs, [e / denom for e in es]))

    for rows, sels, idxs, weights in picks:
        chosen = jnp.zeros((n, LANES), F32)
        for sel in sels:
            chosen = chosen + jnp.where(sel, 1.0, 0.0)
        pos = ecount_ref[...] + _dot(before, chosen.astype(BF16))
        ecount_ref[...] = ecount_ref[...] + jnp.sum(chosen, axis=0, keepdims=True)

        route = jnp.zeros((n, LANES), F32)
        for kk in range(TOP_K):
            rank = jnp.sum(jnp.where(sels[kk], pos, 0.0), axis=-1, keepdims=True)
            route = jnp.where(lane == kk, rank, route)
            route = jnp.where(lane == TOP_K + kk, idxs[kk].astype(F32), route)
            route = jnp.where(lane == 2 * TOP_K + kk, weights[kk], route)
        route_ref[rows, :] = route

    ucarry_ref[...] = u_ref[tm - POOL_HALO:tm, :]
    counts_ref[...] = ecount_ref[...]


def _mix(x2, on, u, gates, wupa, poolw, pscale, wupb, wout, gffn, wr_hi, wr_lo, br, batch, seq):
    tm = MIX_TM
    nb = seq // tm
    t = batch * seq
    full = lambda a: pl.BlockSpec(a.shape, lambda b, j: (0,) * a.ndim)
    row = lambda n: pl.BlockSpec((tm, n), lambda b, j: (b * nb + j, 0))
    return pl.pallas_call(
        _mix_kernel,
        grid=(batch, nb),
        in_specs=[row(D_MODEL), row(DN_WIDTH), row(POOL_WIDTH), row(2 * D_MODEL),
                  full(wupa), full(poolw), full(pscale), full(wupb), full(wout), full(gffn),
                  full(wr_hi), full(wr_lo), full(br)],
        out_specs=[row(D_MODEL), pl.BlockSpec((tm * ROW_SUB, LANES), lambda b, j: (b * nb + j, 0)), row(LANES),
                   pl.BlockSpec((1, LANES), lambda b, j: (0, 0))],
        out_shape=[jax.ShapeDtypeStruct((t, D_MODEL), F32),
                   jax.ShapeDtypeStruct((t * ROW_SUB, LANES), F32),
                   jax.ShapeDtypeStruct((t, LANES), F32),
                   jax.ShapeDtypeStruct((1, LANES), F32)],
        scratch_shapes=[pltpu.VMEM((POOL_HALO, POOL_WIDTH), F32), pltpu.VMEM((1, LANES), F32)],
        compiler_params=pltpu.CompilerParams(dimension_semantics=("arbitrary", "arbitrary"),
                                             vmem_limit_bytes=VMEM_LIMIT),
        name="mix",
    )(x2, on, u, gates, wupa, poolw, pscale, wupb, wout, gffn, wr_hi, wr_lo, br)


def _routing_tables(route, counts, tg, n_tiles):
    cnt = counts[0, 0:N_EXPERTS].astype(jnp.int32)
    padded = (cnt + (tg - 1)) // tg * tg
    ends = jnp.cumsum(padded)
    offs = ends - padded
    rank = route[:, 0:TOP_K].astype(jnp.int32)
    eidx = route[:, TOP_K:2 * TOP_K].astype(jnp.int32)
    experts = jnp.arange(N_EXPERTS, dtype=jnp.int32)
    dest = jnp.sum(jnp.where(eidx[..., None] == experts, offs, 0), axis=-1) + rank
    n_used = ends[N_EXPERTS - 1] // tg
    tile_start = jnp.arange(n_tiles, dtype=jnp.int32) * tg
    tile_expert = jnp.sum((tile_start[:, None] >= ends[None, :]).astype(jnp.int32), axis=1)
    tile_expert = jnp.minimum(tile_expert, N_EXPERTS - 1)
    last = tile_expert[jnp.maximum(n_used - 1, 0)]
    tile_expert = jnp.where(jnp.arange(n_tiles) < n_used, tile_expert, last)
    pad_start = offs + cnt
    pad_len = padded - cnt
    later = (experts[None, :] > experts[:, None]) & (cnt[None, :] > 0)
    next_expert = jnp.min(jnp.where(later, experts[None, :], N_EXPERTS), axis=1)
    next_expert = jnp.where(next_expert == N_EXPERTS, experts, next_expert)
    return (dest, tile_expert.astype(jnp.int32), next_expert.astype(jnp.int32),
            n_used.reshape(1).astype(jnp.int32), pad_start.astype(jnp.int32), pad_len.astype(jnp.int32))


def _dispatch_kernel(pad_start_ref, pad_len_ref, nu_ref, dest_ref, h2_ref, xs_ref, zero_ref, sem, zsem):
    tm = h2_ref.shape[0] // ROW_SUB
    tg = zero_ref.shape[0]
    n_tiles = xs_ref.shape[0] // tg

    @pl.when(pl.program_id(0) == 0)
    def _():
        zero_ref[...] = jnp.zeros_like(zero_ref)

        def pad_copy(d):
            return pltpu.make_async_copy(_row_tile(zero_ref, 0), _row_tile(xs_ref, d), zsem)

        def tile_copy(i):
            return pltpu.make_async_copy(zero_ref, xs_ref.at[pl.ds(pl.multiple_of(i * tg, tg), tg)], zsem)

        def per_expert(e, c):
            lax.fori_loop(0, pad_len_ref[e], lambda r, cc: (pad_copy(pad_start_ref[e] + r).start(), cc)[1], 0)
            lax.fori_loop(0, pad_len_ref[e], lambda r, cc: (pad_copy(0).wait(), cc)[1], 0)
            return c

        lax.fori_loop(0, N_EXPERTS, per_expert, 0)
        lax.fori_loop(nu_ref[0], n_tiles, lambda i, cc: (tile_copy(i).start(), cc)[1], 0)
        lax.fori_loop(nu_ref[0], n_tiles, lambda i, cc: (tile_copy(0).wait(), cc)[1], 0)

    def row_copy(r, d):
        return pltpu.make_async_copy(_row_tile(h2_ref, r), _row_tile(xs_ref, d), sem)

    def start(r, c):
        for kk in range(TOP_K):
            row_copy(r, dest_ref[0, 0, r * TOP_K + kk]).start(priority=kk % 2)
        return c

    lax.fori_loop(0, tm, start, 0, unroll=DMA_UNROLL)
    for kk in range(TOP_K):
        pltpu.make_async_copy(h2_ref, xs_ref.at[pl.ds(0, tm * ROW_SUB)], sem).wait()


def _dispatch(pad_start, pad_len, n_used, dest3, h2, n_rows):
    t = h2.shape[0] // ROW_SUB
    tm = DSP_TM
    return pl.pallas_call(
        _dispatch_kernel,
        grid_spec=pltpu.PrefetchScalarGridSpec(
            num_scalar_prefetch=3,
            grid=(t // tm,),
            in_specs=[pl.BlockSpec((1, 1, tm * TOP_K), lambda i, ps, pn, nu: (i, 0, 0), memory_space=pltpu.SMEM),
                      pl.BlockSpec((tm * ROW_SUB, LANES), lambda i, ps, pn, nu: (i, 0))],
            out_specs=pl.BlockSpec(memory_space=pl.ANY),
            scratch_shapes=[pltpu.VMEM((MOE_TG * ROW_SUB, LANES), F32), pltpu.SemaphoreType.DMA,
                            pltpu.SemaphoreType.DMA]),
        out_shape=jax.ShapeDtypeStruct((n_rows * ROW_SUB, LANES), F32),
        compiler_params=pltpu.CompilerParams(dimension_semantics=("arbitrary",), vmem_limit_bytes=VMEM_LIMIT),
        name="dispatch",
    )(pad_start, pad_len, n_used, dest3, h2)


def _experts_kernel(te_ref, nx_ref, nu_ref, xs_ref, wgu_hbm_ref, bgu_ref, wd_hbm_ref, bd_ref, ys_ref,
                    wgu_f32_ref, wd_f32_ref, wgu_bf_ref, wd_bf_ref, sem):
    i = pl.program_id(0)
    e = te_ref[i]

    def weight_copies(ex):
        return (pltpu.make_async_copy(wgu_hbm_ref.at[ex], wgu_f32_ref, sem.at[0]),
                pltpu.make_async_copy(wd_hbm_ref.at[ex], wd_f32_ref, sem.at[1]))

    @pl.when(i == 0)
    def _():
        for cp in weight_copies(e):
            cp.start()

    @pl.when((i == 0) | (e != te_ref[jnp.maximum(i - 1, 0)]))
    def _():
        for cp in weight_copies(e):
            cp.wait()
        wgu_bf_ref[...] = wgu_f32_ref[...].astype(BF16)
        wd_bf_ref[...] = wd_f32_ref[...].astype(BF16)

        @pl.when(nx_ref[e] != e)
        def _():
            for cp in weight_copies(nx_ref[e]):
                cp.start()

    @pl.when(i < nu_ref[0])
    def _():
        gu = _dot(_load_row_tiles(xs_ref).astype(BF16), wgu_bf_ref[...]) + bgu_ref[...]
        gate = jnp.minimum(gu[:, 0:D_FF], SWIGLU_LIMIT)
        up = jnp.clip(gu[:, D_FF:2 * D_FF], -SWIGLU_LIMIT, SWIGLU_LIMIT)
        act = gate * _sigmoid(SWIGLU_ALPHA * gate) * (up + 1.0)
        _store_row_tiles(ys_ref, _dot(act.astype(BF16), wd_bf_ref[...]) + bd_ref[...])

    @pl.when(i >= nu_ref[0])
    def _():
        ys_ref[...] = jnp.zeros_like(ys_ref)


def _experts(tile_expert, next_expert, n_used, xs, wgu, bgu, wd, bd):
    tg = MOE_TG * ROW_SUB
    n_tiles = xs.shape[0] // tg
    tile = lambda i, te, nx, nu: (jnp.minimum(i, nu[0] - 1), 0)
    expert = lambda i, te, nx, nu: (te[i], 0, 0)
    return pl.pallas_call(
        _experts_kernel,
        grid_spec=pltpu.PrefetchScalarGridSpec(
            num_scalar_prefetch=3,
            grid=(n_tiles,),
            in_specs=[pl.BlockSpec((tg, LANES), tile),
                      pl.BlockSpec(memory_space=pl.ANY),
                      pl.BlockSpec((None, 1, 2 * D_FF), expert),
                      pl.BlockSpec(memory_space=pl.ANY),
                      pl.BlockSpec((None, 1, D_MODEL), expert)],
            out_specs=pl.BlockSpec((tg, LANES), lambda i, te, nx, nu: (i, 0)),
            scratch_shapes=[pltpu.VMEM((D_MODEL, 2 * D_FF), F32), pltpu.VMEM((D_FF, D_MODEL), F32),
                            pltpu.VMEM((D_MODEL, 2 * D_FF), BF16), pltpu.VMEM((D_FF, D_MODEL), BF16),
                            pltpu.SemaphoreType.DMA((2,))]),
        out_shape=jax.ShapeDtypeStruct(xs.shape, F32),
        compiler_params=pltpu.CompilerParams(dimension_semantics=("arbitrary",), vmem_limit_bytes=VMEM_LIMIT),
        name="experts",
    )(tile_expert, next_expert, n_used, xs, wgu, bgu, wd, bd)


def _combine_kernel(dest_ref, dest_next_ref, route_ref, x1_ref, gfin_ref, ys_ref, out_ref, ybuf_ref, sem):
    i = pl.program_id(0)
    tm = x1_ref.shape[0]

    def gather_rows(dref, buf):
        def start(r, c):
            for kk in range(TOP_K):
                pltpu.make_async_copy(_row_tile(ys_ref, dref[0, 0, r * TOP_K + kk]),
                                      _row_tile(ybuf_ref.at[buf, kk], r), sem.at[buf]).start(priority=kk % 2)
            return c
        lax.fori_loop(0, tm, start, 0, unroll=DMA_UNROLL)

    @pl.when(i == 0)
    def _():
        gather_rows(dest_ref, 0)

    @pl.when(i + 1 < pl.num_programs(0))
    def _():
        gather_rows(dest_next_ref, (i + 1) % 2)

    buf = i % 2
    for kk in range(TOP_K):
        pltpu.make_async_copy(ys_ref.at[pl.ds(0, tm * ROW_SUB)], ybuf_ref.at[buf, kk], sem.at[buf]).wait()
    route = route_ref[...]
    acc = x1_ref[...]
    for kk in range(TOP_K):
        acc = acc + route[:, 2 * TOP_K + kk:2 * TOP_K + kk + 1] * _load_row_tiles(ybuf_ref.at[buf, kk])
    out_ref[...] = _rms(acc, gfin_ref[...])


def _combine(dest3, route, x1, gfin, ys):
    t = x1.shape[0]
    tm = CMB_TM
    last = t // tm - 1
    return pl.pallas_call(
        _combine_kernel,
        grid=(t // tm,),
        in_specs=[pl.BlockSpec((1, 1, tm * TOP_K), lambda i: (i, 0, 0), memory_space=pltpu.SMEM),
                  pl.BlockSpec((1, 1, tm * TOP_K), lambda i: (jnp.minimum(i + 1, last), 0, 0),
                               memory_space=pltpu.SMEM),
                  pl.BlockSpec((tm, LANES), lambda i: (i, 0)),
                  pl.BlockSpec((tm, D_MODEL), lambda i: (i, 0)),
                  pl.BlockSpec((1, D_MODEL), lambda i: (0, 0)),
                  pl.BlockSpec(memory_space=pl.ANY)],
        out_specs=pl.BlockSpec((tm, D_MODEL), lambda i: (i, 0)),
        out_shape=jax.ShapeDtypeStruct((t, D_MODEL), F32),
        scratch_shapes=[pltpu.VMEM((2, TOP_K, tm * ROW_SUB, LANES), F32), pltpu.SemaphoreType.DMA((2,))],
        compiler_params=pltpu.CompilerParams(dimension_semantics=("arbitrary",), vmem_limit_bytes=VMEM_LIMIT),
        name="combine",
    )(dest3, dest3, route, x1, gfin, ys)


def kernel(x, g_mix, w_in, conv_w, a_log, dt_bias, dn_norm, w_up_a, pool_w, pool_scale, w_up_b, w_out, g_ffn,
           w_router, b_router, w_gate_up, b_gate_up, w_down, b_down, g_final):
    batch, seq, d = x.shape
    assert d == D_MODEL and seq % DN_LB == 0 and seq % MIX_TM == 0
    assert (batch * seq) % DSP_TM == 0 and (batch * seq * TOP_K) % MOE_TG == 0
    assert g_mix.shape[0] == 1, "one layer"
    t = batch * seq
    x2 = x.reshape(t, d)

    w = w_in[0]
    o_z = 3 * DN_WIDTH
    o_b = o_z + DN_WIDTH
    o_a = o_b + DN_HEADS
    o_u = o_a + DN_HEADS
    o_g = o_u + POOL_WIDTH
    wqkv = w[:, 0:o_z].astype(BF16)
    wz = w[:, o_z:o_b].astype(BF16)
    wu = w[:, o_u:o_g].astype(BF16)
    wg = w[:, o_g:].astype(BF16)
    w_b = w[:, o_b:o_a]
    w_a = w[:, o_a:o_u]
    pad = jnp.zeros((d, LANES - DN_HEADS), F32)
    wba = jnp.concatenate([w_b, pad, w_a, pad], axis=1).astype(BF16)
    wbat = jnp.concatenate([w_b, w_a], axis=1).T.astype(BF16)

    qkv, z, u, gates, ba, bat = _in_proj(x2, g_mix, wqkv, wz, wu, wg, wba, wbat, conv_w[0], seq)

    lane_pad = lambda p: jnp.pad(p.reshape(1, DN_HEADS), ((0, 0), (0, LANES - DN_HEADS)))
    on = _deltanet(qkv, z, ba, bat, lane_pad(a_log[0]), lane_pad(dt_bias[0]),
                   a_log[0].reshape(DN_HEADS, 1), dt_bias[0].reshape(DN_HEADS, 1),
                   jnp.tile(dn_norm[0], DN_HEADS).reshape(1, DN_WIDTH), batch, seq)

    wr = jnp.pad(w_router[0], ((0, 0), (0, LANES - N_EXPERTS)))
    wr_hi = wr.astype(BF16)
    wr_lo = (wr - wr_hi.astype(F32)).astype(BF16)
    br = jnp.pad(b_router[0].reshape(1, N_EXPERTS), ((0, 0), (0, LANES - N_EXPERTS)))
    x1, h2, route, counts = _mix(x2, on, u, gates, w_up_a[0].astype(BF16), pool_w[0].astype(BF16),
                                  pool_scale[0].reshape(1, POOL_WIDTH), w_up_b[0].astype(BF16),
                                  w_out[0].astype(BF16), g_ffn, wr_hi, wr_lo, br, batch, seq)

    n_tiles = t * TOP_K // MOE_TG + N_EXPERTS
    dest, tile_expert, next_expert, n_used, pad_start, pad_len = _routing_tables(route, counts, MOE_TG, n_tiles)
    xs = _dispatch(pad_start, pad_len, n_used, dest.reshape(t // DSP_TM, 1, DSP_TM * TOP_K), h2, n_tiles * MOE_TG)
    ys = _experts(tile_expert, next_expert, n_used, xs, w_gate_up[0],
                  b_gate_up[0].reshape(N_EXPERTS, 1, 2 * D_FF), w_down[0], b_down[0].reshape(N_EXPERTS, 1, D_MODEL))
    out = _combine(dest.reshape(t // CMB_TM, 1, CMB_TM * TOP_K), route, x1, g_final.reshape(1, D_MODEL), ys)
    return out.reshape(batch, seq, d)
```

```python
import functools

import jax
import jax.numpy as jnp
from jax import lax
from jax.experimental import pallas as pl
from jax.experimental.pallas import tpu as pltpu

F32 = jnp.float32
BF16 = jnp.bfloat16

D_MODEL = 1024
CHUNK = 64
DN_HEADS = 8
DN_HEAD_DIM = 64
DN_WIDTH = DN_HEADS * DN_HEAD_DIM
CONV_WIDTH = 4
POOL_GROUPS = 4
POOL_WINDOWS = (2, 4, 8, 16)
POOL_WIDTH = 512
POOL_GROUP_DIM = 128
POOL_HALO = 16
N_EXPERTS = 32
TOP_K = 4
D_FF = D_MODEL
SWIGLU_LIMIT = 7.0
SWIGLU_ALPHA = 1.702
NORM_EPS = 1e-6
LANES = 128
SUBLANES = 8
VMEM_LIMIT = 56 * 1024 * 1024

IN_TM = 256
DN_LB = 256
INTRA_GROUP = 2
MIX_TM = 256
MOE_TG = 512
DSP_TM = 1024
CMB_TM = 512
DMA_UNROLL = 8


def _dot(a, b):
    return jnp.dot(a, b, preferred_element_type=F32)


def _dot_nt(a, b):
    return lax.dot_general(a, b, (((1,), (1,)), ((), ())), preferred_element_type=F32)


def _dot_tn(a, b):
    return lax.dot_general(a, b, (((0,), (0,)), ((), ())), preferred_element_type=F32)


def _split2(x):
    hi = x.astype(BF16)
    lo = (x - hi.astype(F32)).astype(BF16)
    return hi, lo


def _dot_exact_rhs(x, m):
    hi, lo = _split2(x)
    return _dot(hi, m) + _dot(lo, m)


def _dot_exact_lhs(m, x):
    hi, lo = _split2(x)
    return _dot(m, hi) + _dot(m, lo)


def _softplus(x):
    return jnp.maximum(x, 0.0) + jnp.log1p(jnp.exp(-jnp.abs(x)))


def _sigmoid(x):
    return 1.0 / (1.0 + jnp.exp(-x))


def _rms(x, g):
    return x * lax.rsqrt(jnp.mean(x * x, axis=-1, keepdims=True) + NORM_EPS) * g


ROW_SUB = D_MODEL // LANES


def _row_tile(ref, r):
    return ref.at[pl.ds(pl.multiple_of(r * ROW_SUB, ROW_SUB), ROW_SUB)]


def _store_row_tiles(ref, x):
    for j in range(ROW_SUB):
        ref[pl.ds(j, x.shape[0], stride=ROW_SUB), :] = x[:, j * LANES:(j + 1) * LANES]


def _load_row_tiles(ref):
    n = ref.shape[0] // ROW_SUB
    return jnp.concatenate([ref[pl.ds(j, n, stride=ROW_SUB), :] for j in range(ROW_SUB)], axis=1)


def _in_proj_kernel(x_ref, g_ref, wqkv_ref, wz_ref, wu_ref, wg_ref, wba_ref, wbat_ref, convw_ref,
                    qkv_ref, z_ref, u_ref, gates_ref, ba_ref, bat_ref, hist_ref, *, steps_per_seq):
    i = pl.program_id(0)
    tm = x_ref.shape[0]

    @pl.when(i == 0)
    def _():
        hist_ref[...] = jnp.zeros_like(hist_ref)

    @pl.when((i + steps_per_seq - 1) % steps_per_seq == 0)
    def _():
        hist_ref[0:SUBLANES, :] = jnp.zeros((SUBLANES, hist_ref.shape[1]), F32)

    cw = convw_ref[...]
    prev = hist_ref[SUBLANES:SUBLANES + tm, :]
    act = prev * cw[CONV_WIDTH - 1:CONV_WIDTH]
    for s in range(1, CONV_WIDTH):
        act = act + hist_ref[pl.ds(SUBLANES - s, tm), :] * cw[CONV_WIDTH - 1 - s:CONV_WIDTH - s]
    qkv_ref[...] = act * _sigmoid(act)

    hb = _rms(x_ref[...], g_ref[...]).astype(BF16)
    hist_ref[0:SUBLANES, :] = prev[tm - SUBLANES:tm]
    hist_ref[SUBLANES:SUBLANES + tm, :] = _dot(hb, wqkv_ref[...])
    z_ref[...] = _dot(hb, wz_ref[...])
    u_ref[...] = _dot(hb, wu_ref[...])
    gates_ref[...] = _dot(hb, wg_ref[...])
    ba_ref[...] = _dot(hb, wba_ref[...])
    bat_ref[...] = _dot_nt(wbat_ref[...], hb)


def _in_proj(x2, g_mix, wqkv, wz, wu, wg, wba, wbat, conv_w, seq):
    t = x2.shape[0]
    tm = IN_TM
    n = t // tm
    full = lambda a: pl.BlockSpec(a.shape, lambda i: (0, 0))
    row = lambda c: pl.BlockSpec((tm, c), lambda i: (jnp.minimum(i, n - 1), 0))
    return pl.pallas_call(
        functools.partial(_in_proj_kernel, steps_per_seq=seq // tm),
        grid=(n + 1,),
        in_specs=[row(D_MODEL), full(g_mix), full(wqkv), full(wz), full(wu), full(wg), full(wba), full(wbat),
                  full(conv_w)],
        out_specs=[pl.BlockSpec((tm, 3 * DN_WIDTH), lambda i: (jnp.maximum(i - 1, 0), 0)),
                   row(DN_WIDTH), row(POOL_WIDTH), row(2 * D_MODEL), row(2 * LANES),
                   pl.BlockSpec((2 * SUBLANES, tm), lambda i: (0, jnp.minimum(i, n - 1)))],
        out_shape=[jax.ShapeDtypeStruct((t, 3 * DN_WIDTH), F32),
                   jax.ShapeDtypeStruct((t, DN_WIDTH), F32),
                   jax.ShapeDtypeStruct((t, POOL_WIDTH), F32),
                   jax.ShapeDtypeStruct((t, 2 * D_MODEL), F32),
                   jax.ShapeDtypeStruct((t, 2 * LANES), F32),
                   jax.ShapeDtypeStruct((2 * SUBLANES, t), F32)],
        scratch_shapes=[pltpu.VMEM((SUBLANES + tm, 3 * DN_WIDTH), F32)],
        compiler_params=pltpu.CompilerParams(dimension_semantics=("arbitrary",), vmem_limit_bytes=VMEM_LIMIT),
        name="in_proj",
    )(x2, g_mix, wqkv, wz, wu, wg, wba, wbat, conv_w)


def _deltanet_kernel(qkv_ref, z_ref, ba_ref, bat_ref, alog_r_ref, dtb_r_ref, alog_c_ref, dtb_c_ref, dnw_ref,
                     head_ones_ref, tril_ref, triu_ref, chunk_ones_ref, expand_ref, o_ref,
                     s_ref, qn_ref, kn_ref, kbe_ref, vb_ref, qg_ref, kd_ref,
                     xbeta_ref, xgc_ref, xgl_ref, gr_ref, oacc_ref, qkd_ref, wq_ref, u_ref, kdt_ref, gl_ref):
    lb = qkv_ref.shape[0]
    n_chunks = lb // CHUNK

    @pl.when(pl.program_id(1) == 0)
    def _():
        s_ref[...] = jnp.zeros_like(s_ref)

    q = qkv_ref[:, 0:DN_WIDTH]
    k = qkv_ref[:, DN_WIDTH:2 * DN_WIDTH]
    v = qkv_ref[:, 2 * DN_WIDTH:3 * DN_WIDTH]

    head_ones = head_ones_ref[...]
    qn = q * lax.rsqrt(_dot_exact_rhs(q * q, head_ones) + NORM_EPS) * (DN_HEAD_DIM ** -0.5)
    kn = k * lax.rsqrt(_dot_exact_rhs(k * k, head_ones) + NORM_EPS)

    ba = ba_ref[...]
    beta_c = _sigmoid(ba[:, 0:LANES])
    g_c = -jnp.exp(alog_r_ref[...]) * _softplus(ba[:, LANES:2 * LANES] + dtb_r_ref[...])
    lane = lax.broadcasted_iota(jnp.int32, (lb, LANES), 1)
    g_c = jnp.where(lane < DN_HEADS, g_c, 0.0)
    bat = bat_ref[...]
    g_r = -jnp.exp(alog_c_ref[...]) * _softplus(bat[SUBLANES:2 * SUBLANES] + dtb_c_ref[...])

    gc = _dot_exact_lhs(tril_ref[...], g_c)
    gtot = _dot_exact_lhs(chunk_ones_ref[...], g_c)
    gr = _dot_exact_rhs(g_r, triu_ref[...])
    for c in range(n_chunks):
        gr_ref[c] = gr[:, c * CHUNK:(c + 1) * CHUNK]

    expand = expand_ref[...]
    x_beta = _dot_exact_rhs(beta_c, expand)
    x_gc = _dot_exact_rhs(gc, expand)
    x_gtot = _dot_exact_rhs(gtot, expand)
    x_eg = jnp.exp(x_gc)
    xbeta_ref[...] = x_beta
    xgc_ref[...] = x_gc
    xgl_ref[...] = jnp.exp(x_gtot)
    qn_ref[...] = qn.astype(BF16)
    kn_ref[...] = kn.astype(BF16)
    kbe_ref[...] = (kn * (x_beta * x_eg)).astype(BF16)
    vb_ref[...] = (v * x_beta).astype(BF16)
    qg_ref[...] = (qn * x_eg).astype(BF16)
    kd_ref[...] = (kn * jnp.exp(x_gtot - x_gc)).astype(BF16)

    ci = lax.broadcasted_iota(jnp.int32, (CHUNK, CHUNK), 0)
    cj = lax.broadcasted_iota(jnp.int32, (CHUNK, CHUNK), 1)
    eye = (ci == cj).astype(F32)

    heads = range(DN_HEADS)
    lanes = [pl.ds(h * DN_HEAD_DIM, DN_HEAD_DIM) for h in heads]
    stack = lambda top, bot: jnp.concatenate([top, bot], axis=0)

    chunk_rows = lambda c: pl.ds(c * CHUNK, CHUNK)

    def within_chunks(chunk_ids):
        chains = [(c, h) for c in chunk_ids for h in heads]
        ids = range(len(chains))
        rows = [chunk_rows(c) for c, _ in chains]
        ln = [lanes[h] for _, h in chains]
        kb = [kn_ref[rows[i], ln[i]] for i in ids]
        kq = [_dot_nt(stack(kb[i], qn_ref[rows[i], ln[i]]), kb[i]) for i in ids]
        yield
        decay = [jnp.exp(jnp.where(ci >= cj, xgc_ref[rows[i], ln[i]] - gr_ref[c, h:h + 1, :], -jnp.inf))
                 for i, (c, h) in enumerate(chains)]
        a = [jnp.where(ci > cj, xbeta_ref[rows[i], ln[i]] * kq[i][0:CHUNK] * decay[i], 0.0) for i in ids]
        slot = [c * DN_HEADS + h for c, h in chains]
        for i in ids:
            qkd_ref[slot[i]] = (kq[i][CHUNK:2 * CHUNK] * decay[i]).astype(BF16)
            wq_ref[slot[i], CHUNK:2 * CHUNK, :] = qg_ref[rows[i], ln[i]]
            kdt_ref[slot[i]] = kd_ref[rows[i], ln[i]].T.astype(BF16)
            gl_ref[slot[i]] = xgl_ref[rows[i], ln[i]]
        t_inv = [eye - a[i] for i in ids]
        pw = [a[i].astype(BF16) for i in ids]
        pw = [_dot(pw[i], pw[i]).astype(BF16) for i in ids]
        yield
        for _ in range(4):
            r = [_dot(stack(t_inv[i].astype(BF16), pw[i]), pw[i]) for i in ids]
            yield
            t_inv = [t_inv[i] + r[i][0:CHUNK] for i in ids]
            pw = [r[i][CHUNK:2 * CHUNK].astype(BF16) for i in ids]
        t_inv = [t_inv[i] + _dot(t_inv[i].astype(BF16), pw[i]) for i in ids]
        yield
        tb = [t_inv[i].astype(BF16) for i in ids]
        for i in ids:
            wq_ref[slot[i], 0:CHUNK, :] = _dot(tb[i], kbe_ref[rows[i], ln[i]]).astype(BF16)
        yield
        for i in ids:
            u_ref[slot[i]] = _dot(tb[i], vb_ref[rows[i], ln[i]])
        yield

    def state_chunks(chunk_ids):
        for c in chunk_ids:
            rows = chunk_rows(c)
            slot = [c * DN_HEADS + h for h in heads]
            s = [s_ref[h] for h in heads]
            ws = [_dot(wq_ref[slot[h]], s[h].astype(BF16)) for h in heads]
            yield
            vnb = [(u_ref[slot[h]] - ws[h][0:CHUNK]).astype(BF16) for h in heads]
            for h in heads:
                oacc_ref[h, rows, :] = ws[h][CHUNK:2 * CHUNK] + _dot(qkd_ref[slot[h]], vnb[h])
            yield
            for h in heads:
                s_ref[h] = s[h] * gl_ref[slot[h]] + _dot(kdt_ref[slot[h]], vnb[h])
            yield

    groups = [list(range(g, min(g + INTRA_GROUP, n_chunks))) for g in range(0, n_chunks, INTRA_GROUP)]
    pending = iter(())
    for grp in groups:
        for _ in within_chunks(grp):
            next(pending, None)
        for _ in pending:
            pass
        pending = state_chunks(grp)
    for _ in pending:
        pass

    o = jnp.concatenate([oacc_ref[h] for h in heads], axis=1)
    ms = _dot_exact_rhs(o * o, head_ones) * (1.0 / DN_HEAD_DIM)
    z = z_ref[...]
    o_ref[...] = (o * lax.rsqrt(ms + NORM_EPS) * dnw_ref[...] * (z * _sigmoid(z))).astype(o_ref.dtype)


def _deltanet_masks(lb):
    head = jnp.arange(DN_WIDTH) // DN_HEAD_DIM
    head_ones = head[:, None] == head[None, :]
    pos = jnp.arange(lb)
    same = (pos[:, None] // CHUNK) == (pos[None, :] // CHUNK)
    tril = same & (pos[None, :] <= pos[:, None])
    triu = same & (pos[:, None] <= pos[None, :])
    expand = jnp.arange(LANES)[:, None] == head[None, :]
    return tuple(m.astype(BF16) for m in (head_ones, tril, triu, same, expand))


def _deltanet(qkv, z, ba, bat, alog_r, dtb_r, alog_c, dtb_c, dnw, batch, seq):
    lb = DN_LB
    nb = seq // lb
    slots = lb // CHUNK * DN_HEADS
    masks = _deltanet_masks(lb)
    full = lambda a: pl.BlockSpec(a.shape, lambda b, j: (0,) * a.ndim)
    row = lambda n: pl.BlockSpec((lb, n), lambda b, j: (b * nb + j, 0))
    return pl.pallas_call(
        _deltanet_kernel,
        grid=(batch, nb),
        in_specs=[row(3 * DN_WIDTH), row(DN_WIDTH), row(2 * LANES),
                  pl.BlockSpec((2 * SUBLANES, lb), lambda b, j: (0, b * nb + j)),
                  full(alog_r), full(dtb_r), full(alog_c), full(dtb_c), full(dnw)] + [full(m) for m in masks],
        out_specs=row(DN_WIDTH),
        out_shape=jax.ShapeDtypeStruct((batch * seq, DN_WIDTH), BF16),
        scratch_shapes=[
            pltpu.VMEM((DN_HEADS, DN_HEAD_DIM, DN_HEAD_DIM), F32),
            pltpu.VMEM((lb, DN_WIDTH), BF16),
            pltpu.VMEM((lb, DN_WIDTH), BF16),
            pltpu.VMEM((lb, DN_WIDTH), BF16),
            pltpu.VMEM((lb, DN_WIDTH), BF16),
            pltpu.VMEM((lb, DN_WIDTH), BF16),
            pltpu.VMEM((lb, DN_WIDTH), BF16),
            pltpu.VMEM((lb, DN_WIDTH), F32),
            pltpu.VMEM((lb, DN_WIDTH), F32),
            pltpu.VMEM((lb, DN_WIDTH), F32),
            pltpu.VMEM((lb // CHUNK, DN_HEADS, CHUNK), F32),
            pltpu.VMEM((DN_HEADS, lb, DN_HEAD_DIM), F32),
            pltpu.VMEM((slots, CHUNK, CHUNK), BF16),
            pltpu.VMEM((slots, 2 * CHUNK, DN_HEAD_DIM), BF16),
            pltpu.VMEM((slots, CHUNK, DN_HEAD_DIM), F32),
            pltpu.VMEM((slots, DN_HEAD_DIM, CHUNK), BF16),
            pltpu.VMEM((slots, DN_HEAD_DIM, DN_HEAD_DIM), F32),
        ],
        compiler_params=pltpu.CompilerParams(dimension_semantics=("arbitrary", "arbitrary"),
                                             vmem_limit_bytes=VMEM_LIMIT),
        name="deltanet",
    )(qkv, z, ba, bat, alog_r, dtb_r, alog_c, dtb_c, dnw, *masks)


def _mix_kernel(x_ref, on_ref, u_ref, gates_ref, wupa_ref, poolw_ref, pscale_ref, wupb_ref, wout_ref,
                gffn_ref, wr_hi_ref, wr_lo_ref, br_ref,
                x1_ref, h2_ref, route_ref, counts_ref, ucarry_ref, ecount_ref, logits_ref, *, steps_per_seq):
    i = pl.program_id(0)
    tm = x_ref.shape[0]

    @pl.when(i == 0)
    def _():
        ecount_ref[...] = jnp.zeros_like(ecount_ref)
        logits_ref[...] = jnp.zeros_like(logits_ref)

    @pl.when(i % steps_per_seq == 0)
    def _():
        ucarry_ref[...] = jnp.zeros_like(ucarry_ref)

    lane = lax.broadcasted_iota(jnp.int32, (tm, LANES), 1)
    lg = jnp.where(lane < N_EXPERTS, logits_ref[...], -jnp.inf)
    vals, idxs, sels = [], [], []
    for _ in range(TOP_K):
        m = jnp.max(lg, axis=-1, keepdims=True)
        idx = jnp.min(jnp.where(lg == m, lane, LANES), axis=-1, keepdims=True)
        sel = lane == idx
        vals.append(m)
        idxs.append(idx)
        sels.append(sel)
        lg = jnp.where(sel, -jnp.inf, lg)
    es = [jnp.exp(vk - vals[0]) for vk in vals]
    denom = es[0] + es[1] + es[2] + es[3]
    chosen = jnp.zeros((tm, LANES), F32)
    for sel in sels:
        chosen = chosen + jnp.where(sel, 1.0, 0.0)
    chosen = jnp.where(i > 0, chosen, 0.0)

    u = u_ref[...]
    ext = jnp.concatenate([ucarry_ref[...], u], axis=0)
    ucarry_ref[...] = u[tm - POOL_HALO:tm]
    t_pos = (i % steps_per_seq) * tm + lax.broadcasted_iota(jnp.int32, (tm, POOL_GROUP_DIM), 0)
    ys = []
    for g in range(POOL_GROUPS):
        s = ext[:, g * POOL_GROUP_DIM:(g + 1) * POOL_GROUP_DIM]
        shift = 1
        while shift < POOL_WINDOWS[g]:
            s = s + pltpu.roll(s, shift, 0)
            shift *= 2
        count = jnp.minimum(t_pos + 1, POOL_WINDOWS[g]).astype(F32)
        pooled = s[POOL_HALO:] / count - u[:, g * POOL_GROUP_DIM:(g + 1) * POOL_GROUP_DIM]
        ys.append(_dot(pooled.astype(BF16), poolw_ref[g]))
    yb = jnp.concatenate(ys, axis=-1) * pscale_ref[...]
    y_b = _dot(yb.astype(BF16), wupb_ref[...])
    y_a = _dot(on_ref[...], wupa_ref[...])
    gates = gates_ref[...]
    merged = _sigmoid(gates[:, 0:D_MODEL]) * y_a + _sigmoid(gates[:, D_MODEL:2 * D_MODEL]) * y_b
    x1 = x_ref[...] + _dot(merged.astype(BF16), wout_ref[...])
    x1_ref[...] = x1

    h2 = _rms(x1, gffn_ref[...])
    _store_row_tiles(h2_ref, h2)

    hi, lo = _split2(h2)
    logits_ref[...] = (_dot(hi, wr_hi_ref[...]) + _dot(lo, wr_hi_ref[...]) + _dot(hi, wr_lo_ref[...])
                       + br_ref[...])

    ri = lax.broadcasted_iota(jnp.int32, (tm, tm), 0)
    rj = lax.broadcasted_iota(jnp.int32, (tm, tm), 1)
    before = (rj < ri).astype(BF16)
    pos = ecount_ref[...] + _dot(before, chosen.astype(BF16))
    ecount_ref[...] = ecount_ref[...] + jnp.sum(chosen, axis=0, keepdims=True)
    counts_ref[...] = ecount_ref[...]

    route = jnp.zeros((tm, LANES), F32)
    for kk in range(TOP_K):
        rank = jnp.sum(jnp.where(sels[kk], pos, 0.0), axis=-1, keepdims=True)
        route = jnp.where(lane == kk, rank, route)
        route = jnp.where(lane == TOP_K + kk, idxs[kk].astype(F32), route)
        route = jnp.where(lane == 2 * TOP_K + kk, es[kk] / denom, route)
    route_ref[...] = route


def _mix(x2, on, u, gates, wupa, poolw, pscale, wupb, wout, gffn, wr_hi, wr_lo, br, batch, seq):
    tm = MIX_TM
    t = batch * seq
    n = t // tm
    full = lambda a: pl.BlockSpec(a.shape, lambda i: (0,) * a.ndim)
    row = lambda c: pl.BlockSpec((tm, c), lambda i: (jnp.minimum(i, n - 1), 0))
    return pl.pallas_call(
        functools.partial(_mix_kernel, steps_per_seq=seq // tm),
        grid=(n + 1,),
        in_specs=[row(D_MODEL), row(DN_WIDTH), row(POOL_WIDTH), row(2 * D_MODEL),
                  full(wupa), full(poolw), full(pscale), full(wupb), full(wout), full(gffn),
                  full(wr_hi), full(wr_lo), full(br)],
        out_specs=[pl.BlockSpec((tm, D_MODEL), lambda i: (i, 0)),
                   pl.BlockSpec((tm * ROW_SUB, LANES), lambda i: (i, 0)),
                   pl.BlockSpec((tm, LANES), lambda i: (jnp.maximum(i - 1, 0), 0)),
                   pl.BlockSpec((1, LANES), lambda i: (0, 0))],
        out_shape=[jax.ShapeDtypeStruct((t + tm, D_MODEL), F32),
                   jax.ShapeDtypeStruct(((t + tm) * ROW_SUB, LANES), F32),
                   jax.ShapeDtypeStruct((t, LANES), F32),
                   jax.ShapeDtypeStruct((1, LANES), F32)],
        scratch_shapes=[pltpu.VMEM((POOL_HALO, POOL_WIDTH), F32), pltpu.VMEM((1, LANES), F32),
                        pltpu.VMEM((tm, LANES), F32)],
        compiler_params=pltpu.CompilerParams(dimension_semantics=("arbitrary",), vmem_limit_bytes=VMEM_LIMIT),
        name="mix",
    )(x2, on, u, gates, wupa, poolw, pscale, wupb, wout, gffn, wr_hi, wr_lo, br)


def _routing_tables(route, counts, tg, n_tiles):
    cnt = counts[0, 0:N_EXPERTS].astype(jnp.int32)
    padded = (cnt + (tg - 1)) // tg * tg
    ends = jnp.cumsum(padded)
    offs = ends - padded
    rank = route[:, 0:TOP_K].astype(jnp.int32)
    eidx = route[:, TOP_K:2 * TOP_K].astype(jnp.int32)
    experts = jnp.arange(N_EXPERTS, dtype=jnp.int32)
    dest = jnp.sum(jnp.where(eidx[..., None] == experts, offs, 0), axis=-1) + rank
    n_used = ends[N_EXPERTS - 1] // tg
    tile_start = jnp.arange(n_tiles, dtype=jnp.int32) * tg
    tile_expert = jnp.sum((tile_start[:, None] >= ends[None, :]).astype(jnp.int32), axis=1)
    tile_expert = jnp.minimum(tile_expert, N_EXPERTS - 1)
    last = tile_expert[jnp.maximum(n_used - 1, 0)]
    tile_expert = jnp.where(jnp.arange(n_tiles) < n_used, tile_expert, last)
    pad_start = offs + cnt
    pad_len = padded - cnt
    later = (experts[None, :] > experts[:, None]) & (cnt[None, :] > 0)
    next_expert = jnp.min(jnp.where(later, experts[None, :], N_EXPERTS), axis=1)
    next_expert = jnp.where(next_expert == N_EXPERTS, experts, next_expert)
    return (dest, tile_expert.astype(jnp.int32), next_expert.astype(jnp.int32),
            n_used.reshape(1).astype(jnp.int32), pad_start.astype(jnp.int32), pad_len.astype(jnp.int32))


def _dispatch_kernel(pad_start_ref, pad_len_ref, nu_ref, dest_ref, h2_ref, xs_ref, zero_ref, sem, zsem):
    tm = h2_ref.shape[0] // ROW_SUB
    tg = zero_ref.shape[0]
    n_tiles = xs_ref.shape[0] // tg

    @pl.when(pl.program_id(0) == 0)
    def _():
        zero_ref[...] = jnp.zeros_like(zero_ref)

        def pad_copy(d):
            return pltpu.make_async_copy(_row_tile(zero_ref, 0), _row_tile(xs_ref, d), zsem)

        def tile_copy(i):
            return pltpu.make_async_copy(zero_ref, xs_ref.at[pl.ds(pl.multiple_of(i * tg, tg), tg)], zsem)

        def per_expert(e, c):
            lax.fori_loop(0, pad_len_ref[e], lambda r, cc: (pad_copy(pad_start_ref[e] + r).start(), cc)[1], 0)
            lax.fori_loop(0, pad_len_ref[e], lambda r, cc: (pad_copy(0).wait(), cc)[1], 0)
            return c

        lax.fori_loop(0, N_EXPERTS, per_expert, 0)
        lax.fori_loop(nu_ref[0], n_tiles, lambda i, cc: (tile_copy(i).start(), cc)[1], 0)
        lax.fori_loop(nu_ref[0], n_tiles, lambda i, cc: (tile_copy(0).wait(), cc)[1], 0)

    def row_copy(r, d):
        return pltpu.make_async_copy(_row_tile(h2_ref, r), _row_tile(xs_ref, d), sem)

    def start(r, c):
        for kk in range(TOP_K):
            row_copy(r, dest_ref[0, 0, r * TOP_K + kk]).start(priority=kk % 2)
        return c

    lax.fori_loop(0, tm, start, 0, unroll=DMA_UNROLL)
    for kk in range(TOP_K):
        pltpu.make_async_copy(h2_ref, xs_ref.at[pl.ds(0, tm * ROW_SUB)], sem).wait()


def _dispatch(pad_start, pad_len, n_used, dest3, h2, n_rows):
    t = dest3.shape[0] * dest3.shape[2] // TOP_K
    tm = DSP_TM
    return pl.pallas_call(
        _dispatch_kernel,
        grid_spec=pltpu.PrefetchScalarGridSpec(
            num_scalar_prefetch=3,
            grid=(t // tm,),
            in_specs=[pl.BlockSpec((1, 1, tm * TOP_K), lambda i, ps, pn, nu: (i, 0, 0), memory_space=pltpu.SMEM),
                      pl.BlockSpec((tm * ROW_SUB, LANES), lambda i, ps, pn, nu: (i, 0))],
            out_specs=pl.BlockSpec(memory_space=pl.ANY),
            scratch_shapes=[pltpu.VMEM((MOE_TG * ROW_SUB, LANES), F32), pltpu.SemaphoreType.DMA,
                            pltpu.SemaphoreType.DMA]),
        out_shape=jax.ShapeDtypeStruct((n_rows * ROW_SUB, LANES), F32),
        compiler_params=pltpu.CompilerParams(dimension_semantics=("arbitrary",), vmem_limit_bytes=VMEM_LIMIT),
        name="dispatch",
    )(pad_start, pad_len, n_used, dest3, h2)


def _experts_kernel(te_ref, nx_ref, nu_ref, xs_ref, wgu_hbm_ref, bgu_ref, wd_hbm_ref, bd_ref, ys_ref,
                    wgu_f32_ref, wd_f32_ref, wgu_bf_ref, wd_bf_ref, sem):
    i = pl.program_id(0)
    e = te_ref[i]

    def weight_copies(ex):
        return (pltpu.make_async_copy(wgu_hbm_ref.at[ex], wgu_f32_ref, sem.at[0]),
                pltpu.make_async_copy(wd_hbm_ref.at[ex], wd_f32_ref, sem.at[1]))

    @pl.when(i == 0)
    def _():
        for cp in weight_copies(e):
            cp.start()

    @pl.when((i == 0) | (e != te_ref[jnp.maximum(i - 1, 0)]))
    def _():
        for cp in weight_copies(e):
            cp.wait()
        wgu_bf_ref[...] = wgu_f32_ref[...].astype(BF16)
        wd_bf_ref[...] = wd_f32_ref[...].astype(BF16)

        @pl.when(nx_ref[e] != e)
        def _():
            for cp in weight_copies(nx_ref[e]):
                cp.start()

    @pl.when(i < nu_ref[0])
    def _():
        gu = _dot(_load_row_tiles(xs_ref).astype(BF16), wgu_bf_ref[...]) + bgu_ref[...]
        gate = jnp.minimum(gu[:, 0:D_FF], SWIGLU_LIMIT)
        up = jnp.clip(gu[:, D_FF:2 * D_FF], -SWIGLU_LIMIT, SWIGLU_LIMIT)
        act = gate * _sigmoid(SWIGLU_ALPHA * gate) * (up + 1.0)
        _store_row_tiles(ys_ref, _dot(act.astype(BF16), wd_bf_ref[...]) + bd_ref[...])

    @pl.when(i >= nu_ref[0])
    def _():
        ys_ref[...] = jnp.zeros_like(ys_ref)


def _experts(tile_expert, next_expert, n_used, xs, wgu, bgu, wd, bd):
    tg = MOE_TG * ROW_SUB
    n_tiles = xs.shape[0] // tg
    tile = lambda i, te, nx, nu: (jnp.minimum(i, nu[0] - 1), 0)
    expert = lambda i, te, nx, nu: (te[i], 0, 0)
    return pl.pallas_call(
        _experts_kernel,
        grid_spec=pltpu.PrefetchScalarGridSpec(
            num_scalar_prefetch=3,
            grid=(n_tiles,),
            in_specs=[pl.BlockSpec((tg, LANES), tile),
                      pl.BlockSpec(memory_space=pl.ANY),
                      pl.BlockSpec((None, 1, 2 * D_FF), expert),
                      pl.BlockSpec(memory_space=pl.ANY),
                      pl.BlockSpec((None, 1, D_MODEL), expert)],
            out_specs=pl.BlockSpec((tg, LANES), lambda i, te, nx, nu: (i, 0)),
            scratch_shapes=[pltpu.VMEM((D_MODEL, 2 * D_FF), F32), pltpu.VMEM((D_FF, D_MODEL), F32),
                            pltpu.VMEM((D_MODEL, 2 * D_FF), BF16), pltpu.VMEM((D_FF, D_MODEL), BF16),
                            pltpu.SemaphoreType.DMA((2,))]),
        out_shape=jax.ShapeDtypeStruct(xs.shape, F32),
        compiler_params=pltpu.CompilerParams(dimension_semantics=("arbitrary",), vmem_limit_bytes=VMEM_LIMIT),
        name="experts",
    )(tile_expert, next_expert, n_used, xs, wgu, bgu, wd, bd)


def _combine_kernel(dest_ref, dest_next_ref, route_ref, x1_ref, gfin_ref, ys_ref, out_ref, ybuf_ref, sem):
    i = pl.program_id(0)
    tm = x1_ref.shape[0]

    def gather_rows(dref, buf):
        def start(r, c):
            for kk in range(TOP_K):
                pltpu.make_async_copy(_row_tile(ys_ref, dref[0, 0, r * TOP_K + kk]),
                                      _row_tile(ybuf_ref.at[buf, kk], r), sem.at[buf]).start(priority=kk % 2)
            return c
        lax.fori_loop(0, tm, start, 0, unroll=DMA_UNROLL)

    @pl.when(i == 0)
    def _():
        gather_rows(dest_ref, 0)

    @pl.when(i + 1 < pl.num_programs(0))
    def _():
        gather_rows(dest_next_ref, (i + 1) % 2)

    buf = i % 2
    for kk in range(TOP_K):
        pltpu.make_async_copy(ys_ref.at[pl.ds(0, tm * ROW_SUB)], ybuf_ref.at[buf, kk], sem.at[buf]).wait()
    route = route_ref[...]
    acc = x1_ref[...]
    for kk in range(TOP_K):
        acc = acc + route[:, 2 * TOP_K + kk:2 * TOP_K + kk + 1] * _load_row_tiles(ybuf_ref.at[buf, kk])
    out_ref[...] = _rms(acc, gfin_ref[...])


def _combine(dest3, route, x1, gfin, ys):
    t = route.shape[0]
    tm = CMB_TM
    last = t // tm - 1
    return pl.pallas_call(
        _combine_kernel,
        grid=(t // tm,),
        in_specs=[pl.BlockSpec((1, 1, tm * TOP_K), lambda i: (i, 0, 0), memory_space=pltpu.SMEM),
                  pl.BlockSpec((1, 1, tm * TOP_K), lambda i: (jnp.minimum(i + 1, last), 0, 0),
                               memory_space=pltpu.SMEM),
                  pl.BlockSpec((tm, LANES), lambda i: (i, 0)),
                  pl.BlockSpec((tm, D_MODEL), lambda i: (i, 0)),
                  pl.BlockSpec((1, D_MODEL), lambda i: (0, 0)),
                  pl.BlockSpec(memory_space=pl.ANY)],
        out_specs=pl.BlockSpec((tm, D_MODEL), lambda i: (i, 0)),
        out_shape=jax.ShapeDtypeStruct((t, D_MODEL), F32),
        scratch_shapes=[pltpu.VMEM((2, TOP_K, tm * ROW_SUB, LANES), F32), pltpu.SemaphoreType.DMA((2,))],
        compiler_params=pltpu.CompilerParams(dimension_semantics=("arbitrary",), vmem_limit_bytes=VMEM_LIMIT),
        name="combine",
    )(dest3, dest3, route, x1, gfin, ys)


def kernel(x, g_mix, w_in, conv_w, a_log, dt_bias, dn_norm, w_up_a, pool_w, pool_scale, w_up_b, w_out, g_ffn,
           w_router, b_router, w_gate_up, b_gate_up, w_down, b_down, g_final):
    batch, seq, d = x.shape
    assert d == D_MODEL and seq % DN_LB == 0 and seq % MIX_TM == 0
    assert (batch * seq) % DSP_TM == 0 and (batch * seq * TOP_K) % MOE_TG == 0
    assert g_mix.shape[0] == 1, "one layer"
    t = batch * seq
    x2 = x.reshape(t, d)

    w = w_in[0]
    o_z = 3 * DN_WIDTH
    o_b = o_z + DN_WIDTH
    o_a = o_b + DN_HEADS
    o_u = o_a + DN_HEADS
    o_g = o_u + POOL_WIDTH
    wqkv = w[:, 0:o_z].astype(BF16)
    wz = w[:, o_z:o_b].astype(BF16)
    wu = w[:, o_u:o_g].astype(BF16)
    wg = w[:, o_g:].astype(BF16)
    w_b = w[:, o_b:o_a]
    w_a = w[:, o_a:o_u]
    pad = jnp.zeros((d, LANES - DN_HEADS), F32)
    wba = jnp.concatenate([w_b, pad, w_a, pad], axis=1).astype(BF16)
    wbat = jnp.concatenate([w_b, w_a], axis=1).T.astype(BF16)

    qkv, z, u, gates, ba, bat = _in_proj(x2, g_mix, wqkv, wz, wu, wg, wba, wbat, conv_w[0], seq)

    lane_pad = lambda p: jnp.pad(p.reshape(1, DN_HEADS), ((0, 0), (0, LANES - DN_HEADS)))
    on = _deltanet(qkv, z, ba, bat, lane_pad(a_log[0]), lane_pad(dt_bias[0]),
                   a_log[0].reshape(DN_HEADS, 1), dt_bias[0].reshape(DN_HEADS, 1),
                   jnp.tile(dn_norm[0], DN_HEADS).reshape(1, DN_WIDTH), batch, seq)

    wr = jnp.pad(w_router[0], ((0, 0), (0, LANES - N_EXPERTS)))
    wr_hi = wr.astype(BF16)
    wr_lo = (wr - wr_hi.astype(F32)).astype(BF16)
    br = jnp.pad(b_router[0].reshape(1, N_EXPERTS), ((0, 0), (0, LANES - N_EXPERTS)))
    x1, h2, route, counts = _mix(x2, on, u, gates, w_up_a[0].astype(BF16), pool_w[0].astype(BF16),
                                  pool_scale[0].reshape(1, POOL_WIDTH), w_up_b[0].astype(BF16),
                                  w_out[0].astype(BF16), g_ffn, wr_hi, wr_lo, br, batch, seq)

    n_tiles = t * TOP_K // MOE_TG + N_EXPERTS
    dest, tile_expert, next_expert, n_used, pad_start, pad_len = _routing_tables(route, counts, MOE_TG, n_tiles)
    xs = _dispatch(pad_start, pad_len, n_used, dest.reshape(t // DSP_TM, 1, DSP_TM * TOP_K), h2, n_tiles * MOE_TG)
    ys = _experts(tile_expert, next_expert, n_used, xs, w_gate_up[0],
                  b_gate_up[0].reshape(N_EXPERTS, 1, 2 * D_FF), w_down[0], b_down[0].reshape(N_EXPERTS, 1, D_MODEL))
    out = _combine(dest.reshape(t // CMB_TM, 1, CMB_TM * TOP_K), route, x1, g_final.reshape(1, D_MODEL), ys)
    return out.reshape(batch, seq, d)
```

```python
import functools

import jax
import jax.numpy as jnp
from jax import lax
from jax.experimental import pallas as pl
from jax.experimental.pallas import tpu as pltpu

F32 = jnp.float32
BF16 = jnp.bfloat16

D_MODEL = 1024
CHUNK = 64
DN_HEADS = 8
DN_HEAD_DIM = 64
DN_WIDTH = DN_HEADS * DN_HEAD_DIM
CONV_WIDTH = 4
POOL_GROUPS = 4
POOL_WINDOWS = (2, 4, 8, 16)
POOL_WIDTH = 512
POOL_GROUP_DIM = 128
POOL_HALO = 16
N_EXPERTS = 32
TOP_K = 4
D_FF = D_MODEL
SWIGLU_LIMIT = 7.0
SWIGLU_ALPHA = 1.702
NORM_EPS = 1e-6
LANES = 128
SUBLANES = 8
VMEM_LIMIT = 56 * 1024 * 1024

IN_TM = 256
DN_LB = 256
INTRA_GROUP = 2
MIX_TM = 256
MOE_TG = 512
DSP_TM = 1024
CMB_TM = 512
DMA_UNROLL = 8


def _dot(a, b):
    return jnp.dot(a, b, preferred_element_type=F32)


def _dot_nt(a, b):
    return lax.dot_general(a, b, (((1,), (1,)), ((), ())), preferred_element_type=F32)


def _split2(x):
    hi = x.astype(BF16)
    lo = (x - hi.astype(F32)).astype(BF16)
    return hi, lo


def _dot_exact_rhs(x, m):
    hi, lo = _split2(x)
    return _dot(hi, m) + _dot(lo, m)


def _dot_exact_lhs(m, x):
    hi, lo = _split2(x)
    return _dot(m, hi) + _dot(m, lo)


def _softplus(x):
    return jnp.maximum(x, 0.0) + jnp.log1p(jnp.exp(-jnp.abs(x)))


def _sigmoid(x):
    return 1.0 / (1.0 + jnp.exp(-x))


def _rms(x, g):
    return x * lax.rsqrt(jnp.mean(x * x, axis=-1, keepdims=True) + NORM_EPS) * g


ROW_SUB = D_MODEL // LANES


def _row_tile(ref, r):
    return ref.at[pl.ds(pl.multiple_of(r * ROW_SUB, ROW_SUB), ROW_SUB)]


def _store_row_tiles(ref, x):
    for j in range(ROW_SUB):
        ref[pl.ds(j, x.shape[0], stride=ROW_SUB), :] = x[:, j * LANES:(j + 1) * LANES]


def _load_row_tiles(ref):
    n = ref.shape[0] // ROW_SUB
    return jnp.concatenate([ref[pl.ds(j, n, stride=ROW_SUB), :] for j in range(ROW_SUB)], axis=1)


def _in_proj_kernel(x_ref, g_ref, wqkv_ref, wz_ref, wu_ref, wg_ref, wba_ref, wbat_ref, convw_ref,
                    qkv_ref, z_ref, u_ref, gates_ref, ba_ref, bat_ref, hist_ref, *, steps_per_seq):
    i = pl.program_id(0)
    tm = x_ref.shape[0]

    @pl.when(i == 0)
    def _():
        hist_ref[...] = jnp.zeros_like(hist_ref)

    @pl.when((i + steps_per_seq - 1) % steps_per_seq == 0)
    def _():
        hist_ref[0:SUBLANES, :] = jnp.zeros((SUBLANES, hist_ref.shape[1]), F32)

    cw = convw_ref[...]
    prev = hist_ref[SUBLANES:SUBLANES + tm, :]
    act = prev * cw[CONV_WIDTH - 1:CONV_WIDTH]
    for s in range(1, CONV_WIDTH):
        act = act + hist_ref[pl.ds(SUBLANES - s, tm), :] * cw[CONV_WIDTH - 1 - s:CONV_WIDTH - s]
    qkv_ref[...] = act * _sigmoid(act)

    hb = _rms(x_ref[...], g_ref[...]).astype(BF16)
    hist_ref[0:SUBLANES, :] = prev[tm - SUBLANES:tm]
    hist_ref[SUBLANES:SUBLANES + tm, :] = _dot(hb, wqkv_ref[...])
    z_ref[...] = _dot(hb, wz_ref[...])
    u_ref[...] = _dot(hb, wu_ref[...])
    gates_ref[...] = _dot(hb, wg_ref[...])
    ba_ref[...] = _dot(hb, wba_ref[...])
    bat_ref[...] = _dot_nt(wbat_ref[...], hb)


def _in_proj(x2, g_mix, wqkv, wz, wu, wg, wba, wbat, conv_w, seq):
    t = x2.shape[0]
    tm = IN_TM
    n = t // tm
    full = lambda a: pl.BlockSpec(a.shape, lambda i: (0, 0))
    row = lambda c: pl.BlockSpec((tm, c), lambda i: (jnp.minimum(i, n - 1), 0))
    return pl.pallas_call(
        functools.partial(_in_proj_kernel, steps_per_seq=seq // tm),
        grid=(n + 1,),
        in_specs=[row(D_MODEL), full(g_mix), full(wqkv), full(wz), full(wu), full(wg), full(wba), full(wbat),
                  full(conv_w)],
        out_specs=[pl.BlockSpec((tm, 3 * DN_WIDTH), lambda i: (jnp.maximum(i - 1, 0), 0)),
                   row(DN_WIDTH), row(POOL_WIDTH), row(2 * D_MODEL), row(2 * LANES),
                   pl.BlockSpec((2 * SUBLANES, tm), lambda i: (0, jnp.minimum(i, n - 1)))],
        out_shape=[jax.ShapeDtypeStruct((t, 3 * DN_WIDTH), F32),
                   jax.ShapeDtypeStruct((t, DN_WIDTH), F32),
                   jax.ShapeDtypeStruct((t, POOL_WIDTH), F32),
                   jax.ShapeDtypeStruct((t, 2 * D_MODEL), F32),
                   jax.ShapeDtypeStruct((t, 2 * LANES), F32),
                   jax.ShapeDtypeStruct((2 * SUBLANES, t), F32)],
        scratch_shapes=[pltpu.VMEM((SUBLANES + tm, 3 * DN_WIDTH), F32)],
        compiler_params=pltpu.CompilerParams(dimension_semantics=("arbitrary",), vmem_limit_bytes=VMEM_LIMIT),
        name="in_proj",
    )(x2, g_mix, wqkv, wz, wu, wg, wba, wbat, conv_w)


def _deltanet_kernel(qkv_ref, z_ref, ba_ref, bat_ref, alog_r_ref, dtb_r_ref, alog_c_ref, dtb_c_ref, dnw_ref,
                     head_ones_ref, tril_ref, triu_ref, chunk_ones_ref, expand_ref, o_ref,
                     s_ref, qn_ref, kn_ref, kbe_ref, vb_ref, qg_ref, kd_ref,
                     xbeta_ref, xgc_ref, xgl_ref, gr_ref, oacc_ref, qkd_ref, wq_ref, u_ref, kdt_ref, gl_ref):
    lb = qkv_ref.shape[0]
    n_chunks = lb // CHUNK

    @pl.when(pl.program_id(1) == 0)
    def _():
        s_ref[...] = jnp.zeros_like(s_ref)

    q = qkv_ref[:, 0:DN_WIDTH]
    k = qkv_ref[:, DN_WIDTH:2 * DN_WIDTH]
    v = qkv_ref[:, 2 * DN_WIDTH:3 * DN_WIDTH]

    head_ones = head_ones_ref[...]
    qn = q * lax.rsqrt(_dot_exact_rhs(q * q, head_ones) + NORM_EPS) * (DN_HEAD_DIM ** -0.5)
    kn = k * lax.rsqrt(_dot_exact_rhs(k * k, head_ones) + NORM_EPS)

    ba = ba_ref[...]
    beta_c = _sigmoid(ba[:, 0:LANES])
    g_c = -jnp.exp(alog_r_ref[...]) * _softplus(ba[:, LANES:2 * LANES] + dtb_r_ref[...])
    lane = lax.broadcasted_iota(jnp.int32, (lb, LANES), 1)
    g_c = jnp.where(lane < DN_HEADS, g_c, 0.0)
    bat = bat_ref[...]
    g_r = -jnp.exp(alog_c_ref[...]) * _softplus(bat[SUBLANES:2 * SUBLANES] + dtb_c_ref[...])

    gc = _dot_exact_lhs(tril_ref[...], g_c)
    gtot = _dot_exact_lhs(chunk_ones_ref[...], g_c)
    gr = _dot_exact_rhs(g_r, triu_ref[...])
    for c in range(n_chunks):
        gr_ref[c] = gr[:, c * CHUNK:(c + 1) * CHUNK]

    expand = expand_ref[...]
    x_beta = _dot_exact_rhs(beta_c, expand)
    x_gc = _dot_exact_rhs(gc, expand)
    x_gtot = _dot_exact_rhs(gtot, expand)
    x_eg = jnp.exp(x_gc)
    xbeta_ref[...] = x_beta
    xgc_ref[...] = x_gc
    xgl_ref[...] = jnp.exp(x_gtot)
    qn_ref[...] = qn.astype(BF16)
    kn_ref[...] = kn.astype(BF16)
    kbe_ref[...] = (kn * (x_beta * x_eg)).astype(BF16)
    vb_ref[...] = (v * x_beta).astype(BF16)
    qg_ref[...] = (qn * x_eg).astype(BF16)
    kd_ref[...] = (kn * jnp.exp(x_gtot - x_gc)).astype(BF16)

    ci = lax.broadcasted_iota(jnp.int32, (CHUNK, CHUNK), 0)
    cj = lax.broadcasted_iota(jnp.int32, (CHUNK, CHUNK), 1)
    eye = (ci == cj).astype(F32)

    heads = range(DN_HEADS)
    lanes = [pl.ds(h * DN_HEAD_DIM, DN_HEAD_DIM) for h in heads]
    stack = lambda top, bot: jnp.concatenate([top, bot], axis=0)

    chunk_rows = lambda c: pl.ds(c * CHUNK, CHUNK)

    def within_chunks(chunk_ids):
        chains = [(c, h) for c in chunk_ids for h in heads]
        ids = range(len(chains))
        rows = [chunk_rows(c) for c, _ in chains]
        ln = [lanes[h] for _, h in chains]
        kb = [kn_ref[rows[i], ln[i]] for i in ids]
        kq = [_dot_nt(stack(kb[i], qn_ref[rows[i], ln[i]]), kb[i]) for i in ids]
        yield
        decay = [jnp.exp(jnp.where(ci >= cj, xgc_ref[rows[i], ln[i]] - gr_ref[c, h:h + 1, :], -jnp.inf))
                 for i, (c, h) in enumerate(chains)]
        a = [jnp.where(ci > cj, xbeta_ref[rows[i], ln[i]] * kq[i][0:CHUNK] * decay[i], 0.0) for i in ids]
        slot = [c * DN_HEADS + h for c, h in chains]
        for i in ids:
            qkd_ref[slot[i]] = (kq[i][CHUNK:2 * CHUNK] * decay[i]).astype(BF16)
            wq_ref[slot[i], CHUNK:2 * CHUNK, :] = qg_ref[rows[i], ln[i]]
            kdt_ref[slot[i]] = kd_ref[rows[i], ln[i]].T.astype(BF16)
            gl_ref[slot[i]] = xgl_ref[rows[i], ln[i]]
        t_inv = [eye - a[i] for i in ids]
        pw = [a[i].astype(BF16) for i in ids]
        pw = [_dot(pw[i], pw[i]).astype(BF16) for i in ids]
        yield
        for _ in range(4):
            r = [_dot(stack(t_inv[i].astype(BF16), pw[i]), pw[i]) for i in ids]
            yield
            t_inv = [t_inv[i] + r[i][0:CHUNK] for i in ids]
            pw = [r[i][CHUNK:2 * CHUNK].astype(BF16) for i in ids]
        t_inv = [t_inv[i] + _dot(t_inv[i].astype(BF16), pw[i]) for i in ids]
        yield
        tb = [t_inv[i].astype(BF16) for i in ids]
        for i in ids:
            wq_ref[slot[i], 0:CHUNK, :] = _dot(tb[i], kbe_ref[rows[i], ln[i]]).astype(BF16)
        yield
        for i in ids:
            u_ref[slot[i]] = _dot(tb[i], vb_ref[rows[i], ln[i]])
        yield

    def state_chunks(chunk_ids):
        for c in chunk_ids:
            rows = chunk_rows(c)
            slot = [c * DN_HEADS + h for h in heads]
            s = [s_ref[h] for h in heads]
            ws = [_dot(wq_ref[slot[h]], s[h].astype(BF16)) for h in heads]
            yield
            vnb = [(u_ref[slot[h]] - ws[h][0:CHUNK]).astype(BF16) for h in heads]
            for h in heads:
                oacc_ref[h, rows, :] = ws[h][CHUNK:2 * CHUNK] + _dot(qkd_ref[slot[h]], vnb[h])
            yield
            for h in heads:
                s_ref[h] = s[h] * gl_ref[slot[h]] + _dot(kdt_ref[slot[h]], vnb[h])
            yield

    groups = [list(range(g, min(g + INTRA_GROUP, n_chunks))) for g in range(0, n_chunks, INTRA_GROUP)]
    pending = iter(())
    for grp in groups:
        for _ in within_chunks(grp):
            next(pending, None)
        for _ in pending:
            pass
        pending = state_chunks(grp)
    for _ in pending:
        pass

    o = jnp.concatenate([oacc_ref[h] for h in heads], axis=1)
    ms = _dot_exact_rhs(o * o, head_ones) * (1.0 / DN_HEAD_DIM)
    z = z_ref[...]
    o_ref[...] = (o * lax.rsqrt(ms + NORM_EPS) * dnw_ref[...] * (z * _sigmoid(z))).astype(o_ref.dtype)


def _deltanet_masks(lb):
    head = jnp.arange(DN_WIDTH) // DN_HEAD_DIM
    head_ones = head[:, None] == head[None, :]
    pos = jnp.arange(lb)
    same = (pos[:, None] // CHUNK) == (pos[None, :] // CHUNK)
    tril = same & (pos[None, :] <= pos[:, None])
    triu = same & (pos[:, None] <= pos[None, :])
    expand = jnp.arange(LANES)[:, None] == head[None, :]
    return tuple(m.astype(BF16) for m in (head_ones, tril, triu, same, expand))


def _deltanet(qkv, z, ba, bat, alog_r, dtb_r, alog_c, dtb_c, dnw, batch, seq):
    lb = DN_LB
    nb = seq // lb
    slots = lb // CHUNK * DN_HEADS
    masks = _deltanet_masks(lb)
    full = lambda a: pl.BlockSpec(a.shape, lambda b, j: (0,) * a.ndim)
    row = lambda n: pl.BlockSpec((lb, n), lambda b, j: (b * nb + j, 0))
    return pl.pallas_call(
        _deltanet_kernel,
        grid=(batch, nb),
        in_specs=[row(3 * DN_WIDTH), row(DN_WIDTH), row(2 * LANES),
                  pl.BlockSpec((2 * SUBLANES, lb), lambda b, j: (0, b * nb + j)),
                  full(alog_r), full(dtb_r), full(alog_c), full(dtb_c), full(dnw)] + [full(m) for m in masks],
        out_specs=row(DN_WIDTH),
        out_shape=jax.ShapeDtypeStruct((batch * seq, DN_WIDTH), BF16),
        scratch_shapes=[
            pltpu.VMEM((DN_HEADS, DN_HEAD_DIM, DN_HEAD_DIM), F32),
            pltpu.VMEM((lb, DN_WIDTH), BF16),
            pltpu.VMEM((lb, DN_WIDTH), BF16),
            pltpu.VMEM((lb, DN_WIDTH), BF16),
            pltpu.VMEM((lb, DN_WIDTH), BF16),
            pltpu.VMEM((lb, DN_WIDTH), BF16),
            pltpu.VMEM((lb, DN_WIDTH), BF16),
            pltpu.VMEM((lb, DN_WIDTH), F32),
            pltpu.VMEM((lb, DN_WIDTH), F32),
            pltpu.VMEM((lb, DN_WIDTH), F32),
            pltpu.VMEM((lb // CHUNK, DN_HEADS, CHUNK), F32),
            pltpu.VMEM((DN_HEADS, lb, DN_HEAD_DIM), F32),
            pltpu.VMEM((slots, CHUNK, CHUNK), BF16),
            pltpu.VMEM((slots, 2 * CHUNK, DN_HEAD_DIM), BF16),
            pltpu.VMEM((slots, CHUNK, DN_HEAD_DIM), F32),
            pltpu.VMEM((slots, DN_HEAD_DIM, CHUNK), BF16),
            pltpu.VMEM((slots, DN_HEAD_DIM, DN_HEAD_DIM), F32),
        ],
        compiler_params=pltpu.CompilerParams(dimension_semantics=("arbitrary", "arbitrary"),
                                             vmem_limit_bytes=VMEM_LIMIT),
        name="deltanet",
    )(qkv, z, ba, bat, alog_r, dtb_r, alog_c, dtb_c, dnw, *masks)


def _mix_kernel(x_ref, on_ref, u_ref, gates_ref, wupa_ref, poolw_ref, pscale_ref, wupb_ref, wout_ref,
                gffn_ref, wr_hi_ref, wr_lo_ref, br_ref,
                x1_ref, h2_ref, route_ref, counts_ref, ucarry_ref, ecount_ref, logits_ref, *, steps_per_seq):
    i = pl.program_id(0)
    tm = x_ref.shape[0]

    @pl.when(i == 0)
    def _():
        ecount_ref[...] = jnp.zeros_like(ecount_ref)
        logits_ref[...] = jnp.zeros_like(logits_ref)

    @pl.when(i % steps_per_seq == 0)
    def _():
        ucarry_ref[...] = jnp.zeros_like(ucarry_ref)

    lane = lax.broadcasted_iota(jnp.int32, (tm, LANES), 1)
    lg = jnp.where(lane < N_EXPERTS, logits_ref[...], -jnp.inf)
    vals, idxs, sels = [], [], []
    for _ in range(TOP_K):
        m = jnp.max(lg, axis=-1, keepdims=True)
        idx = jnp.min(jnp.where(lg == m, lane, LANES), axis=-1, keepdims=True)
        sel = lane == idx
        vals.append(m)
        idxs.append(idx)
        sels.append(sel)
        lg = jnp.where(sel, -jnp.inf, lg)
    es = [jnp.exp(vk - vals[0]) for vk in vals]
    denom = es[0] + es[1] + es[2] + es[3]
    chosen = jnp.zeros((tm, LANES), F32)
    for sel in sels:
        chosen = chosen + jnp.where(sel, 1.0, 0.0)
    chosen = jnp.where(i > 0, chosen, 0.0)

    u = u_ref[...]
    ext = jnp.concatenate([ucarry_ref[...], u], axis=0)
    ucarry_ref[...] = u[tm - POOL_HALO:tm]
    t_pos = (i % steps_per_seq) * tm + lax.broadcasted_iota(jnp.int32, (tm, POOL_GROUP_DIM), 0)
    ys = []
    for g in range(POOL_GROUPS):
        s = ext[:, g * POOL_GROUP_DIM:(g + 1) * POOL_GROUP_DIM]
        shift = 1
        while shift < POOL_WINDOWS[g]:
            s = s + pltpu.roll(s, shift, 0)
            shift *= 2
        count = jnp.minimum(t_pos + 1, POOL_WINDOWS[g]).astype(F32)
        pooled = s[POOL_HALO:] / count - u[:, g * POOL_GROUP_DIM:(g + 1) * POOL_GROUP_DIM]
        ys.append(_dot(pooled.astype(BF16), poolw_ref[g]))
    yb = jnp.concatenate(ys, axis=-1) * pscale_ref[...]
    y_b = _dot(yb.astype(BF16), wupb_ref[...])
    y_a = _dot(on_ref[...], wupa_ref[...])
    gates = gates_ref[...]
    merged = _sigmoid(gates[:, 0:D_MODEL]) * y_a + _sigmoid(gates[:, D_MODEL:2 * D_MODEL]) * y_b
    x1 = x_ref[...] + _dot(merged.astype(BF16), wout_ref[...])
    x1_ref[...] = x1

    h2 = _rms(x1, gffn_ref[...])
    _store_row_tiles(h2_ref, h2)

    hi, lo = _split2(h2)
    logits_ref[...] = (_dot(hi, wr_hi_ref[...]) + _dot(lo, wr_hi_ref[...]) + _dot(hi, wr_lo_ref[...])
                       + br_ref[...])

    ri = lax.broadcasted_iota(jnp.int32, (tm, tm), 0)
    rj = lax.broadcasted_iota(jnp.int32, (tm, tm), 1)
    before = (rj < ri).astype(BF16)
    pos = ecount_ref[...] + _dot(before, chosen.astype(BF16))
    ecount_ref[...] = ecount_ref[...] + jnp.sum(chosen, axis=0, keepdims=True)
    counts_ref[...] = ecount_ref[...]

    route = jnp.zeros((tm, LANES), F32)
    for kk in range(TOP_K):
        rank = jnp.sum(jnp.where(sels[kk], pos, 0.0), axis=-1, keepdims=True)
        route = jnp.where(lane == kk, rank, route)
        route = jnp.where(lane == TOP_K + kk, idxs[kk].astype(F32), route)
        route = jnp.where(lane == 2 * TOP_K + kk, es[kk] / denom, route)
    route_ref[...] = route


def _mix(x2, on, u, gates, wupa, poolw, pscale, wupb, wout, gffn, wr_hi, wr_lo, br, batch, seq):
    tm = MIX_TM
    t = batch * seq
    n = t // tm
    full = lambda a: pl.BlockSpec(a.shape, lambda i: (0,) * a.ndim)
    row = lambda c: pl.BlockSpec((tm, c), lambda i: (jnp.minimum(i, n - 1), 0))
    return pl.pallas_call(
        functools.partial(_mix_kernel, steps_per_seq=seq // tm),
        grid=(n + 1,),
        in_specs=[row(D_MODEL), row(DN_WIDTH), row(POOL_WIDTH), row(2 * D_MODEL),
                  full(wupa), full(poolw), full(pscale), full(wupb), full(wout), full(gffn),
                  full(wr_hi), full(wr_lo), full(br)],
        out_specs=[pl.BlockSpec((tm, D_MODEL), lambda i: (i, 0)),
                   pl.BlockSpec((tm * ROW_SUB, LANES), lambda i: (i, 0)),
                   pl.BlockSpec((tm, LANES), lambda i: (jnp.maximum(i - 1, 0), 0)),
                   pl.BlockSpec((1, LANES), lambda i: (0, 0))],
        out_shape=[jax.ShapeDtypeStruct((t + tm, D_MODEL), F32),
                   jax.ShapeDtypeStruct(((t + tm) * ROW_SUB, LANES), F32),
                   jax.ShapeDtypeStruct((t, LANES), F32),
                   jax.ShapeDtypeStruct((1, LANES), F32)],
        scratch_shapes=[pltpu.VMEM((POOL_HALO, POOL_WIDTH), F32), pltpu.VMEM((1, LANES), F32),
                        pltpu.VMEM((tm, LANES), F32)],
        compiler_params=pltpu.CompilerParams(dimension_semantics=("arbitrary",), vmem_limit_bytes=VMEM_LIMIT),
        name="mix",
    )(x2, on, u, gates, wupa, poolw, pscale, wupb, wout, gffn, wr_hi, wr_lo, br)


def _routing_tables(route, counts, tg, n_tiles):
    cnt = counts[0, 0:N_EXPERTS].astype(jnp.int32)
    padded = (cnt + (tg - 1)) // tg * tg
    ends = jnp.cumsum(padded)
    offs = ends - padded
    rank = route[:, 0:TOP_K].astype(jnp.int32)
    eidx = route[:, TOP_K:2 * TOP_K].astype(jnp.int32)
    experts = jnp.arange(N_EXPERTS, dtype=jnp.int32)
    dest = jnp.sum(jnp.where(eidx[..., None] == experts, offs, 0), axis=-1) + rank
    n_used = ends[N_EXPERTS - 1] // tg
    tile_start = jnp.arange(n_tiles, dtype=jnp.int32) * tg
    tile_expert = jnp.sum((tile_start[:, None] >= ends[None, :]).astype(jnp.int32), axis=1)
    tile_expert = jnp.minimum(tile_expert, N_EXPERTS - 1)
    last = tile_expert[jnp.maximum(n_used - 1, 0)]
    tile_expert = jnp.where(jnp.arange(n_tiles) < n_used, tile_expert, last)
    pad_start = offs + cnt
    pad_len = padded - cnt
    later = (experts[None, :] > experts[:, None]) & (cnt[None, :] > 0)
    next_expert = jnp.min(jnp.where(later, experts[None, :], N_EXPERTS), axis=1)
    next_expert = jnp.where(next_expert == N_EXPERTS, experts, next_expert)
    return (dest, tile_expert.astype(jnp.int32), next_expert.astype(jnp.int32),
            n_used.reshape(1).astype(jnp.int32), pad_start.astype(jnp.int32), pad_len.astype(jnp.int32))


def _dispatch_kernel(pad_start_ref, pad_len_ref, nu_ref, dest_ref, h2_ref, xs_ref, zero_ref, sem, zsem):
    tm = h2_ref.shape[0] // ROW_SUB
    tg = zero_ref.shape[0]
    n_tiles = xs_ref.shape[0] // tg

    @pl.when(pl.program_id(0) == 0)
    def _():
        zero_ref[...] = jnp.zeros_like(zero_ref)

        def pad_runs(e, act):
            first, n = pad_start_ref[e], pad_len_ref[e]
            for bit in reversed(range((tg // ROW_SUB).bit_length() - 1)):
                rows = (1 << bit) * ROW_SUB

                @pl.when((n >> bit) & 1 == 1)
                def _():
                    start = pl.multiple_of((first + ((n >> (bit + 1)) << (bit + 1))) * ROW_SUB, ROW_SUB)
                    act(pltpu.make_async_copy(zero_ref.at[pl.ds(0, rows)], xs_ref.at[pl.ds(start, rows)], zsem))

        def tile_copy(i):
            return pltpu.make_async_copy(zero_ref, xs_ref.at[pl.ds(pl.multiple_of(i * tg, tg), tg)], zsem)

        lax.fori_loop(0, N_EXPERTS, lambda e, cc: (pad_runs(e, lambda cp: cp.start()), cc)[1], 0)
        lax.fori_loop(0, N_EXPERTS, lambda e, cc: (pad_runs(e, lambda cp: cp.wait()), cc)[1], 0)
        lax.fori_loop(nu_ref[0], n_tiles, lambda i, cc: (tile_copy(i).start(), cc)[1], 0)
        lax.fori_loop(nu_ref[0], n_tiles, lambda i, cc: (tile_copy(0).wait(), cc)[1], 0)

    def row_copy(r, d):
        return pltpu.make_async_copy(_row_tile(h2_ref, r), _row_tile(xs_ref, d), sem)

    def start(r, c):
        for kk in range(TOP_K):
            row_copy(r, dest_ref[0, 0, r * TOP_K + kk]).start(priority=kk % 2)
        return c

    lax.fori_loop(0, tm, start, 0, unroll=DMA_UNROLL)
    for kk in range(TOP_K):
        pltpu.make_async_copy(h2_ref, xs_ref.at[pl.ds(0, tm * ROW_SUB)], sem).wait()


def _dispatch(pad_start, pad_len, n_used, dest3, h2, n_rows):
    t = dest3.shape[0] * dest3.shape[2] // TOP_K
    tm = DSP_TM
    return pl.pallas_call(
        _dispatch_kernel,
        grid_spec=pltpu.PrefetchScalarGridSpec(
            num_scalar_prefetch=3,
            grid=(t // tm,),
            in_specs=[pl.BlockSpec((1, 1, tm * TOP_K), lambda i, ps, pn, nu: (i, 0, 0), memory_space=pltpu.SMEM),
                      pl.BlockSpec((tm * ROW_SUB, LANES), lambda i, ps, pn, nu: (i, 0))],
            out_specs=pl.BlockSpec(memory_space=pl.ANY),
            scratch_shapes=[pltpu.VMEM((MOE_TG * ROW_SUB, LANES), F32), pltpu.SemaphoreType.DMA,
                            pltpu.SemaphoreType.DMA]),
        out_shape=jax.ShapeDtypeStruct((n_rows * ROW_SUB, LANES), F32),
        compiler_params=pltpu.CompilerParams(dimension_semantics=("arbitrary",), vmem_limit_bytes=VMEM_LIMIT),
        name="dispatch",
    )(pad_start, pad_len, n_used, dest3, h2)


def _experts_kernel(te_ref, nx_ref, nu_ref, xs_ref, wgu_hbm_ref, bgu_ref, wd_hbm_ref, bd_ref, ys_ref,
                    wgu_f32_ref, wd_f32_ref, wgu_bf_ref, wd_bf_ref, sem):
    i = pl.program_id(0)
    e = te_ref[i]

    def weight_copies(ex):
        return (pltpu.make_async_copy(wgu_hbm_ref.at[ex], wgu_f32_ref, sem.at[0]),
                pltpu.make_async_copy(wd_hbm_ref.at[ex], wd_f32_ref, sem.at[1]))

    @pl.when(i == 0)
    def _():
        for cp in weight_copies(e):
            cp.start()

    @pl.when((i == 0) | (e != te_ref[jnp.maximum(i - 1, 0)]))
    def _():
        for cp in weight_copies(e):
            cp.wait()
        wgu_bf_ref[...] = wgu_f32_ref[...].astype(BF16)
        wd_bf_ref[...] = wd_f32_ref[...].astype(BF16)

        @pl.when(nx_ref[e] != e)
        def _():
            for cp in weight_copies(nx_ref[e]):
                cp.start()

    @pl.when(i < nu_ref[0])
    def _():
        gu = _dot(_load_row_tiles(xs_ref).astype(BF16), wgu_bf_ref[...]) + bgu_ref[...]
        gate = jnp.minimum(gu[:, 0:D_FF], SWIGLU_LIMIT)
        up = jnp.clip(gu[:, D_FF:2 * D_FF], -SWIGLU_LIMIT, SWIGLU_LIMIT)
        act = gate * _sigmoid(SWIGLU_ALPHA * gate) * (up + 1.0)
        _store_row_tiles(ys_ref, _dot(act.astype(BF16), wd_bf_ref[...]) + bd_ref[...])

    @pl.when(i >= nu_ref[0])
    def _():
        ys_ref[...] = jnp.zeros_like(ys_ref)


def _experts(tile_expert, next_expert, n_used, xs, wgu, bgu, wd, bd):
    tg = MOE_TG * ROW_SUB
    n_tiles = xs.shape[0] // tg
    tile = lambda i, te, nx, nu: (jnp.minimum(i, nu[0] - 1), 0)
    expert = lambda i, te, nx, nu: (te[i], 0, 0)
    return pl.pallas_call(
        _experts_kernel,
        grid_spec=pltpu.PrefetchScalarGridSpec(
            num_scalar_prefetch=3,
            grid=(n_tiles,),
            in_specs=[pl.BlockSpec((tg, LANES), tile),
                      pl.BlockSpec(memory_space=pl.ANY),
                      pl.BlockSpec((None, 1, 2 * D_FF), expert),
                      pl.BlockSpec(memory_space=pl.ANY),
                      pl.BlockSpec((None, 1, D_MODEL), expert)],
            out_specs=pl.BlockSpec((tg, LANES), lambda i, te, nx, nu: (i, 0)),
            scratch_shapes=[pltpu.VMEM((D_MODEL, 2 * D_FF), F32), pltpu.VMEM((D_FF, D_MODEL), F32),
                            pltpu.VMEM((D_MODEL, 2 * D_FF), BF16), pltpu.VMEM((D_FF, D_MODEL), BF16),
                            pltpu.SemaphoreType.DMA((2,))]),
        out_shape=jax.ShapeDtypeStruct(xs.shape, F32),
        compiler_params=pltpu.CompilerParams(dimension_semantics=("arbitrary",), vmem_limit_bytes=VMEM_LIMIT),
        name="experts",
    )(tile_expert, next_expert, n_used, xs, wgu, bgu, wd, bd)


def _combine_kernel(dest_ref, dest_next_ref, route_ref, x1_ref, gfin_ref, ys_ref, out_ref, ybuf_ref, sem):
    i = pl.program_id(0)
    tm = x1_ref.shape[0]

    def gather_rows(dref, buf):
        def start(r, c):
            for kk in range(TOP_K):
                pltpu.make_async_copy(_row_tile(ys_ref, dref[0, 0, r * TOP_K + kk]),
                                      _row_tile(ybuf_ref.at[buf, kk], r), sem.at[buf]).start(priority=kk % 2)
            return c
        lax.fori_loop(0, tm, start, 0, unroll=DMA_UNROLL)

    @pl.when(i == 0)
    def _():
        gather_rows(dest_ref, 0)

    @pl.when(i + 1 < pl.num_programs(0))
    def _():
        gather_rows(dest_next_ref, (i + 1) % 2)

    buf = i % 2
    for kk in range(TOP_K):
        pltpu.make_async_copy(ys_ref.at[pl.ds(0, tm * ROW_SUB)], ybuf_ref.at[buf, kk], sem.at[buf]).wait()
    route = route_ref[...]
    acc = x1_ref[...]
    for kk in range(TOP_K):
        acc = acc + route[:, 2 * TOP_K + kk:2 * TOP_K + kk + 1] * _load_row_tiles(ybuf_ref.at[buf, kk])
    out_ref[...] = _rms(acc, gfin_ref[...])


def _combine(dest3, route, x1, gfin, ys):
    t = route.shape[0]
    tm = CMB_TM
    last = t // tm - 1
    return pl.pallas_call(
        _combine_kernel,
        grid=(t // tm,),
        in_specs=[pl.BlockSpec((1, 1, tm * TOP_K), lambda i: (i, 0, 0), memory_space=pltpu.SMEM),
                  pl.BlockSpec((1, 1, tm * TOP_K), lambda i: (jnp.minimum(i + 1, last), 0, 0),
                               memory_space=pltpu.SMEM),
                  pl.BlockSpec((tm, LANES), lambda i: (i, 0)),
                  pl.BlockSpec((tm, D_MODEL), lambda i: (i, 0)),
                  pl.BlockSpec((1, D_MODEL), lambda i: (0, 0)),
                  pl.BlockSpec(memory_space=pl.ANY)],
        out_specs=pl.BlockSpec((tm, D_MODEL), lambda i: (i, 0)),
        out_shape=jax.ShapeDtypeStruct((t, D_MODEL), F32),
        scratch_shapes=[pltpu.VMEM((2, TOP_K, tm * ROW_SUB, LANES), F32), pltpu.SemaphoreType.DMA((2,))],
        compiler_params=pltpu.CompilerParams(dimension_semantics=("arbitrary",), vmem_limit_bytes=VMEM_LIMIT),
        name="combine",
    )(dest3, dest3, route, x1, gfin, ys)


def kernel(x, g_mix, w_in, conv_w, a_log, dt_bias, dn_norm, w_up_a, pool_w, pool_scale, w_up_b, w_out, g_ffn,
           w_router, b_router, w_gate_up, b_gate_up, w_down, b_down, g_final):
    batch, seq, d = x.shape
    assert d == D_MODEL and seq % DN_LB == 0 and seq % MIX_TM == 0
    assert (batch * seq) % DSP_TM == 0 and (batch * seq * TOP_K) % MOE_TG == 0
    assert g_mix.shape[0] == 1, "one layer"
    t = batch * seq
    x2 = x.reshape(t, d)

    w = w_in[0]
    o_z = 3 * DN_WIDTH
    o_b = o_z + DN_WIDTH
    o_a = o_b + DN_HEADS
    o_u = o_a + DN_HEADS
    o_g = o_u + POOL_WIDTH
    wqkv = w[:, 0:o_z].astype(BF16)
    wz = w[:, o_z:o_b].astype(BF16)
    wu = w[:, o_u:o_g].astype(BF16)
    wg = w[:, o_g:].astype(BF16)
    w_b = w[:, o_b:o_a]
    w_a = w[:, o_a:o_u]
    pad = jnp.zeros((d, LANES - DN_HEADS), F32)
    wba = jnp.concatenate([w_b, pad, w_a, pad], axis=1).astype(BF16)
    wbat = jnp.concatenate([w_b, w_a], axis=1).T.astype(BF16)

    qkv, z, u, gates, ba, bat = _in_proj(x2, g_mix, wqkv, wz, wu, wg, wba, wbat, conv_w[0], seq)

    lane_pad = lambda p: jnp.pad(p.reshape(1, DN_HEADS), ((0, 0), (0, LANES - DN_HEADS)))
    on = _deltanet(qkv, z, ba, bat, lane_pad(a_log[0]), lane_pad(dt_bias[0]),
                   a_log[0].reshape(DN_HEADS, 1), dt_bias[0].reshape(DN_HEADS, 1),
                   jnp.tile(dn_norm[0], DN_HEADS).reshape(1, DN_WIDTH), batch, seq)

    wr = jnp.pad(w_router[0], ((0, 0), (0, LANES - N_EXPERTS)))
    wr_hi = wr.astype(BF16)
    wr_lo = (wr - wr_hi.astype(F32)).astype(BF16)
    br = jnp.pad(b_router[0].reshape(1, N_EXPERTS), ((0, 0), (0, LANES - N_EXPERTS)))
    x1, h2, route, counts = _mix(x2, on, u, gates, w_up_a[0].astype(BF16), pool_w[0].astype(BF16),
                                  pool_scale[0].reshape(1, POOL_WIDTH), w_up_b[0].astype(BF16),
                                  w_out[0].astype(BF16), g_ffn, wr_hi, wr_lo, br, batch, seq)

    n_tiles = t * TOP_K // MOE_TG + N_EXPERTS
    dest, tile_expert, next_expert, n_used, pad_start, pad_len = _routing_tables(route, counts, MOE_TG, n_tiles)
    xs = _dispatch(pad_start, pad_len, n_used, dest.reshape(t // DSP_TM, 1, DSP_TM * TOP_K), h2, n_tiles * MOE_TG)
    ys = _experts(tile_expert, next_expert, n_used, xs, w_gate_up[0],
                  b_gate_up[0].reshape(N_EXPERTS, 1, 2 * D_FF), w_down[0], b_down[0].reshape(N_EXPERTS, 1, D_MODEL))
    out = _combine(dest.reshape(t // CMB_TM, 1, CMB_TM * TOP_K), route, x1, g_final.reshape(1, D_MODEL), ys)
    return out.reshape(batch, seq, d)
```

```python
import functools

import jax
import jax.numpy as jnp
from jax import lax
from jax.experimental import pallas as pl
from jax.experimental.pallas import tpu as pltpu

F32 = jnp.float32
BF16 = jnp.bfloat16

D_MODEL = 1024
CHUNK = 64
DN_HEADS = 8
DN_HEAD_DIM = 64
DN_WIDTH = DN_HEADS * DN_HEAD_DIM
CONV_WIDTH = 4
POOL_GROUPS = 4
POOL_WINDOWS = (2, 4, 8, 16)
POOL_WIDTH = 512
POOL_GROUP_DIM = 128
POOL_HALO = 16
N_EXPERTS = 32
TOP_K = 4
D_FF = D_MODEL
SWIGLU_LIMIT = 7.0
SWIGLU_ALPHA = 1.702
NORM_EPS = 1e-6
LANES = 128
SUBLANES = 8
VMEM_LIMIT = 56 * 1024 * 1024

IN_TM = 256
DN_LB = 256
INTRA_GROUP = 2
MIX_TM = 256
MOE_TG = 512
DSP_TM = 1024
CMB_TM = 512
DMA_UNROLL = 8


def _dot(a, b):
    return jnp.dot(a, b, preferred_element_type=F32)


def _dot_nt(a, b):
    return lax.dot_general(a, b, (((1,), (1,)), ((), ())), preferred_element_type=F32)


def _split2(x):
    hi = x.astype(BF16)
    lo = (x - hi.astype(F32)).astype(BF16)
    return hi, lo


def _dot_exact_rhs(x, m):
    hi, lo = _split2(x)
    return _dot(hi, m) + _dot(lo, m)


def _dot_exact_lhs(m, x):
    hi, lo = _split2(x)
    return _dot(m, hi) + _dot(m, lo)


def _softplus(x):
    return jnp.maximum(x, 0.0) + jnp.log1p(jnp.exp(-jnp.abs(x)))


def _sigmoid(x):
    return 1.0 / (1.0 + jnp.exp(-x))


def _rms(x, g):
    return x * lax.rsqrt(jnp.mean(x * x, axis=-1, keepdims=True) + NORM_EPS) * g


ROW_SUB = D_MODEL // LANES


def _row_tile(ref, r):
    return ref.at[pl.ds(pl.multiple_of(r * ROW_SUB, ROW_SUB), ROW_SUB)]


def _store_row_tiles(ref, x):
    for j in range(ROW_SUB):
        ref[pl.ds(j, x.shape[0], stride=ROW_SUB), :] = x[:, j * LANES:(j + 1) * LANES]


def _load_row_tiles(ref):
    n = ref.shape[0] // ROW_SUB
    return jnp.concatenate([ref[pl.ds(j, n, stride=ROW_SUB), :] for j in range(ROW_SUB)], axis=1)


def _in_proj_kernel(x_ref, g_ref, wqkv_ref, wz_ref, wu_ref, wg_ref, wba_ref, wbat_ref, convw_ref,
                    qkv_ref, z_ref, u_ref, gates_ref, ba_ref, bat_ref, hist_ref, *, steps_per_seq):
    i = pl.program_id(0)
    tm = x_ref.shape[0]

    @pl.when(i == 0)
    def _():
        hist_ref[...] = jnp.zeros_like(hist_ref)

    @pl.when((i + steps_per_seq - 1) % steps_per_seq == 0)
    def _():
        hist_ref[0:SUBLANES, :] = jnp.zeros((SUBLANES, hist_ref.shape[1]), F32)

    cw = convw_ref[...]
    prev = hist_ref[SUBLANES:SUBLANES + tm, :]
    act = prev * cw[CONV_WIDTH - 1:CONV_WIDTH]
    for s in range(1, CONV_WIDTH):
        act = act + hist_ref[pl.ds(SUBLANES - s, tm), :] * cw[CONV_WIDTH - 1 - s:CONV_WIDTH - s]
    qkv_ref[...] = act * _sigmoid(act)

    hb = _rms(x_ref[...], g_ref[...]).astype(BF16)
    hist_ref[0:SUBLANES, :] = prev[tm - SUBLANES:tm]
    hist_ref[SUBLANES:SUBLANES + tm, :] = _dot(hb, wqkv_ref[...])
    z_ref[...] = _dot(hb, wz_ref[...])
    u_ref[...] = _dot(hb, wu_ref[...])
    gates_ref[...] = _dot(hb, wg_ref[...])
    ba_ref[...] = _dot(hb, wba_ref[...])
    bat_ref[...] = _dot_nt(wbat_ref[...], hb)


def _in_proj(x2, g_mix, wqkv, wz, wu, wg, wba, wbat, conv_w, seq):
    t = x2.shape[0]
    tm = IN_TM
    n = t // tm
    full = lambda a: pl.BlockSpec(a.shape, lambda i: (0, 0))
    row = lambda c: pl.BlockSpec((tm, c), lambda i: (jnp.minimum(i, n - 1), 0))
    return pl.pallas_call(
        functools.partial(_in_proj_kernel, steps_per_seq=seq // tm),
        grid=(n + 1,),
        in_specs=[row(D_MODEL), full(g_mix), full(wqkv), full(wz), full(wu), full(wg), full(wba), full(wbat),
                  full(conv_w)],
        out_specs=[pl.BlockSpec((tm, 3 * DN_WIDTH), lambda i: (jnp.maximum(i - 1, 0), 0)),
                   row(DN_WIDTH), row(POOL_WIDTH), row(2 * D_MODEL), row(2 * LANES),
                   pl.BlockSpec((2 * SUBLANES, tm), lambda i: (0, jnp.minimum(i, n - 1)))],
        out_shape=[jax.ShapeDtypeStruct((t, 3 * DN_WIDTH), F32),
                   jax.ShapeDtypeStruct((t, DN_WIDTH), F32),
                   jax.ShapeDtypeStruct((t, POOL_WIDTH), F32),
                   jax.ShapeDtypeStruct((t, 2 * D_MODEL), F32),
                   jax.ShapeDtypeStruct((t, 2 * LANES), F32),
                   jax.ShapeDtypeStruct((2 * SUBLANES, t), F32)],
        scratch_shapes=[pltpu.VMEM((SUBLANES + tm, 3 * DN_WIDTH), F32)],
        compiler_params=pltpu.CompilerParams(dimension_semantics=("arbitrary",), vmem_limit_bytes=VMEM_LIMIT),
        name="in_proj",
    )(x2, g_mix, wqkv, wz, wu, wg, wba, wbat, conv_w)


def _deltanet_kernel(qkv_ref, z_ref, ba_ref, bat_ref, alog_r_ref, dtb_r_ref, alog_c_ref, dtb_c_ref, dnw_ref,
                     head_ones_ref, tril_ref, triu_ref, chunk_ones_ref, expand_ref, o_ref,
                     s_ref, qn_ref, kn_ref, kbe_ref, vb_ref, qg_ref, kd_ref,
                     xbeta_ref, xgc_ref, xgl_ref, gr_ref, oacc_ref, qkd_ref, wq_ref, u_ref, kdt_ref, gl_ref):
    lb = qkv_ref.shape[0]
    n_chunks = lb // CHUNK

    @pl.when(pl.program_id(1) == 0)
    def _():
        s_ref[...] = jnp.zeros_like(s_ref)

    q = qkv_ref[:, 0:DN_WIDTH]
    k = qkv_ref[:, DN_WIDTH:2 * DN_WIDTH]
    v = qkv_ref[:, 2 * DN_WIDTH:3 * DN_WIDTH]

    head_ones = head_ones_ref[...]
    qn = q * lax.rsqrt(_dot_exact_rhs(q * q, head_ones) + NORM_EPS) * (DN_HEAD_DIM ** -0.5)
    kn = k * lax.rsqrt(_dot_exact_rhs(k * k, head_ones) + NORM_EPS)

    ba = ba_ref[...]
    beta_c = _sigmoid(ba[:, 0:LANES])
    g_c = -jnp.exp(alog_r_ref[...]) * _softplus(ba[:, LANES:2 * LANES] + dtb_r_ref[...])
    lane = lax.broadcasted_iota(jnp.int32, (lb, LANES), 1)
    g_c = jnp.where(lane < DN_HEADS, g_c, 0.0)
    bat = bat_ref[...]
    g_r = -jnp.exp(alog_c_ref[...]) * _softplus(bat[SUBLANES:2 * SUBLANES] + dtb_c_ref[...])

    gc = _dot_exact_lhs(tril_ref[...], g_c)
    gtot = _dot_exact_lhs(chunk_ones_ref[...], g_c)
    gr = _dot_exact_rhs(g_r, triu_ref[...])
    for c in range(n_chunks):
        gr_ref[c] = gr[:, c * CHUNK:(c + 1) * CHUNK]

    expand = expand_ref[...]
    x_beta = _dot_exact_rhs(beta_c, expand)
    x_gc = _dot_exact_rhs(gc, expand)
    x_gtot = _dot_exact_rhs(gtot, expand)
    x_eg = jnp.exp(x_gc)
    xbeta_ref[...] = x_beta
    xgc_ref[...] = x_gc
    xgl_ref[...] = jnp.exp(x_gtot)
    qn_ref[...] = qn.astype(BF16)
    kn_ref[...] = kn.astype(BF16)
    kbe_ref[...] = (kn * (x_beta * x_eg)).astype(BF16)
    vb_ref[...] = (v * x_beta).astype(BF16)
    qg_ref[...] = (qn * x_eg).astype(BF16)
    kd_ref[...] = (kn * jnp.exp(x_gtot - x_gc)).astype(BF16)

    ci = lax.broadcasted_iota(jnp.int32, (CHUNK, CHUNK), 0)
    cj = lax.broadcasted_iota(jnp.int32, (CHUNK, CHUNK), 1)
    eye = (ci == cj).astype(F32)

    heads = range(DN_HEADS)
    lanes = [pl.ds(h * DN_HEAD_DIM, DN_HEAD_DIM) for h in heads]
    stack = lambda top, bot: jnp.concatenate([top, bot], axis=0)

    chunk_rows = lambda c: pl.ds(c * CHUNK, CHUNK)

    def within_chunks(chunk_ids):
        chains = [(c, h) for c in chunk_ids for h in heads]
        ids = range(len(chains))
        rows = [chunk_rows(c) for c, _ in chains]
        ln = [lanes[h] for _, h in chains]
        kb = [kn_ref[rows[i], ln[i]] for i in ids]
        kq = [_dot_nt(stack(kb[i], qn_ref[rows[i], ln[i]]), kb[i]) for i in ids]
        yield
        decay = [jnp.exp(jnp.where(ci >= cj, xgc_ref[rows[i], ln[i]] - gr_ref[c, h:h + 1, :], -jnp.inf))
                 for i, (c, h) in enumerate(chains)]
        a = [jnp.where(ci > cj, xbeta_ref[rows[i], ln[i]] * kq[i][0:CHUNK] * decay[i], 0.0) for i in ids]
        slot = [c * DN_HEADS + h for c, h in chains]
        for i in ids:
            qkd_ref[slot[i]] = (kq[i][CHUNK:2 * CHUNK] * decay[i]).astype(BF16)
            wq_ref[slot[i], CHUNK:2 * CHUNK, :] = qg_ref[rows[i], ln[i]]
            kdt_ref[slot[i]] = kd_ref[rows[i], ln[i]].T.astype(BF16)
            gl_ref[slot[i]] = xgl_ref[rows[i], ln[i]]
        t_inv = [eye - a[i] for i in ids]
        pw = [a[i].astype(BF16) for i in ids]
        pw = [_dot(pw[i], pw[i]).astype(BF16) for i in ids]
        yield
        for _ in range(4):
            r = [_dot(stack(t_inv[i].astype(BF16), pw[i]), pw[i]) for i in ids]
            yield
            t_inv = [t_inv[i] + r[i][0:CHUNK] for i in ids]
            pw = [r[i][CHUNK:2 * CHUNK].astype(BF16) for i in ids]
        t_inv = [t_inv[i] + _dot(t_inv[i].astype(BF16), pw[i]) for i in ids]
        yield
        tb = [t_inv[i].astype(BF16) for i in ids]
        for i in ids:
            wq_ref[slot[i], 0:CHUNK, :] = _dot(tb[i], kbe_ref[rows[i], ln[i]]).astype(BF16)
        yield
        for i in ids:
            u_ref[slot[i]] = _dot(tb[i], vb_ref[rows[i], ln[i]])
        yield

    def state_chunks(chunk_ids):
        for c in chunk_ids:
            rows = chunk_rows(c)
            slot = [c * DN_HEADS + h for h in heads]
            s = [s_ref[h] for h in heads]
            ws = [_dot(wq_ref[slot[h]], s[h].astype(BF16)) for h in heads]
            yield
            vnb = [(u_ref[slot[h]] - ws[h][0:CHUNK]).astype(BF16) for h in heads]
            for h in heads:
                oacc_ref[h, rows, :] = ws[h][CHUNK:2 * CHUNK] + _dot(qkd_ref[slot[h]], vnb[h])
            yield
            for h in heads:
                s_ref[h] = s[h] * gl_ref[slot[h]] + _dot(kdt_ref[slot[h]], vnb[h])
            yield

    groups = [list(range(g, min(g + INTRA_GROUP, n_chunks))) for g in range(0, n_chunks, INTRA_GROUP)]
    pending = iter(())
    for grp in groups:
        for _ in within_chunks(grp):
            next(pending, None)
        for _ in pending:
            pass
        pending = state_chunks(grp)
    for _ in pending:
        pass

    o = jnp.concatenate([oacc_ref[h] for h in heads], axis=1)
    ms = _dot_exact_rhs(o * o, head_ones) * (1.0 / DN_HEAD_DIM)
    z = z_ref[...]
    o_ref[...] = (o * lax.rsqrt(ms + NORM_EPS) * dnw_ref[...] * (z * _sigmoid(z))).astype(o_ref.dtype)


def _deltanet_masks(lb):
    head = jnp.arange(DN_WIDTH) // DN_HEAD_DIM
    head_ones = head[:, None] == head[None, :]
    pos = jnp.arange(lb)
    same = (pos[:, None] // CHUNK) == (pos[None, :] // CHUNK)
    tril = same & (pos[None, :] <= pos[:, None])
    triu = same & (pos[:, None] <= pos[None, :])
    expand = jnp.arange(LANES)[:, None] == head[None, :]
    return tuple(m.astype(BF16) for m in (head_ones, tril, triu, same, expand))


def _deltanet(qkv, z, ba, bat, alog_r, dtb_r, alog_c, dtb_c, dnw, batch, seq):
    lb = DN_LB
    nb = seq // lb
    slots = lb // CHUNK * DN_HEADS
    masks = _deltanet_masks(lb)
    full = lambda a: pl.BlockSpec(a.shape, lambda b, j: (0,) * a.ndim)
    row = lambda n: pl.BlockSpec((lb, n), lambda b, j: (b * nb + j, 0))
    return pl.pallas_call(
        _deltanet_kernel,
        grid=(batch, nb),
        in_specs=[row(3 * DN_WIDTH), row(DN_WIDTH), row(2 * LANES),
                  pl.BlockSpec((2 * SUBLANES, lb), lambda b, j: (0, b * nb + j)),
                  full(alog_r), full(dtb_r), full(alog_c), full(dtb_c), full(dnw)] + [full(m) for m in masks],
        out_specs=row(DN_WIDTH),
        out_shape=jax.ShapeDtypeStruct((batch * seq, DN_WIDTH), BF16),
        scratch_shapes=[
            pltpu.VMEM((DN_HEADS, DN_HEAD_DIM, DN_HEAD_DIM), F32),
            pltpu.VMEM((lb, DN_WIDTH), BF16),
            pltpu.VMEM((lb, DN_WIDTH), BF16),
            pltpu.VMEM((lb, DN_WIDTH), BF16),
            pltpu.VMEM((lb, DN_WIDTH), BF16),
            pltpu.VMEM((lb, DN_WIDTH), BF16),
            pltpu.VMEM((lb, DN_WIDTH), BF16),
            pltpu.VMEM((lb, DN_WIDTH), F32),
            pltpu.VMEM((lb, DN_WIDTH), F32),
            pltpu.VMEM((lb, DN_WIDTH), F32),
            pltpu.VMEM((lb // CHUNK, DN_HEADS, CHUNK), F32),
            pltpu.VMEM((DN_HEADS, lb, DN_HEAD_DIM), F32),
            pltpu.VMEM((slots, CHUNK, CHUNK), BF16),
            pltpu.VMEM((slots, 2 * CHUNK, DN_HEAD_DIM), BF16),
            pltpu.VMEM((slots, CHUNK, DN_HEAD_DIM), F32),
            pltpu.VMEM((slots, DN_HEAD_DIM, CHUNK), BF16),
            pltpu.VMEM((slots, DN_HEAD_DIM, DN_HEAD_DIM), F32),
        ],
        compiler_params=pltpu.CompilerParams(dimension_semantics=("arbitrary", "arbitrary"),
                                             vmem_limit_bytes=VMEM_LIMIT),
        name="deltanet",
    )(qkv, z, ba, bat, alog_r, dtb_r, alog_c, dtb_c, dnw, *masks)


def _mix_kernel(x_ref, on_ref, u_ref, gates_ref, wupa_ref, poolw_ref, pscale_ref, wupb_ref, wout_ref,
                gffn_ref, wr_hi_ref, wr_lo_ref, br_ref,
                x1_ref, h2_ref, route_ref, rank_ref, expert_ref, counts_ref,
                ucarry_ref, ecount_ref, logits_ref, *, steps_per_seq):
    i = pl.program_id(0)
    tm = x_ref.shape[0]

    @pl.when(i == 0)
    def _():
        ecount_ref[...] = jnp.zeros_like(ecount_ref)
        logits_ref[...] = jnp.zeros_like(logits_ref)

    @pl.when(i % steps_per_seq == 0)
    def _():
        ucarry_ref[...] = jnp.zeros_like(ucarry_ref)

    lane = lax.broadcasted_iota(jnp.int32, (tm, LANES), 1)
    lg = jnp.where(lane < N_EXPERTS, logits_ref[...], -jnp.inf)
    vals, idxs, sels = [], [], []
    for _ in range(TOP_K):
        m = jnp.max(lg, axis=-1, keepdims=True)
        idx = jnp.min(jnp.where(lg == m, lane, LANES), axis=-1, keepdims=True)
        sel = lane == idx
        vals.append(m)
        idxs.append(idx)
        sels.append(sel)
        lg = jnp.where(sel, -jnp.inf, lg)
    es = [jnp.exp(vk - vals[0]) for vk in vals]
    denom = es[0] + es[1] + es[2] + es[3]
    chosen = jnp.zeros((tm, LANES), F32)
    for sel in sels:
        chosen = chosen + jnp.where(sel, 1.0, 0.0)
    chosen = jnp.where(i > 0, chosen, 0.0)

    u = u_ref[...]
    ext = jnp.concatenate([ucarry_ref[...], u], axis=0)
    ucarry_ref[...] = u[tm - POOL_HALO:tm]
    t_pos = (i % steps_per_seq) * tm + lax.broadcasted_iota(jnp.int32, (tm, POOL_GROUP_DIM), 0)
    ys = []
    for g in range(POOL_GROUPS):
        s = ext[:, g * POOL_GROUP_DIM:(g + 1) * POOL_GROUP_DIM]
        shift = 1
        while shift < POOL_WINDOWS[g]:
            s = s + pltpu.roll(s, shift, 0)
            shift *= 2
        count = jnp.minimum(t_pos + 1, POOL_WINDOWS[g]).astype(F32)
        pooled = s[POOL_HALO:] / count - u[:, g * POOL_GROUP_DIM:(g + 1) * POOL_GROUP_DIM]
        ys.append(_dot(pooled.astype(BF16), poolw_ref[g]))
    yb = jnp.concatenate(ys, axis=-1) * pscale_ref[...]
    y_b = _dot(yb.astype(BF16), wupb_ref[...])
    y_a = _dot(on_ref[...], wupa_ref[...])
    gates = gates_ref[...]
    merged = _sigmoid(gates[:, 0:D_MODEL]) * y_a + _sigmoid(gates[:, D_MODEL:2 * D_MODEL]) * y_b
    x1 = x_ref[...] + _dot(merged.astype(BF16), wout_ref[...])
    x1_ref[...] = x1

    h2 = _rms(x1, gffn_ref[...])
    _store_row_tiles(h2_ref, h2)

    hi, lo = _split2(h2)
    logits_ref[...] = (_dot(hi, wr_hi_ref[...]) + _dot(lo, wr_hi_ref[...]) + _dot(hi, wr_lo_ref[...])
                       + br_ref[...])

    ri = lax.broadcasted_iota(jnp.int32, (tm, tm), 0)
    rj = lax.broadcasted_iota(jnp.int32, (tm, tm), 1)
    before = (rj < ri).astype(BF16)
    pos = ecount_ref[...] + _dot(before, chosen.astype(BF16))
    ecount_ref[...] = ecount_ref[...] + jnp.sum(chosen, axis=0, keepdims=True)
    counts_ref[...] = ecount_ref[...]

    per_row = LANES // TOP_K
    first_lane = (lax.broadcasted_iota(jnp.int32, (tm, LANES), 0) % per_row) * TOP_K
    weights = jnp.zeros((tm, LANES), F32)
    flat_rank = jnp.zeros((tm, LANES), F32)
    flat_expert = jnp.zeros((tm, LANES), F32)
    for kk in range(TOP_K):
        rank = jnp.sum(jnp.where(sels[kk], pos, 0.0), axis=-1, keepdims=True)
        weights = jnp.where(lane == kk, es[kk] / denom, weights)
        flat_rank = jnp.where(lane == first_lane + kk, rank, flat_rank)
        flat_expert = jnp.where(lane == first_lane + kk, idxs[kk].astype(F32), flat_expert)
    route_ref[...] = weights
    fold = lambda a: jnp.sum(a.reshape(tm // per_row, per_row, LANES), axis=1).astype(jnp.int32)
    rank_ref[...] = fold(flat_rank)
    expert_ref[...] = fold(flat_expert)


def _mix(x2, on, u, gates, wupa, poolw, pscale, wupb, wout, gffn, wr_hi, wr_lo, br, batch, seq):
    tm = MIX_TM
    t = batch * seq
    n = t // tm
    full = lambda a: pl.BlockSpec(a.shape, lambda i: (0,) * a.ndim)
    row = lambda c: pl.BlockSpec((tm, c), lambda i: (jnp.minimum(i, n - 1), 0))
    return pl.pallas_call(
        functools.partial(_mix_kernel, steps_per_seq=seq // tm),
        grid=(n + 1,),
        in_specs=[row(D_MODEL), row(DN_WIDTH), row(POOL_WIDTH), row(2 * D_MODEL),
                  full(wupa), full(poolw), full(pscale), full(wupb), full(wout), full(gffn),
                  full(wr_hi), full(wr_lo), full(br)],
        out_specs=[pl.BlockSpec((tm, D_MODEL), lambda i: (i, 0)),
                   pl.BlockSpec((tm * ROW_SUB, LANES), lambda i: (i, 0)),
                   pl.BlockSpec((tm, LANES), lambda i: (jnp.maximum(i - 1, 0), 0)),
                   pl.BlockSpec((tm * TOP_K // LANES, LANES), lambda i: (jnp.maximum(i - 1, 0), 0)),
                   pl.BlockSpec((tm * TOP_K // LANES, LANES), lambda i: (jnp.maximum(i - 1, 0), 0)),
                   pl.BlockSpec((1, LANES), lambda i: (0, 0))],
        out_shape=[jax.ShapeDtypeStruct((t + tm, D_MODEL), F32),
                   jax.ShapeDtypeStruct(((t + tm) * ROW_SUB, LANES), F32),
                   jax.ShapeDtypeStruct((t, LANES), F32),
                   jax.ShapeDtypeStruct((t * TOP_K // LANES, LANES), jnp.int32),
                   jax.ShapeDtypeStruct((t * TOP_K // LANES, LANES), jnp.int32),
                   jax.ShapeDtypeStruct((1, LANES), F32)],
        scratch_shapes=[pltpu.VMEM((POOL_HALO, POOL_WIDTH), F32), pltpu.VMEM((1, LANES), F32),
                        pltpu.VMEM((tm, LANES), F32)],
        compiler_params=pltpu.CompilerParams(dimension_semantics=("arbitrary",), vmem_limit_bytes=VMEM_LIMIT),
        name="mix",
    )(x2, on, u, gates, wupa, poolw, pscale, wupb, wout, gffn, wr_hi, wr_lo, br)


def _routing_tables(rank, expert, counts, tg, n_tiles):
    cnt = counts[0, 0:N_EXPERTS].astype(jnp.int32)
    padded = (cnt + (tg - 1)) // tg * tg
    ends = jnp.cumsum(padded)
    offs = ends - padded
    experts = jnp.arange(N_EXPERTS, dtype=jnp.int32)
    dest = jnp.sum(jnp.where(expert[..., None] == experts, offs, 0), axis=-1) + rank
    n_used = ends[N_EXPERTS - 1] // tg
    tile_start = jnp.arange(n_tiles, dtype=jnp.int32) * tg
    tile_expert = jnp.sum((tile_start[:, None] >= ends[None, :]).astype(jnp.int32), axis=1)
    tile_expert = jnp.minimum(tile_expert, N_EXPERTS - 1)
    last = tile_expert[jnp.maximum(n_used - 1, 0)]
    tile_expert = jnp.where(jnp.arange(n_tiles) < n_used, tile_expert, last)
    pad_start = offs + cnt
    pad_len = padded - cnt
    later = (experts[None, :] > experts[:, None]) & (cnt[None, :] > 0)
    next_expert = jnp.min(jnp.where(later, experts[None, :], N_EXPERTS), axis=1)
    next_expert = jnp.where(next_expert == N_EXPERTS, experts, next_expert)
    return (dest, tile_expert.astype(jnp.int32), next_expert.astype(jnp.int32),
            n_used.reshape(1).astype(jnp.int32), pad_start.astype(jnp.int32), pad_len.astype(jnp.int32))


def _dispatch_kernel(pad_start_ref, pad_len_ref, nu_ref, dest_ref, h2_ref, xs_ref, zero_ref, sem, zsem):
    tm = h2_ref.shape[0] // ROW_SUB
    tg = zero_ref.shape[0]
    n_tiles = xs_ref.shape[0] // tg

    @pl.when(pl.program_id(0) == 0)
    def _():
        zero_ref[...] = jnp.zeros_like(zero_ref)

        def pad_runs(e, act):
            first, n = pad_start_ref[e], pad_len_ref[e]
            for bit in reversed(range((tg // ROW_SUB).bit_length() - 1)):
                rows = (1 << bit) * ROW_SUB

                @pl.when((n >> bit) & 1 == 1)
                def _():
                    start = pl.multiple_of((first + ((n >> (bit + 1)) << (bit + 1))) * ROW_SUB, ROW_SUB)
                    act(pltpu.make_async_copy(zero_ref.at[pl.ds(0, rows)], xs_ref.at[pl.ds(start, rows)], zsem))

        def tile_copy(i):
            return pltpu.make_async_copy(zero_ref, xs_ref.at[pl.ds(pl.multiple_of(i * tg, tg), tg)], zsem)

        lax.fori_loop(0, N_EXPERTS, lambda e, cc: (pad_runs(e, lambda cp: cp.start()), cc)[1], 0)
        lax.fori_loop(0, N_EXPERTS, lambda e, cc: (pad_runs(e, lambda cp: cp.wait()), cc)[1], 0)
        lax.fori_loop(nu_ref[0], n_tiles, lambda i, cc: (tile_copy(i).start(), cc)[1], 0)
        lax.fori_loop(nu_ref[0], n_tiles, lambda i, cc: (tile_copy(0).wait(), cc)[1], 0)

    def row_copy(r, d):
        return pltpu.make_async_copy(_row_tile(h2_ref, r), _row_tile(xs_ref, d), sem)

    def start(r, c):
        for kk in range(TOP_K):
            row_copy(r, dest_ref[0, 0, r * TOP_K + kk]).start(priority=kk % 2)
        return c

    lax.fori_loop(0, tm, start, 0, unroll=DMA_UNROLL)
    for kk in range(TOP_K):
        pltpu.make_async_copy(h2_ref, xs_ref.at[pl.ds(0, tm * ROW_SUB)], sem).wait()


def _dispatch(pad_start, pad_len, n_used, dest3, h2, n_rows):
    t = dest3.shape[0] * dest3.shape[2] // TOP_K
    tm = DSP_TM
    return pl.pallas_call(
        _dispatch_kernel,
        grid_spec=pltpu.PrefetchScalarGridSpec(
            num_scalar_prefetch=3,
            grid=(t // tm,),
            in_specs=[pl.BlockSpec((1, 1, tm * TOP_K), lambda i, ps, pn, nu: (i, 0, 0), memory_space=pltpu.SMEM),
                      pl.BlockSpec((tm * ROW_SUB, LANES), lambda i, ps, pn, nu: (i, 0))],
            out_specs=pl.BlockSpec(memory_space=pl.ANY),
            scratch_shapes=[pltpu.VMEM((MOE_TG * ROW_SUB, LANES), F32), pltpu.SemaphoreType.DMA,
                            pltpu.SemaphoreType.DMA]),
        out_shape=jax.ShapeDtypeStruct((n_rows * ROW_SUB, LANES), F32),
        compiler_params=pltpu.CompilerParams(dimension_semantics=("arbitrary",), vmem_limit_bytes=VMEM_LIMIT),
        name="dispatch",
    )(pad_start, pad_len, n_used, dest3, h2)


def _experts_kernel(te_ref, nx_ref, nu_ref, xs_ref, wgu_hbm_ref, bgu_ref, wd_hbm_ref, bd_ref, ys_ref,
                    wgu_f32_ref, wd_f32_ref, wgu_bf_ref, wd_bf_ref, sem):
    i = pl.program_id(0)
    e = te_ref[i]

    def weight_copies(ex):
        return (pltpu.make_async_copy(wgu_hbm_ref.at[ex], wgu_f32_ref, sem.at[0]),
                pltpu.make_async_copy(wd_hbm_ref.at[ex], wd_f32_ref, sem.at[1]))

    @pl.when(i == 0)
    def _():
        for cp in weight_copies(e):
            cp.start()

    @pl.when((i == 0) | (e != te_ref[jnp.maximum(i - 1, 0)]))
    def _():
        for cp in weight_copies(e):
            cp.wait()
        wgu_bf_ref[...] = wgu_f32_ref[...].astype(BF16)
        wd_bf_ref[...] = wd_f32_ref[...].astype(BF16)

        @pl.when(nx_ref[e] != e)
        def _():
            for cp in weight_copies(nx_ref[e]):
                cp.start()

    @pl.when(i < nu_ref[0])
    def _():
        gu = _dot(_load_row_tiles(xs_ref).astype(BF16), wgu_bf_ref[...]) + bgu_ref[...]
        gate = jnp.minimum(gu[:, 0:D_FF], SWIGLU_LIMIT)
        up = jnp.clip(gu[:, D_FF:2 * D_FF], -SWIGLU_LIMIT, SWIGLU_LIMIT)
        act = gate * _sigmoid(SWIGLU_ALPHA * gate) * (up + 1.0)
        _store_row_tiles(ys_ref, _dot(act.astype(BF16), wd_bf_ref[...]) + bd_ref[...])

    @pl.when(i >= nu_ref[0])
    def _():
        ys_ref[...] = jnp.zeros_like(ys_ref)


def _experts(tile_expert, next_expert, n_used, xs, wgu, bgu, wd, bd):
    tg = MOE_TG * ROW_SUB
    n_tiles = xs.shape[0] // tg
    tile = lambda i, te, nx, nu: (jnp.minimum(i, nu[0] - 1), 0)
    expert = lambda i, te, nx, nu: (te[i], 0, 0)
    return pl.pallas_call(
        _experts_kernel,
        grid_spec=pltpu.PrefetchScalarGridSpec(
            num_scalar_prefetch=3,
            grid=(n_tiles,),
            in_specs=[pl.BlockSpec((tg, LANES), tile),
                      pl.BlockSpec(memory_space=pl.ANY),
                      pl.BlockSpec((None, 1, 2 * D_FF), expert),
                      pl.BlockSpec(memory_space=pl.ANY),
                      pl.BlockSpec((None, 1, D_MODEL), expert)],
            out_specs=pl.BlockSpec((tg, LANES), lambda i, te, nx, nu: (i, 0)),
            scratch_shapes=[pltpu.VMEM((D_MODEL, 2 * D_FF), F32), pltpu.VMEM((D_FF, D_MODEL), F32),
                            pltpu.VMEM((D_MODEL, 2 * D_FF), BF16), pltpu.VMEM((D_FF, D_MODEL), BF16),
                            pltpu.SemaphoreType.DMA((2,))]),
        out_shape=jax.ShapeDtypeStruct(xs.shape, F32),
        compiler_params=pltpu.CompilerParams(dimension_semantics=("arbitrary",), vmem_limit_bytes=VMEM_LIMIT),
        name="experts",
    )(tile_expert, next_expert, n_used, xs, wgu, bgu, wd, bd)


def _combine_kernel(dest_ref, dest_next_ref, route_ref, x1_ref, gfin_ref, ys_ref, out_ref, ybuf_ref, sem):
    i = pl.program_id(0)
    tm = x1_ref.shape[0]

    def gather_rows(dref, buf):
        def start(r, c):
            for kk in range(TOP_K):
                pltpu.make_async_copy(_row_tile(ys_ref, dref[0, 0, r * TOP_K + kk]),
                                      _row_tile(ybuf_ref.at[buf, kk], r), sem.at[buf]).start(priority=kk % 2)
            return c
        lax.fori_loop(0, tm, start, 0, unroll=DMA_UNROLL)

    @pl.when(i == 0)
    def _():
        gather_rows(dest_ref, 0)

    @pl.when(i + 1 < pl.num_programs(0))
    def _():
        gather_rows(dest_next_ref, (i + 1) % 2)

    buf = i % 2
    for kk in range(TOP_K):
        pltpu.make_async_copy(ys_ref.at[pl.ds(0, tm * ROW_SUB)], ybuf_ref.at[buf, kk], sem.at[buf]).wait()
    route = route_ref[...]
    acc = x1_ref[...]
    for kk in range(TOP_K):
        acc = acc + route[:, kk:kk + 1] * _load_row_tiles(ybuf_ref.at[buf, kk])
    out_ref[...] = _rms(acc, gfin_ref[...])


def _combine(dest3, route, x1, gfin, ys):
    t = route.shape[0]
    tm = CMB_TM
    last = t // tm - 1
    return pl.pallas_call(
        _combine_kernel,
        grid=(t // tm,),
        in_specs=[pl.BlockSpec((1, 1, tm * TOP_K), lambda i: (i, 0, 0), memory_space=pltpu.SMEM),
                  pl.BlockSpec((1, 1, tm * TOP_K), lambda i: (jnp.minimum(i + 1, last), 0, 0),
                               memory_space=pltpu.SMEM),
                  pl.BlockSpec((tm, LANES), lambda i: (i, 0)),
                  pl.BlockSpec((tm, D_MODEL), lambda i: (i, 0)),
                  pl.BlockSpec((1, D_MODEL), lambda i: (0, 0)),
                  pl.BlockSpec(memory_space=pl.ANY)],
        out_specs=pl.BlockSpec((tm, D_MODEL), lambda i: (i, 0)),
        out_shape=jax.ShapeDtypeStruct((t, D_MODEL), F32),
        scratch_shapes=[pltpu.VMEM((2, TOP_K, tm * ROW_SUB, LANES), F32), pltpu.SemaphoreType.DMA((2,))],
        compiler_params=pltpu.CompilerParams(dimension_semantics=("arbitrary",), vmem_limit_bytes=VMEM_LIMIT),
        name="combine",
    )(dest3, dest3, route, x1, gfin, ys)


def kernel(x, g_mix, w_in, conv_w, a_log, dt_bias, dn_norm, w_up_a, pool_w, pool_scale, w_up_b, w_out, g_ffn,
           w_router, b_router, w_gate_up, b_gate_up, w_down, b_down, g_final):
    batch, seq, d = x.shape
    assert d == D_MODEL and seq % DN_LB == 0 and seq % MIX_TM == 0
    assert (batch * seq) % DSP_TM == 0 and (batch * seq * TOP_K) % MOE_TG == 0
    assert g_mix.shape[0] == 1, "one layer"
    t = batch * seq
    x2 = x.reshape(t, d)

    w = w_in[0]
    o_z = 3 * DN_WIDTH
    o_b = o_z + DN_WIDTH
    o_a = o_b + DN_HEADS
    o_u = o_a + DN_HEADS
    o_g = o_u + POOL_WIDTH
    wqkv = w[:, 0:o_z].astype(BF16)
    wz = w[:, o_z:o_b].astype(BF16)
    wu = w[:, o_u:o_g].astype(BF16)
    wg = w[:, o_g:].astype(BF16)
    w_b = w[:, o_b:o_a]
    w_a = w[:, o_a:o_u]
    pad = jnp.zeros((d, LANES - DN_HEADS), F32)
    wba = jnp.concatenate([w_b, pad, w_a, pad], axis=1).astype(BF16)
    wbat = jnp.concatenate([w_b, w_a], axis=1).T.astype(BF16)

    qkv, z, u, gates, ba, bat = _in_proj(x2, g_mix, wqkv, wz, wu, wg, wba, wbat, conv_w[0], seq)

    lane_pad = lambda p: jnp.pad(p.reshape(1, DN_HEADS), ((0, 0), (0, LANES - DN_HEADS)))
    on = _deltanet(qkv, z, ba, bat, lane_pad(a_log[0]), lane_pad(dt_bias[0]),
                   a_log[0].reshape(DN_HEADS, 1), dt_bias[0].reshape(DN_HEADS, 1),
                   jnp.tile(dn_norm[0], DN_HEADS).reshape(1, DN_WIDTH), batch, seq)

    wr = jnp.pad(w_router[0], ((0, 0), (0, LANES - N_EXPERTS)))
    wr_hi = wr.astype(BF16)
    wr_lo = (wr - wr_hi.astype(F32)).astype(BF16)
    br = jnp.pad(b_router[0].reshape(1, N_EXPERTS), ((0, 0), (0, LANES - N_EXPERTS)))
    x1, h2, route, rank, expert, counts = _mix(x2, on, u, gates, w_up_a[0].astype(BF16), pool_w[0].astype(BF16),
                                                pool_scale[0].reshape(1, POOL_WIDTH), w_up_b[0].astype(BF16),
                                                w_out[0].astype(BF16), g_ffn, wr_hi, wr_lo, br, batch, seq)

    n_tiles = t * TOP_K // MOE_TG + N_EXPERTS
    dest, tile_expert, next_expert, n_used, pad_start, pad_len = _routing_tables(rank, expert, counts, MOE_TG,
                                                                                 n_tiles)
    xs = _dispatch(pad_start, pad_len, n_used, dest.reshape(t // DSP_TM, 1, DSP_TM * TOP_K), h2, n_tiles * MOE_TG)
    ys = _experts(tile_expert, next_expert, n_used, xs, w_gate_up[0],
                  b_gate_up[0].reshape(N_EXPERTS, 1, 2 * D_FF), w_down[0], b_down[0].reshape(N_EXPERTS, 1, D_MODEL))
    out = _combine(dest.reshape(t // CMB_TM, 1, CMB_TM * TOP_K), route, x1, g_final.reshape(1, D_MODEL), ys)
    return out.reshape(batch, seq, d)
```

```python
import functools

import jax
import jax.numpy as jnp
from jax import lax
from jax.experimental import pallas as pl
from jax.experimental.pallas import tpu as pltpu

F32 = jnp.float32
BF16 = jnp.bfloat16

D_MODEL = 1024
CHUNK = 64
DN_HEADS = 8
DN_HEAD_DIM = 64
DN_WIDTH = DN_HEADS * DN_HEAD_DIM
CONV_WIDTH = 4
POOL_GROUPS = 4
POOL_WINDOWS = (2, 4, 8, 16)
POOL_WIDTH = 512
POOL_GROUP_DIM = 128
POOL_HALO = 16
N_EXPERTS = 32
TOP_K = 4
D_FF = D_MODEL
SWIGLU_LIMIT = 7.0
SWIGLU_ALPHA = 1.702
NORM_EPS = 1e-6
LANES = 128
SUBLANES = 8
VMEM_LIMIT = 56 * 1024 * 1024

IN_TM = 512
DN_LB = 256
INTRA_GROUP = 2
MIX_TM = 512
MOE_TG = 512
DSP_TM = 1024
CMB_TM = 512
DMA_UNROLL = 8


def _dot(a, b):
    return jnp.dot(a, b, preferred_element_type=F32)


def _dot_nt(a, b):
    return lax.dot_general(a, b, (((1,), (1,)), ((), ())), preferred_element_type=F32)


def _split2(x):
    hi = x.astype(BF16)
    lo = (x - hi.astype(F32)).astype(BF16)
    return hi, lo


def _dot_exact_rhs(x, m):
    hi, lo = _split2(x)
    return _dot(hi, m) + _dot(lo, m)


def _dot_exact_lhs(m, x):
    hi, lo = _split2(x)
    return _dot(m, hi) + _dot(m, lo)


def _softplus(x):
    return jnp.maximum(x, 0.0) + jnp.log1p(jnp.exp(-jnp.abs(x)))


def _sigmoid(x):
    return 1.0 / (1.0 + jnp.exp(-x))


def _rms(x, g):
    return x * lax.rsqrt(jnp.mean(x * x, axis=-1, keepdims=True) + NORM_EPS) * g


ROW_SUB = D_MODEL // LANES


def _row_tile(ref, r):
    return ref.at[pl.ds(pl.multiple_of(r * ROW_SUB, ROW_SUB), ROW_SUB)]


def _store_row_tiles(ref, x):
    for j in range(ROW_SUB):
        ref[pl.ds(j, x.shape[0], stride=ROW_SUB), :] = x[:, j * LANES:(j + 1) * LANES]


def _load_row_tiles(ref):
    n = ref.shape[0] // ROW_SUB
    return jnp.concatenate([ref[pl.ds(j, n, stride=ROW_SUB), :] for j in range(ROW_SUB)], axis=1)


def _in_proj_kernel(x_ref, g_ref, wqkv_ref, wz_ref, wu_ref, wg_ref, wba_ref, wbat_ref, convw_ref,
                    qkv_ref, z_ref, u_ref, gates_ref, ba_ref, bat_ref, hist_ref, *, steps_per_seq):
    i = pl.program_id(0)
    tm = x_ref.shape[0]

    @pl.when(i == 0)
    def _():
        hist_ref[...] = jnp.zeros_like(hist_ref)

    @pl.when((i + steps_per_seq - 1) % steps_per_seq == 0)
    def _():
        hist_ref[0:SUBLANES, :] = jnp.zeros((SUBLANES, hist_ref.shape[1]), F32)

    cw = convw_ref[...]
    prev = hist_ref[SUBLANES:SUBLANES + tm, :]
    act = prev * cw[CONV_WIDTH - 1:CONV_WIDTH]
    for s in range(1, CONV_WIDTH):
        act = act + hist_ref[pl.ds(SUBLANES - s, tm), :] * cw[CONV_WIDTH - 1 - s:CONV_WIDTH - s]
    qkv_ref[...] = act * _sigmoid(act)

    hb = _rms(x_ref[...], g_ref[...]).astype(BF16)
    hist_ref[0:SUBLANES, :] = prev[tm - SUBLANES:tm]
    hist_ref[SUBLANES:SUBLANES + tm, :] = _dot(hb, wqkv_ref[...])
    z_ref[...] = _dot(hb, wz_ref[...])
    u_ref[...] = _dot(hb, wu_ref[...])
    gates_ref[...] = _dot(hb, wg_ref[...])
    ba_ref[...] = _dot(hb, wba_ref[...])
    bat_ref[...] = _dot_nt(wbat_ref[...], hb)


def _in_proj(x2, g_mix, wqkv, wz, wu, wg, wba, wbat, conv_w, seq):
    t = x2.shape[0]
    tm = IN_TM
    n = t // tm
    full = lambda a: pl.BlockSpec(a.shape, lambda i: (0, 0))
    row = lambda c: pl.BlockSpec((tm, c), lambda i: (jnp.minimum(i, n - 1), 0))
    return pl.pallas_call(
        functools.partial(_in_proj_kernel, steps_per_seq=seq // tm),
        grid=(n + 1,),
        in_specs=[row(D_MODEL), full(g_mix), full(wqkv), full(wz), full(wu), full(wg), full(wba), full(wbat),
                  full(conv_w)],
        out_specs=[pl.BlockSpec((tm, 3 * DN_WIDTH), lambda i: (jnp.maximum(i - 1, 0), 0)),
                   row(DN_WIDTH), row(POOL_WIDTH), row(2 * D_MODEL), row(2 * LANES),
                   pl.BlockSpec((2 * SUBLANES, tm), lambda i: (0, jnp.minimum(i, n - 1)))],
        out_shape=[jax.ShapeDtypeStruct((t, 3 * DN_WIDTH), F32),
                   jax.ShapeDtypeStruct((t, DN_WIDTH), F32),
                   jax.ShapeDtypeStruct((t, POOL_WIDTH), F32),
                   jax.ShapeDtypeStruct((t, 2 * D_MODEL), F32),
                   jax.ShapeDtypeStruct((t, 2 * LANES), F32),
                   jax.ShapeDtypeStruct((2 * SUBLANES, t), F32)],
        scratch_shapes=[pltpu.VMEM((SUBLANES + tm, 3 * DN_WIDTH), F32)],
        compiler_params=pltpu.CompilerParams(dimension_semantics=("arbitrary",), vmem_limit_bytes=VMEM_LIMIT),
        name="in_proj",
    )(x2, g_mix, wqkv, wz, wu, wg, wba, wbat, conv_w)


def _deltanet_kernel(qkv_ref, z_ref, ba_ref, bat_ref, alog_r_ref, dtb_r_ref, alog_c_ref, dtb_c_ref, dnw_ref,
                     head_ones_ref, tril_ref, triu_ref, chunk_ones_ref, expand_ref, o_ref,
                     s_ref, qn_ref, kn_ref, kbe_ref, vb_ref, qg_ref, kd_ref,
                     xbeta_ref, xgc_ref, xgl_ref, gr_ref, oacc_ref, qkd_ref, wq_ref, u_ref, kdt_ref, gl_ref):
    lb = qkv_ref.shape[0]
    n_chunks = lb // CHUNK

    @pl.when(pl.program_id(1) == 0)
    def _():
        s_ref[...] = jnp.zeros_like(s_ref)

    q = qkv_ref[:, 0:DN_WIDTH]
    k = qkv_ref[:, DN_WIDTH:2 * DN_WIDTH]
    v = qkv_ref[:, 2 * DN_WIDTH:3 * DN_WIDTH]

    head_ones = head_ones_ref[...]
    qn = q * lax.rsqrt(_dot_exact_rhs(q * q, head_ones) + NORM_EPS) * (DN_HEAD_DIM ** -0.5)
    kn = k * lax.rsqrt(_dot_exact_rhs(k * k, head_ones) + NORM_EPS)

    ba = ba_ref[...]
    beta_c = _sigmoid(ba[:, 0:LANES])
    g_c = -jnp.exp(alog_r_ref[...]) * _softplus(ba[:, LANES:2 * LANES] + dtb_r_ref[...])
    lane = lax.broadcasted_iota(jnp.int32, (lb, LANES), 1)
    g_c = jnp.where(lane < DN_HEADS, g_c, 0.0)
    bat = bat_ref[...]
    g_r = -jnp.exp(alog_c_ref[...]) * _softplus(bat[SUBLANES:2 * SUBLANES] + dtb_c_ref[...])

    gc = _dot_exact_lhs(tril_ref[...], g_c)
    gtot = _dot_exact_lhs(chunk_ones_ref[...], g_c)
    gr = _dot_exact_rhs(g_r, triu_ref[...])
    for c in range(n_chunks):
        gr_ref[c] = gr[:, c * CHUNK:(c + 1) * CHUNK]

    expand = expand_ref[...]
    x_beta = _dot_exact_rhs(beta_c, expand)
    x_gc = _dot_exact_rhs(gc, expand)
    x_gtot = _dot_exact_rhs(gtot, expand)
    x_eg = jnp.exp(x_gc)
    xbeta_ref[...] = x_beta
    xgc_ref[...] = x_gc
    xgl_ref[...] = jnp.exp(x_gtot)
    qn_ref[...] = qn.astype(BF16)
    kn_ref[...] = kn.astype(BF16)
    kbe_ref[...] = (kn * (x_beta * x_eg)).astype(BF16)
    vb_ref[...] = (v * x_beta).astype(BF16)
    qg_ref[...] = (qn * x_eg).astype(BF16)
    kd_ref[...] = (kn * jnp.exp(x_gtot - x_gc)).astype(BF16)

    ci = lax.broadcasted_iota(jnp.int32, (CHUNK, CHUNK), 0)
    cj = lax.broadcasted_iota(jnp.int32, (CHUNK, CHUNK), 1)
    eye = (ci == cj).astype(F32)

    heads = range(DN_HEADS)
    lanes = [pl.ds(h * DN_HEAD_DIM, DN_HEAD_DIM) for h in heads]
    stack = lambda top, bot: jnp.concatenate([top, bot], axis=0)

    chunk_rows = lambda c: pl.ds(c * CHUNK, CHUNK)

    def within_chunks(chunk_ids):
        chains = [(c, h) for c in chunk_ids for h in heads]
        ids = range(len(chains))
        rows = [chunk_rows(c) for c, _ in chains]
        ln = [lanes[h] for _, h in chains]
        kb = [kn_ref[rows[i], ln[i]] for i in ids]
        kq = [_dot_nt(stack(kb[i], qn_ref[rows[i], ln[i]]), kb[i]) for i in ids]
        yield
        decay = [jnp.exp(jnp.where(ci >= cj, xgc_ref[rows[i], ln[i]] - gr_ref[c, h:h + 1, :], -jnp.inf))
                 for i, (c, h) in enumerate(chains)]
        a = [jnp.where(ci > cj, xbeta_ref[rows[i], ln[i]] * kq[i][0:CHUNK] * decay[i], 0.0) for i in ids]
        slot = [c * DN_HEADS + h for c, h in chains]
        for i in ids:
            qkd_ref[slot[i]] = (kq[i][CHUNK:2 * CHUNK] * decay[i]).astype(BF16)
            wq_ref[slot[i], CHUNK:2 * CHUNK, :] = qg_ref[rows[i], ln[i]]
            kdt_ref[slot[i]] = kd_ref[rows[i], ln[i]].T.astype(BF16)
            gl_ref[slot[i]] = xgl_ref[rows[i], ln[i]]
        t_inv = [eye - a[i] for i in ids]
        pw = [a[i].astype(BF16) for i in ids]
        pw = [_dot(pw[i], pw[i]).astype(BF16) for i in ids]
        yield
        for _ in range(4):
            r = [_dot(stack(t_inv[i].astype(BF16), pw[i]), pw[i]) for i in ids]
            yield
            t_inv = [t_inv[i] + r[i][0:CHUNK] for i in ids]
            pw = [r[i][CHUNK:2 * CHUNK].astype(BF16) for i in ids]
        t_inv = [t_inv[i] + _dot(t_inv[i].astype(BF16), pw[i]) for i in ids]
        yield
        tb = [t_inv[i].astype(BF16) for i in ids]
        for i in ids:
            wq_ref[slot[i], 0:CHUNK, :] = _dot(tb[i], kbe_ref[rows[i], ln[i]]).astype(BF16)
        yield
        for i in ids:
            u_ref[slot[i]] = _dot(tb[i], vb_ref[rows[i], ln[i]])
        yield

    def state_chunks(chunk_ids):
        for c in chunk_ids:
            rows = chunk_rows(c)
            slot = [c * DN_HEADS + h for h in heads]
            s = [s_ref[h] for h in heads]
            ws = [_dot(wq_ref[slot[h]], s[h].astype(BF16)) for h in heads]
            yield
            vnb = [(u_ref[slot[h]] - ws[h][0:CHUNK]).astype(BF16) for h in heads]
            for h in heads:
                oacc_ref[h, rows, :] = ws[h][CHUNK:2 * CHUNK] + _dot(qkd_ref[slot[h]], vnb[h])
            yield
            for h in heads:
                s_ref[h] = s[h] * gl_ref[slot[h]] + _dot(kdt_ref[slot[h]], vnb[h])
            yield

    groups = [list(range(g, min(g + INTRA_GROUP, n_chunks))) for g in range(0, n_chunks, INTRA_GROUP)]
    pending = iter(())
    for grp in groups:
        for _ in within_chunks(grp):
            next(pending, None)
        for _ in pending:
            pass
        pending = state_chunks(grp)
    for _ in pending:
        pass

    o = jnp.concatenate([oacc_ref[h] for h in heads], axis=1)
    ms = _dot_exact_rhs(o * o, head_ones) * (1.0 / DN_HEAD_DIM)
    z = z_ref[...]
    o_ref[...] = (o * lax.rsqrt(ms + NORM_EPS) * dnw_ref[...] * (z * _sigmoid(z))).astype(o_ref.dtype)


def _deltanet_masks(lb):
    head = jnp.arange(DN_WIDTH) // DN_HEAD_DIM
    head_ones = head[:, None] == head[None, :]
    pos = jnp.arange(lb)
    same = (pos[:, None] // CHUNK) == (pos[None, :] // CHUNK)
    tril = same & (pos[None, :] <= pos[:, None])
    triu = same & (pos[:, None] <= pos[None, :])
    expand = jnp.arange(LANES)[:, None] == head[None, :]
    return tuple(m.astype(BF16) for m in (head_ones, tril, triu, same, expand))


def _deltanet(qkv, z, ba, bat, alog_r, dtb_r, alog_c, dtb_c, dnw, batch, seq):
    lb = DN_LB
    nb = seq // lb
    slots = lb // CHUNK * DN_HEADS
    masks = _deltanet_masks(lb)
    full = lambda a: pl.BlockSpec(a.shape, lambda b, j: (0,) * a.ndim)
    row = lambda n: pl.BlockSpec((lb, n), lambda b, j: (b * nb + j, 0))
    return pl.pallas_call(
        _deltanet_kernel,
        grid=(batch, nb),
        in_specs=[row(3 * DN_WIDTH), row(DN_WIDTH), row(2 * LANES),
                  pl.BlockSpec((2 * SUBLANES, lb), lambda b, j: (0, b * nb + j)),
                  full(alog_r), full(dtb_r), full(alog_c), full(dtb_c), full(dnw)] + [full(m) for m in masks],
        out_specs=row(DN_WIDTH),
        out_shape=jax.ShapeDtypeStruct((batch * seq, DN_WIDTH), BF16),
        scratch_shapes=[
            pltpu.VMEM((DN_HEADS, DN_HEAD_DIM, DN_HEAD_DIM), F32),
            pltpu.VMEM((lb, DN_WIDTH), BF16),
            pltpu.VMEM((lb, DN_WIDTH), BF16),
            pltpu.VMEM((lb, DN_WIDTH), BF16),
            pltpu.VMEM((lb, DN_WIDTH), BF16),
            pltpu.VMEM((lb, DN_WIDTH), BF16),
            pltpu.VMEM((lb, DN_WIDTH), BF16),
            pltpu.VMEM((lb, DN_WIDTH), F32),
            pltpu.VMEM((lb, DN_WIDTH), F32),
            pltpu.VMEM((lb, DN_WIDTH), F32),
            pltpu.VMEM((lb // CHUNK, DN_HEADS, CHUNK), F32),
            pltpu.VMEM((DN_HEADS, lb, DN_HEAD_DIM), F32),
            pltpu.VMEM((slots, CHUNK, CHUNK), BF16),
            pltpu.VMEM((slots, 2 * CHUNK, DN_HEAD_DIM), BF16),
            pltpu.VMEM((slots, CHUNK, DN_HEAD_DIM), F32),
            pltpu.VMEM((slots, DN_HEAD_DIM, CHUNK), BF16),
            pltpu.VMEM((slots, DN_HEAD_DIM, DN_HEAD_DIM), F32),
        ],
        compiler_params=pltpu.CompilerParams(dimension_semantics=("arbitrary", "arbitrary"),
                                             vmem_limit_bytes=VMEM_LIMIT),
        name="deltanet",
    )(qkv, z, ba, bat, alog_r, dtb_r, alog_c, dtb_c, dnw, *masks)


def _mix_kernel(x_ref, on_ref, u_ref, gates_ref, wupa_ref, poolw_ref, pscale_ref, wupb_ref, wout_ref,
                gffn_ref, wr_hi_ref, wr_lo_ref, br_ref,
                x1_ref, h2_ref, route_ref, rank_ref, expert_ref, counts_ref,
                ucarry_ref, ecount_ref, logits_ref, *, steps_per_seq):
    i = pl.program_id(0)
    tm = x_ref.shape[0]

    @pl.when(i == 0)
    def _():
        ecount_ref[...] = jnp.zeros_like(ecount_ref)
        logits_ref[...] = jnp.zeros_like(logits_ref)

    @pl.when(i % steps_per_seq == 0)
    def _():
        ucarry_ref[...] = jnp.zeros_like(ucarry_ref)

    lane = lax.broadcasted_iota(jnp.int32, (tm, LANES), 1)
    lg = jnp.where(lane < N_EXPERTS, logits_ref[...], -jnp.inf)
    vals, idxs, sels = [], [], []
    for _ in range(TOP_K):
        m = jnp.max(lg, axis=-1, keepdims=True)
        idx = jnp.min(jnp.where(lg == m, lane, LANES), axis=-1, keepdims=True)
        sel = lane == idx
        vals.append(m)
        idxs.append(idx)
        sels.append(sel)
        lg = jnp.where(sel, -jnp.inf, lg)
    es = [jnp.exp(vk - vals[0]) for vk in vals]
    denom = es[0] + es[1] + es[2] + es[3]
    chosen = jnp.zeros((tm, LANES), F32)
    for sel in sels:
        chosen = chosen + jnp.where(sel, 1.0, 0.0)
    chosen = jnp.where(i > 0, chosen, 0.0)

    u = u_ref[...]
    ext = jnp.concatenate([ucarry_ref[...], u], axis=0)
    ucarry_ref[...] = u[tm - POOL_HALO:tm]
    t_pos = (i % steps_per_seq) * tm + lax.broadcasted_iota(jnp.int32, (tm, POOL_GROUP_DIM), 0)
    ys = []
    for g in range(POOL_GROUPS):
        s = ext[:, g * POOL_GROUP_DIM:(g + 1) * POOL_GROUP_DIM]
        shift = 1
        while shift < POOL_WINDOWS[g]:
            s = s + pltpu.roll(s, shift, 0)
            shift *= 2
        count = jnp.minimum(t_pos + 1, POOL_WINDOWS[g]).astype(F32)
        pooled = s[POOL_HALO:] / count - u[:, g * POOL_GROUP_DIM:(g + 1) * POOL_GROUP_DIM]
        ys.append(_dot(pooled.astype(BF16), poolw_ref[g]))
    yb = jnp.concatenate(ys, axis=-1) * pscale_ref[...]
    y_b = _dot(yb.astype(BF16), wupb_ref[...])
    y_a = _dot(on_ref[...], wupa_ref[...])
    gates = gates_ref[...]
    merged = _sigmoid(gates[:, 0:D_MODEL]) * y_a + _sigmoid(gates[:, D_MODEL:2 * D_MODEL]) * y_b
    x1 = x_ref[...] + _dot(merged.astype(BF16), wout_ref[...])
    x1_ref[...] = x1

    h2 = _rms(x1, gffn_ref[...])
    _store_row_tiles(h2_ref, h2)

    hi, lo = _split2(h2)
    logits_ref[...] = (_dot(hi, wr_hi_ref[...]) + _dot(lo, wr_hi_ref[...]) + _dot(hi, wr_lo_ref[...])
                       + br_ref[...])

    ri = lax.broadcasted_iota(jnp.int32, (tm, tm), 0)
    rj = lax.broadcasted_iota(jnp.int32, (tm, tm), 1)
    before = (rj < ri).astype(BF16)
    pos = ecount_ref[...] + _dot(before, chosen.astype(BF16))
    ecount_ref[...] = ecount_ref[...] + jnp.sum(chosen, axis=0, keepdims=True)
    counts_ref[...] = ecount_ref[...]

    per_row = LANES // TOP_K
    first_lane = (lax.broadcasted_iota(jnp.int32, (tm, LANES), 0) % per_row) * TOP_K
    weights = jnp.zeros((tm, LANES), F32)
    flat_rank = jnp.zeros((tm, LANES), F32)
    flat_expert = jnp.zeros((tm, LANES), F32)
    for kk in range(TOP_K):
        rank = jnp.sum(jnp.where(sels[kk], pos, 0.0), axis=-1, keepdims=True)
        weights = jnp.where(lane == kk, es[kk] / denom, weights)
        flat_rank = jnp.where(lane == first_lane + kk, rank, flat_rank)
        flat_expert = jnp.where(lane == first_lane + kk, idxs[kk].astype(F32), flat_expert)
    route_ref[...] = weights
    fold = lambda a: jnp.sum(a.reshape(tm // per_row, per_row, LANES), axis=1).astype(jnp.int32)
    rank_ref[...] = fold(flat_rank)
    expert_ref[...] = fold(flat_expert)


def _mix(x2, on, u, gates, wupa, poolw, pscale, wupb, wout, gffn, wr_hi, wr_lo, br, batch, seq):
    tm = MIX_TM
    t = batch * seq
    n = t // tm
    full = lambda a: pl.BlockSpec(a.shape, lambda i: (0,) * a.ndim)
    row = lambda c: pl.BlockSpec((tm, c), lambda i: (jnp.minimum(i, n - 1), 0))
    return pl.pallas_call(
        functools.partial(_mix_kernel, steps_per_seq=seq // tm),
        grid=(n + 1,),
        in_specs=[row(D_MODEL), row(DN_WIDTH), row(POOL_WIDTH), row(2 * D_MODEL),
                  full(wupa), full(poolw), full(pscale), full(wupb), full(wout), full(gffn),
                  full(wr_hi), full(wr_lo), full(br)],
        out_specs=[pl.BlockSpec((tm, D_MODEL), lambda i: (i, 0)),
                   pl.BlockSpec((tm * ROW_SUB, LANES), lambda i: (i, 0)),
                   pl.BlockSpec((tm, LANES), lambda i: (jnp.maximum(i - 1, 0), 0)),
                   pl.BlockSpec((tm * TOP_K // LANES, LANES), lambda i: (jnp.maximum(i - 1, 0), 0)),
                   pl.BlockSpec((tm * TOP_K // LANES, LANES), lambda i: (jnp.maximum(i - 1, 0), 0)),
                   pl.BlockSpec((1, LANES), lambda i: (0, 0))],
        out_shape=[jax.ShapeDtypeStruct((t + tm, D_MODEL), F32),
                   jax.ShapeDtypeStruct(((t + tm) * ROW_SUB, LANES), F32),
                   jax.ShapeDtypeStruct((t, LANES), F32),
                   jax.ShapeDtypeStruct((t * TOP_K // LANES, LANES), jnp.int32),
                   jax.ShapeDtypeStruct((t * TOP_K // LANES, LANES), jnp.int32),
                   jax.ShapeDtypeStruct((1, LANES), F32)],
        scratch_shapes=[pltpu.VMEM((POOL_HALO, POOL_WIDTH), F32), pltpu.VMEM((1, LANES), F32),
                        pltpu.VMEM((tm, LANES), F32)],
        compiler_params=pltpu.CompilerParams(dimension_semantics=("arbitrary",), vmem_limit_bytes=VMEM_LIMIT),
        name="mix",
    )(x2, on, u, gates, wupa, poolw, pscale, wupb, wout, gffn, wr_hi, wr_lo, br)


def _routing_tables(rank, expert, counts, tg, n_tiles):
    cnt = counts[0, 0:N_EXPERTS].astype(jnp.int32)
    padded = (cnt + (tg - 1)) // tg * tg
    ends = jnp.cumsum(padded)
    offs = ends - padded
    experts = jnp.arange(N_EXPERTS, dtype=jnp.int32)
    dest = jnp.sum(jnp.where(expert[..., None] == experts, offs, 0), axis=-1) + rank
    n_used = ends[N_EXPERTS - 1] // tg
    tile_start = jnp.arange(n_tiles, dtype=jnp.int32) * tg
    tile_expert = jnp.sum((tile_start[:, None] >= ends[None, :]).astype(jnp.int32), axis=1)
    tile_expert = jnp.minimum(tile_expert, N_EXPERTS - 1)
    last = tile_expert[jnp.maximum(n_used - 1, 0)]
    tile_expert = jnp.where(jnp.arange(n_tiles) < n_used, tile_expert, last)
    pad_start = offs + cnt
    pad_len = padded - cnt
    later = (experts[None, :] > experts[:, None]) & (cnt[None, :] > 0)
    next_expert = jnp.min(jnp.where(later, experts[None, :], N_EXPERTS), axis=1)
    next_expert = jnp.where(next_expert == N_EXPERTS, experts, next_expert)
    return (dest, tile_expert.astype(jnp.int32), next_expert.astype(jnp.int32),
            n_used.reshape(1).astype(jnp.int32), pad_start.astype(jnp.int32), pad_len.astype(jnp.int32))


def _dispatch_kernel(pad_start_ref, pad_len_ref, nu_ref, dest_ref, h2_ref, xs_ref, zero_ref, sem, zsem):
    tm = h2_ref.shape[0] // ROW_SUB
    tg = zero_ref.shape[0]
    n_tiles = xs_ref.shape[0] // tg

    @pl.when(pl.program_id(0) == 0)
    def _():
        zero_ref[...] = jnp.zeros_like(zero_ref)

        def pad_runs(e, act):
            first, n = pad_start_ref[e], pad_len_ref[e]
            for bit in reversed(range((tg // ROW_SUB).bit_length() - 1)):
                rows = (1 << bit) * ROW_SUB

                @pl.when((n >> bit) & 1 == 1)
                def _():
                    start = pl.multiple_of((first + ((n >> (bit + 1)) << (bit + 1))) * ROW_SUB, ROW_SUB)
                    act(pltpu.make_async_copy(zero_ref.at[pl.ds(0, rows)], xs_ref.at[pl.ds(start, rows)], zsem))

        def tile_copy(i):
            return pltpu.make_async_copy(zero_ref, xs_ref.at[pl.ds(pl.multiple_of(i * tg, tg), tg)], zsem)

        lax.fori_loop(0, N_EXPERTS, lambda e, cc: (pad_runs(e, lambda cp: cp.start()), cc)[1], 0)
        lax.fori_loop(0, N_EXPERTS, lambda e, cc: (pad_runs(e, lambda cp: cp.wait()), cc)[1], 0)
        lax.fori_loop(nu_ref[0], n_tiles, lambda i, cc: (tile_copy(i).start(), cc)[1], 0)
        lax.fori_loop(nu_ref[0], n_tiles, lambda i, cc: (tile_copy(0).wait(), cc)[1], 0)

    def row_copy(r, d):
        return pltpu.make_async_copy(_row_tile(h2_ref, r), _row_tile(xs_ref, d), sem)

    def start(r, c):
        for kk in range(TOP_K):
            row_copy(r, dest_ref[0, 0, r * TOP_K + kk]).start(priority=kk % 2)
        return c

    lax.fori_loop(0, tm, start, 0, unroll=DMA_UNROLL)
    for kk in range(TOP_K):
        pltpu.make_async_copy(h2_ref, xs_ref.at[pl.ds(0, tm * ROW_SUB)], sem).wait()


def _dispatch(pad_start, pad_len, n_used, dest3, h2, n_rows):
    t = dest3.shape[0] * dest3.shape[2] // TOP_K
    tm = DSP_TM
    return pl.pallas_call(
        _dispatch_kernel,
        grid_spec=pltpu.PrefetchScalarGridSpec(
            num_scalar_prefetch=3,
            grid=(t // tm,),
            in_specs=[pl.BlockSpec((1, 1, tm * TOP_K), lambda i, ps, pn, nu: (i, 0, 0), memory_space=pltpu.SMEM),
                      pl.BlockSpec((tm * ROW_SUB, LANES), lambda i, ps, pn, nu: (i, 0))],
            out_specs=pl.BlockSpec(memory_space=pl.ANY),
            scratch_shapes=[pltpu.VMEM((MOE_TG * ROW_SUB, LANES), F32), pltpu.SemaphoreType.DMA,
                            pltpu.SemaphoreType.DMA]),
        out_shape=jax.ShapeDtypeStruct((n_rows * ROW_SUB, LANES), F32),
        compiler_params=pltpu.CompilerParams(dimension_semantics=("arbitrary",), vmem_limit_bytes=VMEM_LIMIT),
        name="dispatch",
    )(pad_start, pad_len, n_used, dest3, h2)


def _experts_kernel(te_ref, nx_ref, nu_ref, xs_ref, wgu_hbm_ref, bgu_ref, wd_hbm_ref, bd_ref, ys_ref,
                    wgu_f32_ref, wd_f32_ref, wgu_bf_ref, wd_bf_ref, sem):
    i = pl.program_id(0)
    e = te_ref[i]

    def weight_copies(ex):
        return (pltpu.make_async_copy(wgu_hbm_ref.at[ex], wgu_f32_ref, sem.at[0]),
                pltpu.make_async_copy(wd_hbm_ref.at[ex], wd_f32_ref, sem.at[1]))

    @pl.when(i == 0)
    def _():
        for cp in weight_copies(e):
            cp.start()

    @pl.when((i == 0) | (e != te_ref[jnp.maximum(i - 1, 0)]))
    def _():
        for cp in weight_copies(e):
            cp.wait()
        wgu_bf_ref[...] = wgu_f32_ref[...].astype(BF16)
        wd_bf_ref[...] = wd_f32_ref[...].astype(BF16)

        @pl.when(nx_ref[e] != e)
        def _():
            for cp in weight_copies(nx_ref[e]):
                cp.start()

    @pl.when(i < nu_ref[0])
    def _():
        gu = _dot(_load_row_tiles(xs_ref).astype(BF16), wgu_bf_ref[...]) + bgu_ref[...]
        gate = jnp.minimum(gu[:, 0:D_FF], SWIGLU_LIMIT)
        up = jnp.clip(gu[:, D_FF:2 * D_FF], -SWIGLU_LIMIT, SWIGLU_LIMIT)
        act = gate * _sigmoid(SWIGLU_ALPHA * gate) * (up + 1.0)
        _store_row_tiles(ys_ref, _dot(act.astype(BF16), wd_bf_ref[...]) + bd_ref[...])

    @pl.when(i >= nu_ref[0])
    def _():
        ys_ref[...] = jnp.zeros_like(ys_ref)


def _experts(tile_expert, next_expert, n_used, xs, wgu, bgu, wd, bd):
    tg = MOE_TG * ROW_SUB
    n_tiles = xs.shape[0] // tg
    tile = lambda i, te, nx, nu: (jnp.minimum(i, nu[0] - 1), 0)
    expert = lambda i, te, nx, nu: (te[i], 0, 0)
    return pl.pallas_call(
        _experts_kernel,
        grid_spec=pltpu.PrefetchScalarGridSpec(
            num_scalar_prefetch=3,
            grid=(n_tiles,),
            in_specs=[pl.BlockSpec((tg, LANES), tile),
                      pl.BlockSpec(memory_space=pl.ANY),
                      pl.BlockSpec((None, 1, 2 * D_FF), expert),
                      pl.BlockSpec(memory_space=pl.ANY),
                      pl.BlockSpec((None, 1, D_MODEL), expert)],
            out_specs=pl.BlockSpec((tg, LANES), lambda i, te, nx, nu: (i, 0)),
            scratch_shapes=[pltpu.VMEM((D_MODEL, 2 * D_FF), F32), pltpu.VMEM((D_FF, D_MODEL), F32),
                            pltpu.VMEM((D_MODEL, 2 * D_FF), BF16), pltpu.VMEM((D_FF, D_MODEL), BF16),
                            pltpu.SemaphoreType.DMA((2,))]),
        out_shape=jax.ShapeDtypeStruct(xs.shape, F32),
        compiler_params=pltpu.CompilerParams(dimension_semantics=("arbitrary",), vmem_limit_bytes=VMEM_LIMIT),
        name="experts",
    )(tile_expert, next_expert, n_used, xs, wgu, bgu, wd, bd)


def _combine_kernel(dest_ref, dest_next_ref, route_ref, x1_ref, gfin_ref, ys_ref, out_ref, ybuf_ref, sem):
    i = pl.program_id(0)
    tm = x1_ref.shape[0]

    def gather_rows(dref, buf):
        def start(r, c):
            for kk in range(TOP_K):
                pltpu.make_async_copy(_row_tile(ys_ref, dref[0, 0, r * TOP_K + kk]),
                                      _row_tile(ybuf_ref.at[buf, kk], r), sem.at[buf]).start(priority=kk % 2)
            return c
        lax.fori_loop(0, tm, start, 0, unroll=DMA_UNROLL)

    @pl.when(i == 0)
    def _():
        gather_rows(dest_ref, 0)

    @pl.when(i + 1 < pl.num_programs(0))
    def _():
        gather_rows(dest_next_ref, (i + 1) % 2)

    buf = i % 2
    for kk in range(TOP_K):
        pltpu.make_async_copy(ys_ref.at[pl.ds(0, tm * ROW_SUB)], ybuf_ref.at[buf, kk], sem.at[buf]).wait()
    route = route_ref[...]
    acc = x1_ref[...]
    for kk in range(TOP_K):
        acc = acc + route[:, kk:kk + 1] * _load_row_tiles(ybuf_ref.at[buf, kk])
    out_ref[...] = _rms(acc, gfin_ref[...])


def _combine(dest3, route, x1, gfin, ys):
    t = route.shape[0]
    tm = CMB_TM
    last = t // tm - 1
    return pl.pallas_call(
        _combine_kernel,
        grid=(t // tm,),
        in_specs=[pl.BlockSpec((1, 1, tm * TOP_K), lambda i: (i, 0, 0), memory_space=pltpu.SMEM),
                  pl.BlockSpec((1, 1, tm * TOP_K), lambda i: (jnp.minimum(i + 1, last), 0, 0),
                               memory_space=pltpu.SMEM),
                  pl.BlockSpec((tm, LANES), lambda i: (i, 0)),
                  pl.BlockSpec((tm, D_MODEL), lambda i: (i, 0)),
                  pl.BlockSpec((1, D_MODEL), lambda i: (0, 0)),
                  pl.BlockSpec(memory_space=pl.ANY)],
        out_specs=pl.BlockSpec((tm, D_MODEL), lambda i: (i, 0)),
        out_shape=jax.ShapeDtypeStruct((t, D_MODEL), F32),
        scratch_shapes=[pltpu.VMEM((2, TOP_K, tm * ROW_SUB, LANES), F32), pltpu.SemaphoreType.DMA((2,))],
        compiler_params=pltpu.CompilerParams(dimension_semantics=("arbitrary",), vmem_limit_bytes=VMEM_LIMIT),
        name="combine",
    )(dest3, dest3, route, x1, gfin, ys)


def kernel(x, g_mix, w_in, conv_w, a_log, dt_bias, dn_norm, w_up_a, pool_w, pool_scale, w_up_b, w_out, g_ffn,
           w_router, b_router, w_gate_up, b_gate_up, w_down, b_down, g_final):
    batch, seq, d = x.shape
    assert d == D_MODEL and seq % DN_LB == 0 and seq % MIX_TM == 0
    assert (batch * seq) % DSP_TM == 0 and (batch * seq * TOP_K) % MOE_TG == 0
    assert g_mix.shape[0] == 1, "one layer"
    t = batch * seq
    x2 = x.reshape(t, d)

    w = w_in[0]
    o_z = 3 * DN_WIDTH
    o_b = o_z + DN_WIDTH
    o_a = o_b + DN_HEADS
    o_u = o_a + DN_HEADS
    o_g = o_u + POOL_WIDTH
    wqkv = w[:, 0:o_z].astype(BF16)
    wz = w[:, o_z:o_b].astype(BF16)
    wu = w[:, o_u:o_g].astype(BF16)
    wg = w[:, o_g:].astype(BF16)
    w_b = w[:, o_b:o_a]
    w_a = w[:, o_a:o_u]
    pad = jnp.zeros((d, LANES - DN_HEADS), F32)
    wba = jnp.concatenate([w_b, pad, w_a, pad], axis=1).astype(BF16)
    wbat = jnp.concatenate([w_b, w_a], axis=1).T.astype(BF16)

    qkv, z, u, gates, ba, bat = _in_proj(x2, g_mix, wqkv, wz, wu, wg, wba, wbat, conv_w[0], seq)

    lane_pad = lambda p: jnp.pad(p.reshape(1, DN_HEADS), ((0, 0), (0, LANES - DN_HEADS)))
    on = _deltanet(qkv, z, ba, bat, lane_pad(a_log[0]), lane_pad(dt_bias[0]),
                   a_log[0].reshape(DN_HEADS, 1), dt_bias[0].reshape(DN_HEADS, 1),
                   jnp.tile(dn_norm[0], DN_HEADS).reshape(1, DN_WIDTH), batch, seq)

    wr = jnp.pad(w_router[0], ((0, 0), (0, LANES - N_EXPERTS)))
    wr_hi = wr.astype(BF16)
    wr_lo = (wr - wr_hi.astype(F32)).astype(BF16)
    br = jnp.pad(b_router[0].reshape(1, N_EXPERTS), ((0, 0), (0, LANES - N_EXPERTS)))
    x1, h2, route, rank, expert, counts = _mix(x2, on, u, gates, w_up_a[0].astype(BF16), pool_w[0].astype(BF16),
                                                pool_scale[0].reshape(1, POOL_WIDTH), w_up_b[0].astype(BF16),
                                                w_out[0].astype(BF16), g_ffn, wr_hi, wr_lo, br, batch, seq)

    n_tiles = t * TOP_K // MOE_TG + N_EXPERTS
    dest, tile_expert, next_expert, n_used, pad_start, pad_len = _routing_tables(rank, expert, counts, MOE_TG,
                                                                                 n_tiles)
    xs = _dispatch(pad_start, pad_len, n_used, dest.reshape(t // DSP_TM, 1, DSP_TM * TOP_K), h2, n_tiles * MOE_TG)
    ys = _experts(tile_expert, next_expert, n_used, xs, w_gate_up[0],
                  b_gate_up[0].reshape(N_EXPERTS, 1, 2 * D_FF), w_down[0], b_down[0].reshape(N_EXPERTS, 1, D_MODEL))
    out = _combine(dest.reshape(t // CMB_TM, 1, CMB_TM * TOP_K), route, x1, g_final.reshape(1, D_MODEL), ys)
    return out.reshape(batch, seq, d)
```

```python
import functools

import jax
import jax.numpy as jnp
from jax import lax
from jax.experimental import pallas as pl
from jax.experimental.pallas import tpu as pltpu

F32 = jnp.float32
BF16 = jnp.bfloat16

D_MODEL = 1024
CHUNK = 64
DN_HEADS = 8
DN_HEAD_DIM = 64
DN_WIDTH = DN_HEADS * DN_HEAD_DIM
CONV_WIDTH = 4
POOL_GROUPS = 4
POOL_WINDOWS = (2, 4, 8, 16)
POOL_WIDTH = 512
POOL_GROUP_DIM = 128
POOL_HALO = 16
N_EXPERTS = 32
TOP_K = 4
D_FF = D_MODEL
SWIGLU_LIMIT = 7.0
SWIGLU_ALPHA = 1.702
NORM_EPS = 1e-6
LANES = 128
SUBLANES = 8
VMEM_LIMIT = 56 * 1024 * 1024

IN_TM = 512
DN_LB = 512
INTRA_GROUP = 2
MIX_TM = 512
MOE_TG = 512
DSP_TM = 2048
CMB_TM = 512
DMA_UNROLL = 8


def _dot(a, b):
    return jnp.dot(a, b, preferred_element_type=F32)


def _dot_nt(a, b):
    return lax.dot_general(a, b, (((1,), (1,)), ((), ())), preferred_element_type=F32)


def _split2(x):
    hi = x.astype(BF16)
    lo = (x - hi.astype(F32)).astype(BF16)
    return hi, lo


def _dot_exact_rhs(x, m):
    hi, lo = _split2(x)
    return _dot(hi, m) + _dot(lo, m)


def _dot_exact_lhs(m, x):
    hi, lo = _split2(x)
    return _dot(m, hi) + _dot(m, lo)


def _softplus(x):
    return jnp.maximum(x, 0.0) + jnp.log1p(jnp.exp(-jnp.abs(x)))


def _sigmoid(x):
    return 1.0 / (1.0 + jnp.exp(-x))


def _rms(x, g):
    return x * lax.rsqrt(jnp.mean(x * x, axis=-1, keepdims=True) + NORM_EPS) * g


ROW_SUB = D_MODEL // LANES


def _row_tile(ref, r):
    return ref.at[pl.ds(pl.multiple_of(r * ROW_SUB, ROW_SUB), ROW_SUB)]


def _store_row_tiles(ref, x):
    for j in range(ROW_SUB):
        ref[pl.ds(j, x.shape[0], stride=ROW_SUB), :] = x[:, j * LANES:(j + 1) * LANES]


def _load_row_tiles(ref):
    n = ref.shape[0] // ROW_SUB
    return jnp.concatenate([ref[pl.ds(j, n, stride=ROW_SUB), :] for j in range(ROW_SUB)], axis=1)


def _in_proj_kernel(x_ref, g_ref, wqkv_ref, wz_ref, wu_ref, wg_ref, wba_ref, wbat_ref, convw_ref,
                    qkv_ref, z_ref, u_ref, gates_ref, ba_ref, bat_ref, hist_ref, *, steps_per_seq):
    i = pl.program_id(0)
    tm = x_ref.shape[0]

    @pl.when(i == 0)
    def _():
        hist_ref[...] = jnp.zeros_like(hist_ref)

    @pl.when((i + steps_per_seq - 1) % steps_per_seq == 0)
    def _():
        hist_ref[0:SUBLANES, :] = jnp.zeros((SUBLANES, hist_ref.shape[1]), F32)

    cw = convw_ref[...]
    prev = hist_ref[SUBLANES:SUBLANES + tm, :]
    act = prev * cw[CONV_WIDTH - 1:CONV_WIDTH]
    for s in range(1, CONV_WIDTH):
        act = act + hist_ref[pl.ds(SUBLANES - s, tm), :] * cw[CONV_WIDTH - 1 - s:CONV_WIDTH - s]
    qkv_ref[...] = act * _sigmoid(act)

    hb = _rms(x_ref[...], g_ref[...]).astype(BF16)
    hist_ref[0:SUBLANES, :] = prev[tm - SUBLANES:tm]
    hist_ref[SUBLANES:SUBLANES + tm, :] = _dot(hb, wqkv_ref[...])
    z_ref[...] = _dot(hb, wz_ref[...])
    u_ref[...] = _dot(hb, wu_ref[...])
    gates_ref[...] = _dot(hb, wg_ref[...])
    ba_ref[...] = _dot(hb, wba_ref[...])
    bat_ref[...] = _dot_nt(wbat_ref[...], hb)


def _in_proj(x2, g_mix, wqkv, wz, wu, wg, wba, wbat, conv_w, seq):
    t = x2.shape[0]
    tm = IN_TM
    n = t // tm
    full = lambda a: pl.BlockSpec(a.shape, lambda i: (0, 0))
    row = lambda c: pl.BlockSpec((tm, c), lambda i: (jnp.minimum(i, n - 1), 0))
    return pl.pallas_call(
        functools.partial(_in_proj_kernel, steps_per_seq=seq // tm),
        grid=(n + 1,),
        in_specs=[row(D_MODEL), full(g_mix), full(wqkv), full(wz), full(wu), full(wg), full(wba), full(wbat),
                  full(conv_w)],
        out_specs=[pl.BlockSpec((tm, 3 * DN_WIDTH), lambda i: (jnp.maximum(i - 1, 0), 0)),
                   row(DN_WIDTH), row(POOL_WIDTH), row(2 * D_MODEL), row(2 * LANES),
                   pl.BlockSpec((2 * SUBLANES, tm), lambda i: (0, jnp.minimum(i, n - 1)))],
        out_shape=[jax.ShapeDtypeStruct((t, 3 * DN_WIDTH), F32),
                   jax.ShapeDtypeStruct((t, DN_WIDTH), F32),
                   jax.ShapeDtypeStruct((t, POOL_WIDTH), F32),
                   jax.ShapeDtypeStruct((t, 2 * D_MODEL), F32),
                   jax.ShapeDtypeStruct((t, 2 * LANES), F32),
                   jax.ShapeDtypeStruct((2 * SUBLANES, t), F32)],
        scratch_shapes=[pltpu.VMEM((SUBLANES + tm, 3 * DN_WIDTH), F32)],
        compiler_params=pltpu.CompilerParams(dimension_semantics=("arbitrary",), vmem_limit_bytes=VMEM_LIMIT),
        name="in_proj",
    )(x2, g_mix, wqkv, wz, wu, wg, wba, wbat, conv_w)


def _deltanet_kernel(qkv_ref, z_ref, ba_ref, bat_ref, alog_r_ref, dtb_r_ref, alog_c_ref, dtb_c_ref, dnw_ref,
                     head_ones_ref, tril_ref, triu_ref, chunk_ones_ref, expand_ref, o_ref,
                     s_ref, qn_ref, kn_ref, kbe_ref, vb_ref, qg_ref, kd_ref,
                     xbeta_ref, xgc_ref, xgl_ref, gr_ref, oacc_ref, qkd_ref, wq_ref, u_ref, kdt_ref, gl_ref):
    lb = qkv_ref.shape[0]
    n_chunks = lb // CHUNK

    @pl.when(pl.program_id(1) == 0)
    def _():
        s_ref[...] = jnp.zeros_like(s_ref)

    q = qkv_ref[:, 0:DN_WIDTH]
    k = qkv_ref[:, DN_WIDTH:2 * DN_WIDTH]
    v = qkv_ref[:, 2 * DN_WIDTH:3 * DN_WIDTH]

    head_ones = head_ones_ref[...]
    qn = q * lax.rsqrt(_dot_exact_rhs(q * q, head_ones) + NORM_EPS) * (DN_HEAD_DIM ** -0.5)
    kn = k * lax.rsqrt(_dot_exact_rhs(k * k, head_ones) + NORM_EPS)

    ba = ba_ref[...]
    beta_c = _sigmoid(ba[:, 0:LANES])
    g_c = -jnp.exp(alog_r_ref[...]) * _softplus(ba[:, LANES:2 * LANES] + dtb_r_ref[...])
    lane = lax.broadcasted_iota(jnp.int32, (lb, LANES), 1)
    g_c = jnp.where(lane < DN_HEADS, g_c, 0.0)
    bat = bat_ref[...]
    g_r = -jnp.exp(alog_c_ref[...]) * _softplus(bat[SUBLANES:2 * SUBLANES] + dtb_c_ref[...])

    gc = _dot_exact_lhs(tril_ref[...], g_c)
    gtot = _dot_exact_lhs(chunk_ones_ref[...], g_c)
    gr = _dot_exact_rhs(g_r, triu_ref[...])
    for c in range(n_chunks):
        gr_ref[c] = gr[:, c * CHUNK:(c + 1) * CHUNK]

    expand = expand_ref[...]
    x_beta = _dot_exact_rhs(beta_c, expand)
    x_gc = _dot_exact_rhs(gc, expand)
    x_gtot = _dot_exact_rhs(gtot, expand)
    x_eg = jnp.exp(x_gc)
    xbeta_ref[...] = x_beta
    xgc_ref[...] = x_gc
    xgl_ref[...] = jnp.exp(x_gtot)
    qn_ref[...] = qn.astype(BF16)
    kn_ref[...] = kn.astype(BF16)
    kbe_ref[...] = (kn * (x_beta * x_eg)).astype(BF16)
    vb_ref[...] = (v * x_beta).astype(BF16)
    qg_ref[...] = (qn * x_eg).astype(BF16)
    kd_ref[...] = (kn * jnp.exp(x_gtot - x_gc)).astype(BF16)

    ci = lax.broadcasted_iota(jnp.int32, (CHUNK, CHUNK), 0)
    cj = lax.broadcasted_iota(jnp.int32, (CHUNK, CHUNK), 1)
    eye = (ci == cj).astype(F32)

    heads = range(DN_HEADS)
    lanes = [pl.ds(h * DN_HEAD_DIM, DN_HEAD_DIM) for h in heads]
    stack = lambda top, bot: jnp.concatenate([top, bot], axis=0)

    chunk_rows = lambda c: pl.ds(c * CHUNK, CHUNK)

    def within_chunks(chunk_ids):
        chains = [(c, h) for c in chunk_ids for h in heads]
        ids = range(len(chains))
        rows = [chunk_rows(c) for c, _ in chains]
        ln = [lanes[h] for _, h in chains]
        kb = [kn_ref[rows[i], ln[i]] for i in ids]
        kq = [_dot_nt(stack(kb[i], qn_ref[rows[i], ln[i]]), kb[i]) for i in ids]
        yield
        decay = [jnp.exp(jnp.where(ci >= cj, xgc_ref[rows[i], ln[i]] - gr_ref[c, h:h + 1, :], -jnp.inf))
                 for i, (c, h) in enumerate(chains)]
        a = [jnp.where(ci > cj, xbeta_ref[rows[i], ln[i]] * kq[i][0:CHUNK] * decay[i], 0.0) for i in ids]
        slot = [c * DN_HEADS + h for c, h in chains]
        for i in ids:
            qkd_ref[slot[i]] = (kq[i][CHUNK:2 * CHUNK] * decay[i]).astype(BF16)
            wq_ref[slot[i], CHUNK:2 * CHUNK, :] = qg_ref[rows[i], ln[i]]
            kdt_ref[slot[i]] = kd_ref[rows[i], ln[i]].T.astype(BF16)
            gl_ref[slot[i]] = xgl_ref[rows[i], ln[i]]
        t_inv = [eye - a[i] for i in ids]
        pw = [a[i].astype(BF16) for i in ids]
        pw = [_dot(pw[i], pw[i]).astype(BF16) for i in ids]
        yield
        for _ in range(4):
            r = [_dot(stack(t_inv[i].astype(BF16), pw[i]), pw[i]) for i in ids]
            yield
            t_inv = [t_inv[i] + r[i][0:CHUNK] for i in ids]
            pw = [r[i][CHUNK:2 * CHUNK].astype(BF16) for i in ids]
        t_inv = [t_inv[i] + _dot(t_inv[i].astype(BF16), pw[i]) for i in ids]
        yield
        tb = [t_inv[i].astype(BF16) for i in ids]
        for i in ids:
            wq_ref[slot[i], 0:CHUNK, :] = _dot(tb[i], kbe_ref[rows[i], ln[i]]).astype(BF16)
        yield
        for i in ids:
            u_ref[slot[i]] = _dot(tb[i], vb_ref[rows[i], ln[i]])
        yield

    def state_chunks(chunk_ids):
        for c in chunk_ids:
            rows = chunk_rows(c)
            slot = [c * DN_HEADS + h for h in heads]
            s = [s_ref[h] for h in heads]
            ws = [_dot(wq_ref[slot[h]], s[h].astype(BF16)) for h in heads]
            yield
            vnb = [(u_ref[slot[h]] - ws[h][0:CHUNK]).astype(BF16) for h in heads]
            for h in heads:
                oacc_ref[h, rows, :] = ws[h][CHUNK:2 * CHUNK] + _dot(qkd_ref[slot[h]], vnb[h])
            yield
            for h in heads:
                s_ref[h] = s[h] * gl_ref[slot[h]] + _dot(kdt_ref[slot[h]], vnb[h])
            yield

    groups = [list(range(g, min(g + INTRA_GROUP, n_chunks))) for g in range(0, n_chunks, INTRA_GROUP)]
    pending = iter(())
    for grp in groups:
        for _ in within_chunks(grp):
            next(pending, None)
        for _ in pending:
            pass
        pending = state_chunks(grp)
    for _ in pending:
        pass

    o = jnp.concatenate([oacc_ref[h] for h in heads], axis=1)
    ms = _dot_exact_rhs(o * o, head_ones) * (1.0 / DN_HEAD_DIM)
    z = z_ref[...]
    o_ref[...] = (o * lax.rsqrt(ms + NORM_EPS) * dnw_ref[...] * (z * _sigmoid(z))).astype(o_ref.dtype)


def _deltanet_masks(lb):
    head = jnp.arange(DN_WIDTH) // DN_HEAD_DIM
    head_ones = head[:, None] == head[None, :]
    pos = jnp.arange(lb)
    same = (pos[:, None] // CHUNK) == (pos[None, :] // CHUNK)
    tril = same & (pos[None, :] <= pos[:, None])
    triu = same & (pos[:, None] <= pos[None, :])
    expand = jnp.arange(LANES)[:, None] == head[None, :]
    return tuple(m.astype(BF16) for m in (head_ones, tril, triu, same, expand))


def _deltanet(qkv, z, ba, bat, alog_r, dtb_r, alog_c, dtb_c, dnw, batch, seq):
    lb = DN_LB
    nb = seq // lb
    slots = lb // CHUNK * DN_HEADS
    masks = _deltanet_masks(lb)
    full = lambda a: pl.BlockSpec(a.shape, lambda b, j: (0,) * a.ndim)
    row = lambda n: pl.BlockSpec((lb, n), lambda b, j: (b * nb + j, 0))
    return pl.pallas_call(
        _deltanet_kernel,
        grid=(batch, nb),
        in_specs=[row(3 * DN_WIDTH), row(DN_WIDTH), row(2 * LANES),
                  pl.BlockSpec((2 * SUBLANES, lb), lambda b, j: (0, b * nb + j)),
                  full(alog_r), full(dtb_r), full(alog_c), full(dtb_c), full(dnw)] + [full(m) for m in masks],
        out_specs=row(DN_WIDTH),
        out_shape=jax.ShapeDtypeStruct((batch * seq, DN_WIDTH), BF16),
        scratch_shapes=[
            pltpu.VMEM((DN_HEADS, DN_HEAD_DIM, DN_HEAD_DIM), F32),
            pltpu.VMEM((lb, DN_WIDTH), BF16),
            pltpu.VMEM((lb, DN_WIDTH), BF16),
            pltpu.VMEM((lb, DN_WIDTH), BF16),
            pltpu.VMEM((lb, DN_WIDTH), BF16),
            pltpu.VMEM((lb, DN_WIDTH), BF16),
            pltpu.VMEM((lb, DN_WIDTH), BF16),
            pltpu.VMEM((lb, DN_WIDTH), F32),
            pltpu.VMEM((lb, DN_WIDTH), F32),
            pltpu.VMEM((lb, DN_WIDTH), F32),
            pltpu.VMEM((lb // CHUNK, DN_HEADS, CHUNK), F32),
            pltpu.VMEM((DN_HEADS, lb, DN_HEAD_DIM), F32),
            pltpu.VMEM((slots, CHUNK, CHUNK), BF16),
            pltpu.VMEM((slots, 2 * CHUNK, DN_HEAD_DIM), BF16),
            pltpu.VMEM((slots, CHUNK, DN_HEAD_DIM), F32),
            pltpu.VMEM((slots, DN_HEAD_DIM, CHUNK), BF16),
            pltpu.VMEM((slots, DN_HEAD_DIM, DN_HEAD_DIM), F32),
        ],
        compiler_params=pltpu.CompilerParams(dimension_semantics=("arbitrary", "arbitrary"),
                                             vmem_limit_bytes=VMEM_LIMIT),
        name="deltanet",
    )(qkv, z, ba, bat, alog_r, dtb_r, alog_c, dtb_c, dnw, *masks)


def _mix_kernel(x_ref, on_ref, u_ref, gates_ref, wupa_ref, poolw_ref, pscale_ref, wupb_ref, wout_ref,
                gffn_ref, wr_hi_ref, wr_lo_ref, br_ref,
                x1_ref, h2_ref, route_ref, rank_ref, expert_ref, counts_ref,
                ucarry_ref, ecount_ref, logits_ref, *, steps_per_seq):
    i = pl.program_id(0)
    tm = x_ref.shape[0]

    @pl.when(i == 0)
    def _():
        ecount_ref[...] = jnp.zeros_like(ecount_ref)
        logits_ref[...] = jnp.zeros_like(logits_ref)

    @pl.when(i % steps_per_seq == 0)
    def _():
        ucarry_ref[...] = jnp.zeros_like(ucarry_ref)

    lane = lax.broadcasted_iota(jnp.int32, (tm, LANES), 1)
    lg = jnp.where(lane < N_EXPERTS, logits_ref[...], -jnp.inf)
    vals, idxs, sels = [], [], []
    for _ in range(TOP_K):
        m = jnp.max(lg, axis=-1, keepdims=True)
        idx = jnp.min(jnp.where(lg == m, lane, LANES), axis=-1, keepdims=True)
        sel = lane == idx
        vals.append(m)
        idxs.append(idx)
        sels.append(sel)
        lg = jnp.where(sel, -jnp.inf, lg)
    es = [jnp.exp(vk - vals[0]) for vk in vals]
    denom = es[0] + es[1] + es[2] + es[3]
    chosen = jnp.zeros((tm, LANES), F32)
    for sel in sels:
        chosen = chosen + jnp.where(sel, 1.0, 0.0)
    chosen = jnp.where(i > 0, chosen, 0.0)

    u = u_ref[...]
    ext = jnp.concatenate([ucarry_ref[...], u], axis=0)
    ucarry_ref[...] = u[tm - POOL_HALO:tm]
    t_pos = (i % steps_per_seq) * tm + lax.broadcasted_iota(jnp.int32, (tm, POOL_GROUP_DIM), 0)
    ys = []
    for g in range(POOL_GROUPS):
        s = ext[:, g * POOL_GROUP_DIM:(g + 1) * POOL_GROUP_DIM]
        shift = 1
        while shift < POOL_WINDOWS[g]:
            s = s + pltpu.roll(s, shift, 0)
            shift *= 2
        count = jnp.minimum(t_pos + 1, POOL_WINDOWS[g]).astype(F32)
        pooled = s[POOL_HALO:] / count - u[:, g * POOL_GROUP_DIM:(g + 1) * POOL_GROUP_DIM]
        ys.append(_dot(pooled.astype(BF16), poolw_ref[g]))
    yb = jnp.concatenate(ys, axis=-1) * pscale_ref[...]
    y_b = _dot(yb.astype(BF16), wupb_ref[...])
    y_a = _dot(on_ref[...], wupa_ref[...])
    gates = gates_ref[...]
    merged = _sigmoid(gates[:, 0:D_MODEL]) * y_a + _sigmoid(gates[:, D_MODEL:2 * D_MODEL]) * y_b
    x1 = x_ref[...] + _dot(merged.astype(BF16), wout_ref[...])
    x1_ref[...] = x1

    h2 = _rms(x1, gffn_ref[...])
    _store_row_tiles(h2_ref, h2)

    hi, lo = _split2(h2)
    logits_ref[...] = (_dot(hi, wr_hi_ref[...]) + _dot(lo, wr_hi_ref[...]) + _dot(hi, wr_lo_ref[...])
                       + br_ref[...])

    ri = lax.broadcasted_iota(jnp.int32, (tm, tm), 0)
    rj = lax.broadcasted_iota(jnp.int32, (tm, tm), 1)
    before = (rj < ri).astype(BF16)
    pos = ecount_ref[...] + _dot(before, chosen.astype(BF16))
    ecount_ref[...] = ecount_ref[...] + jnp.sum(chosen, axis=0, keepdims=True)
    counts_ref[...] = ecount_ref[...]

    per_row = LANES // TOP_K
    first_lane = (lax.broadcasted_iota(jnp.int32, (tm, LANES), 0) % per_row) * TOP_K
    weights = jnp.zeros((tm, LANES), F32)
    flat_rank = jnp.zeros((tm, LANES), F32)
    flat_expert = jnp.zeros((tm, LANES), F32)
    for kk in range(TOP_K):
        rank = jnp.sum(jnp.where(sels[kk], pos, 0.0), axis=-1, keepdims=True)
        weights = jnp.where(lane == kk, es[kk] / denom, weights)
        flat_rank = jnp.where(lane == first_lane + kk, rank, flat_rank)
        flat_expert = jnp.where(lane == first_lane + kk, idxs[kk].astype(F32), flat_expert)
    route_ref[...] = weights
    fold = lambda a: jnp.sum(a.reshape(tm // per_row, per_row, LANES), axis=1).astype(jnp.int32)
    rank_ref[...] = fold(flat_rank)
    expert_ref[...] = fold(flat_expert)


def _mix(x2, on, u, gates, wupa, poolw, pscale, wupb, wout, gffn, wr_hi, wr_lo, br, batch, seq):
    tm = MIX_TM
    t = batch * seq
    n = t // tm
    full = lambda a: pl.BlockSpec(a.shape, lambda i: (0,) * a.ndim)
    row = lambda c: pl.BlockSpec((tm, c), lambda i: (jnp.minimum(i, n - 1), 0))
    return pl.pallas_call(
        functools.partial(_mix_kernel, steps_per_seq=seq // tm),
        grid=(n + 1,),
        in_specs=[row(D_MODEL), row(DN_WIDTH), row(POOL_WIDTH), row(2 * D_MODEL),
                  full(wupa), full(poolw), full(pscale), full(wupb), full(wout), full(gffn),
                  full(wr_hi), full(wr_lo), full(br)],
        out_specs=[pl.BlockSpec((tm, D_MODEL), lambda i: (i, 0)),
                   pl.BlockSpec((tm * ROW_SUB, LANES), lambda i: (i, 0)),
                   pl.BlockSpec((tm, LANES), lambda i: (jnp.maximum(i - 1, 0), 0)),
                   pl.BlockSpec((tm * TOP_K // LANES, LANES), lambda i: (jnp.maximum(i - 1, 0), 0)),
                   pl.BlockSpec((tm * TOP_K // LANES, LANES), lambda i: (jnp.maximum(i - 1, 0), 0)),
                   pl.BlockSpec((1, LANES), lambda i: (0, 0))],
        out_shape=[jax.ShapeDtypeStruct((t + tm, D_MODEL), F32),
                   jax.ShapeDtypeStruct(((t + tm) * ROW_SUB, LANES), F32),
                   jax.ShapeDtypeStruct((t, LANES), F32),
                   jax.ShapeDtypeStruct((t * TOP_K // LANES, LANES), jnp.int32),
                   jax.ShapeDtypeStruct((t * TOP_K // LANES, LANES), jnp.int32),
                   jax.ShapeDtypeStruct((1, LANES), F32)],
        scratch_shapes=[pltpu.VMEM((POOL_HALO, POOL_WIDTH), F32), pltpu.VMEM((1, LANES), F32),
                        pltpu.VMEM((tm, LANES), F32)],
        compiler_params=pltpu.CompilerParams(dimension_semantics=("arbitrary",), vmem_limit_bytes=VMEM_LIMIT),
        name="mix",
    )(x2, on, u, gates, wupa, poolw, pscale, wupb, wout, gffn, wr_hi, wr_lo, br)


def _routing_tables(rank, expert, counts, tg, n_tiles):
    cnt = counts[0, 0:N_EXPERTS].astype(jnp.int32)
    padded = (cnt + (tg - 1)) // tg * tg
    ends = jnp.cumsum(padded)
    offs = ends - padded
    experts = jnp.arange(N_EXPERTS, dtype=jnp.int32)
    dest = jnp.sum(jnp.where(expert[..., None] == experts, offs, 0), axis=-1) + rank
    n_used = ends[N_EXPERTS - 1] // tg
    tile_start = jnp.arange(n_tiles, dtype=jnp.int32) * tg
    tile_expert = jnp.sum((tile_start[:, None] >= ends[None, :]).astype(jnp.int32), axis=1)
    tile_expert = jnp.minimum(tile_expert, N_EXPERTS - 1)
    last = tile_expert[jnp.maximum(n_used - 1, 0)]
    tile_expert = jnp.where(jnp.arange(n_tiles) < n_used, tile_expert, last)
    pad_start = offs + cnt
    pad_len = padded - cnt
    later = (experts[None, :] > experts[:, None]) & (cnt[None, :] > 0)
    next_expert = jnp.min(jnp.where(later, experts[None, :], N_EXPERTS), axis=1)
    next_expert = jnp.where(next_expert == N_EXPERTS, experts, next_expert)
    return (dest, tile_expert.astype(jnp.int32), next_expert.astype(jnp.int32),
            n_used.reshape(1).astype(jnp.int32), pad_start.astype(jnp.int32), pad_len.astype(jnp.int32))


def _dispatch_kernel(pad_start_ref, pad_len_ref, nu_ref, dest_ref, h2_ref, xs_ref, zero_ref, sem, zsem):
    tm = h2_ref.shape[0] // ROW_SUB
    tg = zero_ref.shape[0]
    n_tiles = xs_ref.shape[0] // tg

    @pl.when(pl.program_id(0) == 0)
    def _():
        zero_ref[...] = jnp.zeros_like(zero_ref)

        def pad_runs(e, act):
            first, n = pad_start_ref[e], pad_len_ref[e]
            for bit in reversed(range((tg // ROW_SUB).bit_length() - 1)):
                rows = (1 << bit) * ROW_SUB

                @pl.when((n >> bit) & 1 == 1)
                def _():
                    start = pl.multiple_of((first + ((n >> (bit + 1)) << (bit + 1))) * ROW_SUB, ROW_SUB)
                    act(pltpu.make_async_copy(zero_ref.at[pl.ds(0, rows)], xs_ref.at[pl.ds(start, rows)], zsem))

        def tile_copy(i):
            return pltpu.make_async_copy(zero_ref, xs_ref.at[pl.ds(pl.multiple_of(i * tg, tg), tg)], zsem)

        lax.fori_loop(0, N_EXPERTS, lambda e, cc: (pad_runs(e, lambda cp: cp.start()), cc)[1], 0)
        lax.fori_loop(0, N_EXPERTS, lambda e, cc: (pad_runs(e, lambda cp: cp.wait()), cc)[1], 0)
        lax.fori_loop(nu_ref[0], n_tiles, lambda i, cc: (tile_copy(i).start(), cc)[1], 0)
        lax.fori_loop(nu_ref[0], n_tiles, lambda i, cc: (tile_copy(0).wait(), cc)[1], 0)

    def row_copy(r, d):
        return pltpu.make_async_copy(_row_tile(h2_ref, r), _row_tile(xs_ref, d), sem)

    def start(r, c):
        for kk in range(TOP_K):
            row_copy(r, dest_ref[0, 0, r * TOP_K + kk]).start(priority=kk % 2)
        return c

    lax.fori_loop(0, tm, start, 0, unroll=DMA_UNROLL)
    for kk in range(TOP_K):
        pltpu.make_async_copy(h2_ref, xs_ref.at[pl.ds(0, tm * ROW_SUB)], sem).wait()


def _dispatch(pad_start, pad_len, n_used, dest3, h2, n_rows):
    t = dest3.shape[0] * dest3.shape[2] // TOP_K
    tm = DSP_TM
    return pl.pallas_call(
        _dispatch_kernel,
        grid_spec=pltpu.PrefetchScalarGridSpec(
            num_scalar_prefetch=3,
            grid=(t // tm,),
            in_specs=[pl.BlockSpec((1, 1, tm * TOP_K), lambda i, ps, pn, nu: (i, 0, 0), memory_space=pltpu.SMEM),
                      pl.BlockSpec((tm * ROW_SUB, LANES), lambda i, ps, pn, nu: (i, 0))],
            out_specs=pl.BlockSpec(memory_space=pl.ANY),
            scratch_shapes=[pltpu.VMEM((MOE_TG * ROW_SUB, LANES), F32), pltpu.SemaphoreType.DMA,
                            pltpu.SemaphoreType.DMA]),
        out_shape=jax.ShapeDtypeStruct((n_rows * ROW_SUB, LANES), F32),
        compiler_params=pltpu.CompilerParams(dimension_semantics=("arbitrary",), vmem_limit_bytes=VMEM_LIMIT),
        name="dispatch",
    )(pad_start, pad_len, n_used, dest3, h2)


def _experts_kernel(te_ref, nx_ref, nu_ref, xs_ref, wgu_hbm_ref, bgu_ref, wd_hbm_ref, bd_ref, ys_ref,
                    wgu_f32_ref, wd_f32_ref, wgu_bf_ref, wd_bf_ref, sem):
    i = pl.program_id(0)
    e = te_ref[i]

    def weight_copies(ex):
        return (pltpu.make_async_copy(wgu_hbm_ref.at[ex], wgu_f32_ref, sem.at[0]),
                pltpu.make_async_copy(wd_hbm_ref.at[ex], wd_f32_ref, sem.at[1]))

    @pl.when(i == 0)
    def _():
        for cp in weight_copies(e):
            cp.start()

    @pl.when((i == 0) | (e != te_ref[jnp.maximum(i - 1, 0)]))
    def _():
        for cp in weight_copies(e):
            cp.wait()
        wgu_bf_ref[...] = wgu_f32_ref[...].astype(BF16)
        wd_bf_ref[...] = wd_f32_ref[...].astype(BF16)

        @pl.when(nx_ref[e] != e)
        def _():
            for cp in weight_copies(nx_ref[e]):
                cp.start()

    @pl.when(i < nu_ref[0])
    def _():
        gu = _dot(_load_row_tiles(xs_ref).astype(BF16), wgu_bf_ref[...]) + bgu_ref[...]
        gate = jnp.minimum(gu[:, 0:D_FF], SWIGLU_LIMIT)
        up = jnp.clip(gu[:, D_FF:2 * D_FF], -SWIGLU_LIMIT, SWIGLU_LIMIT)
        act = gate * _sigmoid(SWIGLU_ALPHA * gate) * (up + 1.0)
        _store_row_tiles(ys_ref, _dot(act.astype(BF16), wd_bf_ref[...]) + bd_ref[...])

    @pl.when(i >= nu_ref[0])
    def _():
        ys_ref[...] = jnp.zeros_like(ys_ref)


def _experts(tile_expert, next_expert, n_used, xs, wgu, bgu, wd, bd):
    tg = MOE_TG * ROW_SUB
    n_tiles = xs.shape[0] // tg
    tile = lambda i, te, nx, nu: (jnp.minimum(i, nu[0] - 1), 0)
    expert = lambda i, te, nx, nu: (te[i], 0, 0)
    return pl.pallas_call(
        _experts_kernel,
        grid_spec=pltpu.PrefetchScalarGridSpec(
            num_scalar_prefetch=3,
            grid=(n_tiles,),
            in_specs=[pl.BlockSpec((tg, LANES), tile),
                      pl.BlockSpec(memory_space=pl.ANY),
                      pl.BlockSpec((None, 1, 2 * D_FF), expert),
                      pl.BlockSpec(memory_space=pl.ANY),
                      pl.BlockSpec((None, 1, D_MODEL), expert)],
            out_specs=pl.BlockSpec((tg, LANES), lambda i, te, nx, nu: (i, 0)),
            scratch_shapes=[pltpu.VMEM((D_MODEL, 2 * D_FF), F32), pltpu.VMEM((D_FF, D_MODEL), F32),
                            pltpu.VMEM((D_MODEL, 2 * D_FF), BF16), pltpu.VMEM((D_FF, D_MODEL), BF16),
                            pltpu.SemaphoreType.DMA((2,))]),
        out_shape=jax.ShapeDtypeStruct(xs.shape, F32),
        compiler_params=pltpu.CompilerParams(dimension_semantics=("arbitrary",), vmem_limit_bytes=VMEM_LIMIT),
        name="experts",
    )(tile_expert, next_expert, n_used, xs, wgu, bgu, wd, bd)


def _combine_kernel(dest_ref, dest_next_ref, route_ref, x1_ref, gfin_ref, ys_ref, out_ref, ybuf_ref, sem):
    i = pl.program_id(0)
    tm = x1_ref.shape[0]

    def gather_rows(dref, buf):
        def start(r, c):
            for kk in range(TOP_K):
                pltpu.make_async_copy(_row_tile(ys_ref, dref[0, 0, r * TOP_K + kk]),
                                      _row_tile(ybuf_ref.at[buf, kk], r), sem.at[buf]).start(priority=kk % 2)
            return c
        lax.fori_loop(0, tm, start, 0, unroll=DMA_UNROLL)

    @pl.when(i == 0)
    def _():
        gather_rows(dest_ref, 0)

    @pl.when(i + 1 < pl.num_programs(0))
    def _():
        gather_rows(dest_next_ref, (i + 1) % 2)

    buf = i % 2
    for kk in range(TOP_K):
        pltpu.make_async_copy(ys_ref.at[pl.ds(0, tm * ROW_SUB)], ybuf_ref.at[buf, kk], sem.at[buf]).wait()
    route = route_ref[...]
    acc = x1_ref[...]
    for kk in range(TOP_K):
        acc = acc + route[:, kk:kk + 1] * _load_row_tiles(ybuf_ref.at[buf, kk])
    out_ref[...] = _rms(acc, gfin_ref[...])


def _combine(dest3, route, x1, gfin, ys):
    t = route.shape[0]
    tm = CMB_TM
    last = t // tm - 1
    return pl.pallas_call(
        _combine_kernel,
        grid=(t // tm,),
        in_specs=[pl.BlockSpec((1, 1, tm * TOP_K), lambda i: (i, 0, 0), memory_space=pltpu.SMEM),
                  pl.BlockSpec((1, 1, tm * TOP_K), lambda i: (jnp.minimum(i + 1, last), 0, 0),
                               memory_space=pltpu.SMEM),
                  pl.BlockSpec((tm, LANES), lambda i: (i, 0)),
                  pl.BlockSpec((tm, D_MODEL), lambda i: (i, 0)),
                  pl.BlockSpec((1, D_MODEL), lambda i: (0, 0)),
                  pl.BlockSpec(memory_space=pl.ANY)],
        out_specs=pl.BlockSpec((tm, D_MODEL), lambda i: (i, 0)),
        out_shape=jax.ShapeDtypeStruct((t, D_MODEL), F32),
        scratch_shapes=[pltpu.VMEM((2, TOP_K, tm * ROW_SUB, LANES), F32), pltpu.SemaphoreType.DMA((2,))],
        compiler_params=pltpu.CompilerParams(dimension_semantics=("arbitrary",), vmem_limit_bytes=VMEM_LIMIT),
        name="combine",
    )(dest3, dest3, route, x1, gfin, ys)


def kernel(x, g_mix, w_in, conv_w, a_log, dt_bias, dn_norm, w_up_a, pool_w, pool_scale, w_up_b, w_out, g_ffn,
           w_router, b_router, w_gate_up, b_gate_up, w_down, b_down, g_final):
    batch, seq, d = x.shape
    assert d == D_MODEL and seq % DN_LB == 0 and seq % MIX_TM == 0
    assert (batch * seq) % DSP_TM == 0 and (batch * seq * TOP_K) % MOE_TG == 0
    assert g_mix.shape[0] == 1, "one layer"
    t = batch * seq
    x2 = x.reshape(t, d)

    w = w_in[0]
    o_z = 3 * DN_WIDTH
    o_b = o_z + DN_WIDTH
    o_a = o_b + DN_HEADS
    o_u = o_a + DN_HEADS
    o_g = o_u + POOL_WIDTH
    wqkv = w[:, 0:o_z].astype(BF16)
    wz = w[:, o_z:o_b].astype(BF16)
    wu = w[:, o_u:o_g].astype(BF16)
    wg = w[:, o_g:].astype(BF16)
    w_b = w[:, o_b:o_a]
    w_a = w[:, o_a:o_u]
    pad = jnp.zeros((d, LANES - DN_HEADS), F32)
    wba = jnp.concatenate([w_b, pad, w_a, pad], axis=1).astype(BF16)
    wbat = jnp.concatenate([w_b, w_a], axis=1).T.astype(BF16)

    qkv, z, u, gates, ba, bat = _in_proj(x2, g_mix, wqkv, wz, wu, wg, wba, wbat, conv_w[0], seq)

    lane_pad = lambda p: jnp.pad(p.reshape(1, DN_HEADS), ((0, 0), (0, LANES - DN_HEADS)))
    on = _deltanet(qkv, z, ba, bat, lane_pad(a_log[0]), lane_pad(dt_bias[0]),
                   a_log[0].reshape(DN_HEADS, 1), dt_bias[0].reshape(DN_HEADS, 1),
                   jnp.tile(dn_norm[0], DN_HEADS).reshape(1, DN_WIDTH), batch, seq)

    wr = jnp.pad(w_router[0], ((0, 0), (0, LANES - N_EXPERTS)))
    wr_hi = wr.astype(BF16)
    wr_lo = (wr - wr_hi.astype(F32)).astype(BF16)
    br = jnp.pad(b_router[0].reshape(1, N_EXPERTS), ((0, 0), (0, LANES - N_EXPERTS)))
    x1, h2, route, rank, expert, counts = _mix(x2, on, u, gates, w_up_a[0].astype(BF16), pool_w[0].astype(BF16),
                                                pool_scale[0].reshape(1, POOL_WIDTH), w_up_b[0].astype(BF16),
                                                w_out[0].astype(BF16), g_ffn, wr_hi, wr_lo, br, batch, seq)

    n_tiles = t * TOP_K // MOE_TG + N_EXPERTS
    dest, tile_expert, next_expert, n_used, pad_start, pad_len = _routing_tables(rank, expert, counts, MOE_TG,
                                                                                 n_tiles)
    xs = _dispatch(pad_start, pad_len, n_used, dest.reshape(t // DSP_TM, 1, DSP_TM * TOP_K), h2, n_tiles * MOE_TG)
    ys = _experts(tile_expert, next_expert, n_used, xs, w_gate_up[0],
                  b_gate_up[0].reshape(N_EXPERTS, 1, 2 * D_FF), w_down[0], b_down[0].reshape(N_EXPERTS, 1, D_MODEL))
    out = _combine(dest.reshape(t // CMB_TM, 1, CMB_TM * TOP_K), route, x1, g_final.reshape(1, D_MODEL), ys)
    return out.reshape(batch, seq, d)
```

```python
import functools

import jax
import jax.numpy as jnp
from jax import lax
from jax.experimental import pallas as pl
from jax.experimental.pallas import tpu as pltpu

F32 = jnp.float32
BF16 = jnp.bfloat16

D_MODEL = 1024
CHUNK = 64
DN_HEADS = 8
DN_HEAD_DIM = 64
DN_WIDTH = DN_HEADS * DN_HEAD_DIM
CONV_WIDTH = 4
POOL_GROUPS = 4
POOL_WINDOWS = (2, 4, 8, 16)
POOL_WIDTH = 512
POOL_GROUP_DIM = 128
POOL_HALO = 16
N_EXPERTS = 32
TOP_K = 4
D_FF = D_MODEL
SWIGLU_LIMIT = 7.0
SWIGLU_ALPHA = 1.702
NORM_EPS = 1e-6
LANES = 128
SUBLANES = 8
VMEM_LIMIT = 56 * 1024 * 1024

IN_TM = 512
DN_LB = 512
INTRA_GROUP = 3
MIX_TM = 512
MOE_TG = 512
DSP_TM = 2048
CMB_TM = 512
DMA_UNROLL = 8


def _dot(a, b):
    return jnp.dot(a, b, preferred_element_type=F32)


def _dot_nt(a, b):
    return lax.dot_general(a, b, (((1,), (1,)), ((), ())), preferred_element_type=F32)


def _split2(x):
    hi = x.astype(BF16)
    lo = (x - hi.astype(F32)).astype(BF16)
    return hi, lo


def _dot_exact_rhs(x, m):
    hi, lo = _split2(x)
    return _dot(hi, m) + _dot(lo, m)


def _dot_exact_lhs(m, x):
    hi, lo = _split2(x)
    return _dot(m, hi) + _dot(m, lo)


def _softplus(x):
    return jnp.maximum(x, 0.0) + jnp.log1p(jnp.exp(-jnp.abs(x)))


def _sigmoid(x):
    return 1.0 / (1.0 + jnp.exp(-x))


def _rms(x, g):
    return x * lax.rsqrt(jnp.mean(x * x, axis=-1, keepdims=True) + NORM_EPS) * g


ROW_SUB = D_MODEL // LANES


def _row_tile(ref, r):
    return ref.at[pl.ds(pl.multiple_of(r * ROW_SUB, ROW_SUB), ROW_SUB)]


def _store_row_tiles(ref, x):
    for j in range(ROW_SUB):
        ref[pl.ds(j, x.shape[0], stride=ROW_SUB), :] = x[:, j * LANES:(j + 1) * LANES]


def _load_row_tiles(ref):
    n = ref.shape[0] // ROW_SUB
    return jnp.concatenate([ref[pl.ds(j, n, stride=ROW_SUB), :] for j in range(ROW_SUB)], axis=1)


def _in_proj_kernel(x_ref, g_ref, wqkv_ref, wz_ref, wu_ref, wg_ref, wba_ref, wbat_ref, convw_ref,
                    qkv_ref, z_ref, u_ref, gates_ref, ba_ref, bat_ref, hist_ref, *, steps_per_seq):
    i = pl.program_id(0)
    tm = x_ref.shape[0]

    @pl.when(i == 0)
    def _():
        hist_ref[...] = jnp.zeros_like(hist_ref)

    @pl.when((i + steps_per_seq - 1) % steps_per_seq == 0)
    def _():
        hist_ref[0:SUBLANES, :] = jnp.zeros((SUBLANES, hist_ref.shape[1]), F32)

    cw = convw_ref[...]
    prev = hist_ref[SUBLANES:SUBLANES + tm, :]
    act = prev * cw[CONV_WIDTH - 1:CONV_WIDTH]
    for s in range(1, CONV_WIDTH):
        act = act + hist_ref[pl.ds(SUBLANES - s, tm), :] * cw[CONV_WIDTH - 1 - s:CONV_WIDTH - s]
    qkv_ref[...] = act * _sigmoid(act)

    hb = _rms(x_ref[...], g_ref[...]).astype(BF16)
    hist_ref[0:SUBLANES, :] = prev[tm - SUBLANES:tm]
    hist_ref[SUBLANES:SUBLANES + tm, :] = _dot(hb, wqkv_ref[...])
    z_ref[...] = _dot(hb, wz_ref[...])
    u_ref[...] = _dot(hb, wu_ref[...])
    gates_ref[...] = _dot(hb, wg_ref[...])
    ba_ref[...] = _dot(hb, wba_ref[...])
    bat_ref[...] = _dot_nt(wbat_ref[...], hb)


def _in_proj(x2, g_mix, wqkv, wz, wu, wg, wba, wbat, conv_w, seq):
    t = x2.shape[0]
    tm = IN_TM
    n = t // tm
    full = lambda a: pl.BlockSpec(a.shape, lambda i: (0, 0))
    row = lambda c: pl.BlockSpec((tm, c), lambda i: (jnp.minimum(i, n - 1), 0))
    return pl.pallas_call(
        functools.partial(_in_proj_kernel, steps_per_seq=seq // tm),
        grid=(n + 1,),
        in_specs=[row(D_MODEL), full(g_mix), full(wqkv), full(wz), full(wu), full(wg), full(wba), full(wbat),
                  full(conv_w)],
        out_specs=[pl.BlockSpec((tm, 3 * DN_WIDTH), lambda i: (jnp.maximum(i - 1, 0), 0)),
                   row(DN_WIDTH), row(POOL_WIDTH), row(2 * D_MODEL), row(2 * LANES),
                   pl.BlockSpec((2 * SUBLANES, tm), lambda i: (0, jnp.minimum(i, n - 1)))],
        out_shape=[jax.ShapeDtypeStruct((t, 3 * DN_WIDTH), F32),
                   jax.ShapeDtypeStruct((t, DN_WIDTH), F32),
                   jax.ShapeDtypeStruct((t, POOL_WIDTH), F32),
                   jax.ShapeDtypeStruct((t, 2 * D_MODEL), F32),
                   jax.ShapeDtypeStruct((t, 2 * LANES), F32),
                   jax.ShapeDtypeStruct((2 * SUBLANES, t), F32)],
        scratch_shapes=[pltpu.VMEM((SUBLANES + tm, 3 * DN_WIDTH), F32)],
        compiler_params=pltpu.CompilerParams(dimension_semantics=("arbitrary",), vmem_limit_bytes=VMEM_LIMIT),
        name="in_proj",
    )(x2, g_mix, wqkv, wz, wu, wg, wba, wbat, conv_w)


def _deltanet_kernel(qkv_ref, z_ref, ba_ref, bat_ref, alog_r_ref, dtb_r_ref, alog_c_ref, dtb_c_ref, dnw_ref,
                     head_ones_ref, tril_ref, triu_ref, chunk_ones_ref, expand_ref, o_ref,
                     s_ref, qn_ref, kn_ref, kbe_ref, vb_ref, qg_ref, kd_ref,
                     xbeta_ref, xgc_ref, xgl_ref, gr_ref, oacc_ref, qkd_ref, wq_ref, u_ref, kdt_ref, gl_ref):
    lb = qkv_ref.shape[0]
    n_chunks = lb // CHUNK

    @pl.when(pl.program_id(1) == 0)
    def _():
        s_ref[...] = jnp.zeros_like(s_ref)

    q = qkv_ref[:, 0:DN_WIDTH]
    k = qkv_ref[:, DN_WIDTH:2 * DN_WIDTH]
    v = qkv_ref[:, 2 * DN_WIDTH:3 * DN_WIDTH]

    head_ones = head_ones_ref[...]
    qn = q * lax.rsqrt(_dot_exact_rhs(q * q, head_ones) + NORM_EPS) * (DN_HEAD_DIM ** -0.5)
    kn = k * lax.rsqrt(_dot_exact_rhs(k * k, head_ones) + NORM_EPS)

    ba = ba_ref[...]
    beta_c = _sigmoid(ba[:, 0:LANES])
    g_c = -jnp.exp(alog_r_ref[...]) * _softplus(ba[:, LANES:2 * LANES] + dtb_r_ref[...])
    lane = lax.broadcasted_iota(jnp.int32, (lb, LANES), 1)
    g_c = jnp.where(lane < DN_HEADS, g_c, 0.0)
    bat = bat_ref[...]
    g_r = -jnp.exp(alog_c_ref[...]) * _softplus(bat[SUBLANES:2 * SUBLANES] + dtb_c_ref[...])

    gc = _dot_exact_lhs(tril_ref[...], g_c)
    gtot = _dot_exact_lhs(chunk_ones_ref[...], g_c)
    gr = _dot_exact_rhs(g_r, triu_ref[...])
    for c in range(n_chunks):
        gr_ref[c] = gr[:, c * CHUNK:(c + 1) * CHUNK]

    expand = expand_ref[...]
    x_beta = _dot_exact_rhs(beta_c, expand)
    x_gc = _dot_exact_rhs(gc, expand)
    x_gtot = _dot_exact_rhs(gtot, expand)
    x_eg = jnp.exp(x_gc)
    xbeta_ref[...] = x_beta
    xgc_ref[...] = x_gc
    xgl_ref[...] = jnp.exp(x_gtot)
    qn_ref[...] = qn.astype(BF16)
    kn_ref[...] = kn.astype(BF16)
    kbe_ref[...] = (kn * (x_beta * x_eg)).astype(BF16)
    vb_ref[...] = (v * x_beta).astype(BF16)
    qg_ref[...] = (qn * x_eg).astype(BF16)
    kd_ref[...] = (kn * jnp.exp(x_gtot - x_gc)).astype(BF16)

    ci = lax.broadcasted_iota(jnp.int32, (CHUNK, CHUNK), 0)
    cj = lax.broadcasted_iota(jnp.int32, (CHUNK, CHUNK), 1)
    eye = (ci == cj).astype(F32)

    heads = range(DN_HEADS)
    lanes = [pl.ds(h * DN_HEAD_DIM, DN_HEAD_DIM) for h in heads]
    stack = lambda top, bot: jnp.concatenate([top, bot], axis=0)

    chunk_rows = lambda c: pl.ds(c * CHUNK, CHUNK)

    def within_chunks(chunk_ids):
        chains = [(c, h) for c in chunk_ids for h in heads]
        ids = range(len(chains))
        rows = [chunk_rows(c) for c, _ in chains]
        ln = [lanes[h] for _, h in chains]
        kb = [kn_ref[rows[i], ln[i]] for i in ids]
        kq = [_dot_nt(stack(kb[i], qn_ref[rows[i], ln[i]]), kb[i]) for i in ids]
        yield
        decay = [jnp.exp(jnp.where(ci >= cj, xgc_ref[rows[i], ln[i]] - gr_ref[c, h:h + 1, :], -jnp.inf))
                 for i, (c, h) in enumerate(chains)]
        a = [jnp.where(ci > cj, xbeta_ref[rows[i], ln[i]] * kq[i][0:CHUNK] * decay[i], 0.0) for i in ids]
        slot = [c * DN_HEADS + h for c, h in chains]
        for i in ids:
            qkd_ref[slot[i]] = (kq[i][CHUNK:2 * CHUNK] * decay[i]).astype(BF16)
            wq_ref[slot[i], CHUNK:2 * CHUNK, :] = qg_ref[rows[i], ln[i]]
            kdt_ref[slot[i]] = kd_ref[rows[i], ln[i]].T.astype(BF16)
            gl_ref[slot[i]] = xgl_ref[rows[i], ln[i]]
        t_inv = [eye - a[i] for i in ids]
        pw = [a[i].astype(BF16) for i in ids]
        pw = [_dot(pw[i], pw[i]).astype(BF16) for i in ids]
        yield
        for _ in range(4):
            r = [_dot(stack(t_inv[i].astype(BF16), pw[i]), pw[i]) for i in ids]
            yield
            t_inv = [t_inv[i] + r[i][0:CHUNK] for i in ids]
            pw = [r[i][CHUNK:2 * CHUNK].astype(BF16) for i in ids]
        t_inv = [t_inv[i] + _dot(t_inv[i].astype(BF16), pw[i]) for i in ids]
        yield
        tb = [t_inv[i].astype(BF16) for i in ids]
        for i in ids:
            wq_ref[slot[i], 0:CHUNK, :] = _dot(tb[i], kbe_ref[rows[i], ln[i]]).astype(BF16)
        yield
        for i in ids:
            u_ref[slot[i]] = _dot(tb[i], vb_ref[rows[i], ln[i]])
        yield

    def state_chunks(chunk_ids):
        for c in chunk_ids:
            rows = chunk_rows(c)
            slot = [c * DN_HEADS + h for h in heads]
            s = [s_ref[h] for h in heads]
            ws = [_dot(wq_ref[slot[h]], s[h].astype(BF16)) for h in heads]
            yield
            vnb = [(u_ref[slot[h]] - ws[h][0:CHUNK]).astype(BF16) for h in heads]
            for h in heads:
                oacc_ref[h, rows, :] = ws[h][CHUNK:2 * CHUNK] + _dot(qkd_ref[slot[h]], vnb[h])
            yield
            for h in heads:
                s_ref[h] = s[h] * gl_ref[slot[h]] + _dot(kdt_ref[slot[h]], vnb[h])
            yield

    groups = [list(range(g, min(g + INTRA_GROUP, n_chunks))) for g in range(0, n_chunks, INTRA_GROUP)]
    pending = iter(())
    for grp in groups:
        for _ in within_chunks(grp):
            next(pending, None)
        for _ in pending:
            pass
        pending = state_chunks(grp)
    for _ in pending:
        pass

    o = jnp.concatenate([oacc_ref[h] for h in heads], axis=1)
    ms = _dot_exact_rhs(o * o, head_ones) * (1.0 / DN_HEAD_DIM)
    z = z_ref[...]
    o_ref[...] = (o * lax.rsqrt(ms + NORM_EPS) * dnw_ref[...] * (z * _sigmoid(z))).astype(o_ref.dtype)


def _deltanet_masks(lb):
    head = jnp.arange(DN_WIDTH) // DN_HEAD_DIM
    head_ones = head[:, None] == head[None, :]
    pos = jnp.arange(lb)
    same = (pos[:, None] // CHUNK) == (pos[None, :] // CHUNK)
    tril = same & (pos[None, :] <= pos[:, None])
    triu = same & (pos[:, None] <= pos[None, :])
    expand = jnp.arange(LANES)[:, None] == head[None, :]
    return tuple(m.astype(BF16) for m in (head_ones, tril, triu, same, expand))


def _deltanet(qkv, z, ba, bat, alog_r, dtb_r, alog_c, dtb_c, dnw, batch, seq):
    lb = DN_LB
    nb = seq // lb
    slots = lb // CHUNK * DN_HEADS
    masks = _deltanet_masks(lb)
    full = lambda a: pl.BlockSpec(a.shape, lambda b, j: (0,) * a.ndim)
    row = lambda n: pl.BlockSpec((lb, n), lambda b, j: (b * nb + j, 0))
    return pl.pallas_call(
        _deltanet_kernel,
        grid=(batch, nb),
        in_specs=[row(3 * DN_WIDTH), row(DN_WIDTH), row(2 * LANES),
                  pl.BlockSpec((2 * SUBLANES, lb), lambda b, j: (0, b * nb + j)),
                  full(alog_r), full(dtb_r), full(alog_c), full(dtb_c), full(dnw)] + [full(m) for m in masks],
        out_specs=row(DN_WIDTH),
        out_shape=jax.ShapeDtypeStruct((batch * seq, DN_WIDTH), BF16),
        scratch_shapes=[
            pltpu.VMEM((DN_HEADS, DN_HEAD_DIM, DN_HEAD_DIM), F32),
            pltpu.VMEM((lb, DN_WIDTH), BF16),
            pltpu.VMEM((lb, DN_WIDTH), BF16),
            pltpu.VMEM((lb, DN_WIDTH), BF16),
            pltpu.VMEM((lb, DN_WIDTH), BF16),
            pltpu.VMEM((lb, DN_WIDTH), BF16),
            pltpu.VMEM((lb, DN_WIDTH), BF16),
            pltpu.VMEM((lb, DN_WIDTH), F32),
            pltpu.VMEM((lb, DN_WIDTH), F32),
            pltpu.VMEM((lb, DN_WIDTH), F32),
            pltpu.VMEM((lb // CHUNK, DN_HEADS, CHUNK), F32),
            pltpu.VMEM((DN_HEADS, lb, DN_HEAD_DIM), F32),
            pltpu.VMEM((slots, CHUNK, CHUNK), BF16),
            pltpu.VMEM((slots, 2 * CHUNK, DN_HEAD_DIM), BF16),
            pltpu.VMEM((slots, CHUNK, DN_HEAD_DIM), F32),
            pltpu.VMEM((slots, DN_HEAD_DIM, CHUNK), BF16),
            pltpu.VMEM((slots, DN_HEAD_DIM, DN_HEAD_DIM), F32),
        ],
        compiler_params=pltpu.CompilerParams(dimension_semantics=("arbitrary", "arbitrary"),
                                             vmem_limit_bytes=VMEM_LIMIT),
        name="deltanet",
    )(qkv, z, ba, bat, alog_r, dtb_r, alog_c, dtb_c, dnw, *masks)


def _mix_kernel(x_ref, on_ref, u_ref, gates_ref, wupa_ref, poolw_ref, pscale_ref, wupb_ref, wout_ref,
                gffn_ref, wr_hi_ref, wr_lo_ref, br_ref,
                x1_ref, h2_ref, route_ref, rank_ref, expert_ref, counts_ref,
                ucarry_ref, ecount_ref, logits_ref, *, steps_per_seq):
    i = pl.program_id(0)
    tm = x_ref.shape[0]

    @pl.when(i == 0)
    def _():
        ecount_ref[...] = jnp.zeros_like(ecount_ref)
        logits_ref[...] = jnp.zeros_like(logits_ref)

    @pl.when(i % steps_per_seq == 0)
    def _():
        ucarry_ref[...] = jnp.zeros_like(ucarry_ref)

    lane = lax.broadcasted_iota(jnp.int32, (tm, LANES), 1)
    lg = jnp.where(lane < N_EXPERTS, logits_ref[...], -jnp.inf)
    vals, idxs, sels = [], [], []
    for _ in range(TOP_K):
        m = jnp.max(lg, axis=-1, keepdims=True)
        idx = jnp.min(jnp.where(lg == m, lane, LANES), axis=-1, keepdims=True)
        sel = lane == idx
        vals.append(m)
        idxs.append(idx)
        sels.append(sel)
        lg = jnp.where(sel, -jnp.inf, lg)
    es = [jnp.exp(vk - vals[0]) for vk in vals]
    denom = es[0] + es[1] + es[2] + es[3]
    chosen = jnp.zeros((tm, LANES), F32)
    for sel in sels:
        chosen = chosen + jnp.where(sel, 1.0, 0.0)
    chosen = jnp.where(i > 0, chosen, 0.0)

    u = u_ref[...]
    ext = jnp.concatenate([ucarry_ref[...], u], axis=0)
    ucarry_ref[...] = u[tm - POOL_HALO:tm]
    t_pos = (i % steps_per_seq) * tm + lax.broadcasted_iota(jnp.int32, (tm, POOL_GROUP_DIM), 0)
    ys = []
    for g in range(POOL_GROUPS):
        s = ext[:, g * POOL_GROUP_DIM:(g + 1) * POOL_GROUP_DIM]
        shift = 1
        while shift < POOL_WINDOWS[g]:
            s = s + pltpu.roll(s, shift, 0)
            shift *= 2
        count = jnp.minimum(t_pos + 1, POOL_WINDOWS[g]).astype(F32)
        pooled = s[POOL_HALO:] / count - u[:, g * POOL_GROUP_DIM:(g + 1) * POOL_GROUP_DIM]
        ys.append(_dot(pooled.astype(BF16), poolw_ref[g]))
    yb = jnp.concatenate(ys, axis=-1) * pscale_ref[...]
    y_b = _dot(yb.astype(BF16), wupb_ref[...])
    y_a = _dot(on_ref[...], wupa_ref[...])
    gates = gates_ref[...]
    merged = _sigmoid(gates[:, 0:D_MODEL]) * y_a + _sigmoid(gates[:, D_MODEL:2 * D_MODEL]) * y_b
    x1 = x_ref[...] + _dot(merged.astype(BF16), wout_ref[...])
    x1_ref[...] = x1

    h2 = _rms(x1, gffn_ref[...])
    _store_row_tiles(h2_ref, h2)

    hi, lo = _split2(h2)
    logits_ref[...] = (_dot(hi, wr_hi_ref[...]) + _dot(lo, wr_hi_ref[...]) + _dot(hi, wr_lo_ref[...])
                       + br_ref[...])

    ri = lax.broadcasted_iota(jnp.int32, (tm, tm), 0)
    rj = lax.broadcasted_iota(jnp.int32, (tm, tm), 1)
    before = (rj < ri).astype(BF16)
    pos = ecount_ref[...] + _dot(before, chosen.astype(BF16))
    ecount_ref[...] = ecount_ref[...] + jnp.sum(chosen, axis=0, keepdims=True)
    counts_ref[...] = ecount_ref[...]

    per_row = LANES // TOP_K
    first_lane = (lax.broadcasted_iota(jnp.int32, (tm, LANES), 0) % per_row) * TOP_K
    weights = jnp.zeros((tm, LANES), F32)
    flat_rank = jnp.zeros((tm, LANES), F32)
    flat_expert = jnp.zeros((tm, LANES), F32)
    for kk in range(TOP_K):
        rank = jnp.sum(jnp.where(sels[kk], pos, 0.0), axis=-1, keepdims=True)
        weights = jnp.where(lane == kk, es[kk] / denom, weights)
        flat_rank = jnp.where(lane == first_lane + kk, rank, flat_rank)
        flat_expert = jnp.where(lane == first_lane + kk, idxs[kk].astype(F32), flat_expert)
    route_ref[...] = weights
    fold = lambda a: jnp.sum(a.reshape(tm // per_row, per_row, LANES), axis=1).astype(jnp.int32)
    rank_ref[...] = fold(flat_rank)
    expert_ref[...] = fold(flat_expert)


def _mix(x2, on, u, gates, wupa, poolw, pscale, wupb, wout, gffn, wr_hi, wr_lo, br, batch, seq):
    tm = MIX_TM
    t = batch * seq
    n = t // tm
    full = lambda a: pl.BlockSpec(a.shape, lambda i: (0,) * a.ndim)
    row = lambda c: pl.BlockSpec((tm, c), lambda i: (jnp.minimum(i, n - 1), 0))
    return pl.pallas_call(
        functools.partial(_mix_kernel, steps_per_seq=seq // tm),
        grid=(n + 1,),
        in_specs=[row(D_MODEL), row(DN_WIDTH), row(POOL_WIDTH), row(2 * D_MODEL),
                  full(wupa), full(poolw), full(pscale), full(wupb), full(wout), full(gffn),
                  full(wr_hi), full(wr_lo), full(br)],
        out_specs=[pl.BlockSpec((tm, D_MODEL), lambda i: (i, 0)),
                   pl.BlockSpec((tm * ROW_SUB, LANES), lambda i: (i, 0)),
                   pl.BlockSpec((tm, LANES), lambda i: (jnp.maximum(i - 1, 0), 0)),
                   pl.BlockSpec((tm * TOP_K // LANES, LANES), lambda i: (jnp.maximum(i - 1, 0), 0)),
                   pl.BlockSpec((tm * TOP_K // LANES, LANES), lambda i: (jnp.maximum(i - 1, 0), 0)),
                   pl.BlockSpec((1, LANES), lambda i: (0, 0))],
        out_shape=[jax.ShapeDtypeStruct((t + tm, D_MODEL), F32),
                   jax.ShapeDtypeStruct(((t + tm) * ROW_SUB, LANES), F32),
                   jax.ShapeDtypeStruct((t, LANES), F32),
                   jax.ShapeDtypeStruct((t * TOP_K // LANES, LANES), jnp.int32),
                   jax.ShapeDtypeStruct((t * TOP_K // LANES, LANES), jnp.int32),
                   jax.ShapeDtypeStruct((1, LANES), F32)],
        scratch_shapes=[pltpu.VMEM((POOL_HALO, POOL_WIDTH), F32), pltpu.VMEM((1, LANES), F32),
                        pltpu.VMEM((tm, LANES), F32)],
        compiler_params=pltpu.CompilerParams(dimension_semantics=("arbitrary",), vmem_limit_bytes=VMEM_LIMIT),
        name="mix",
    )(x2, on, u, gates, wupa, poolw, pscale, wupb, wout, gffn, wr_hi, wr_lo, br)


def _routing_tables(rank, expert, counts, tg, n_tiles):
    cnt = counts[0, 0:N_EXPERTS].astype(jnp.int32)
    padded = (cnt + (tg - 1)) // tg * tg
    ends = jnp.cumsum(padded)
    offs = ends - padded
    experts = jnp.arange(N_EXPERTS, dtype=jnp.int32)
    dest = jnp.sum(jnp.where(expert[..., None] == experts, offs, 0), axis=-1) + rank
    n_used = ends[N_EXPERTS - 1] // tg
    tile_start = jnp.arange(n_tiles, dtype=jnp.int32) * tg
    tile_expert = jnp.sum((tile_start[:, None] >= ends[None, :]).astype(jnp.int32), axis=1)
    tile_expert = jnp.minimum(tile_expert, N_EXPERTS - 1)
    last = tile_expert[jnp.maximum(n_used - 1, 0)]
    tile_expert = jnp.where(jnp.arange(n_tiles) < n_used, tile_expert, last)
    pad_start = offs + cnt
    pad_len = padded - cnt
    later = (experts[None, :] > experts[:, None]) & (cnt[None, :] > 0)
    next_expert = jnp.min(jnp.where(later, experts[None, :], N_EXPERTS), axis=1)
    next_expert = jnp.where(next_expert == N_EXPERTS, experts, next_expert)
    return (dest, tile_expert.astype(jnp.int32), next_expert.astype(jnp.int32),
            n_used.reshape(1).astype(jnp.int32), pad_start.astype(jnp.int32), pad_len.astype(jnp.int32))


def _dispatch_kernel(pad_start_ref, pad_len_ref, nu_ref, dest_ref, h2_ref, xs_ref, zero_ref, sem, zsem):
    tm = h2_ref.shape[0] // ROW_SUB
    tg = zero_ref.shape[0]
    n_tiles = xs_ref.shape[0] // tg

    @pl.when(pl.program_id(0) == 0)
    def _():
        zero_ref[...] = jnp.zeros_like(zero_ref)

        def pad_runs(e, act):
            first, n = pad_start_ref[e], pad_len_ref[e]
            for bit in reversed(range((tg // ROW_SUB).bit_length() - 1)):
                rows = (1 << bit) * ROW_SUB

                @pl.when((n >> bit) & 1 == 1)
                def _():
                    start = pl.multiple_of((first + ((n >> (bit + 1)) << (bit + 1))) * ROW_SUB, ROW_SUB)
                    act(pltpu.make_async_copy(zero_ref.at[pl.ds(0, rows)], xs_ref.at[pl.ds(start, rows)], zsem))

        def tile_copy(i):
            return pltpu.make_async_copy(zero_ref, xs_ref.at[pl.ds(pl.multiple_of(i * tg, tg), tg)], zsem)

        lax.fori_loop(0, N_EXPERTS, lambda e, cc: (pad_runs(e, lambda cp: cp.start()), cc)[1], 0)
        lax.fori_loop(0, N_EXPERTS, lambda e, cc: (pad_runs(e, lambda cp: cp.wait()), cc)[1], 0)
        lax.fori_loop(nu_ref[0], n_tiles, lambda i, cc: (tile_copy(i).start(), cc)[1], 0)
        lax.fori_loop(nu_ref[0], n_tiles, lambda i, cc: (tile_copy(0).wait(), cc)[1], 0)

    def row_copy(r, d):
        return pltpu.make_async_copy(_row_tile(h2_ref, r), _row_tile(xs_ref, d), sem)

    def start(r, c):
        for kk in range(TOP_K):
            row_copy(r, dest_ref[0, 0, r * TOP_K + kk]).start(priority=kk % 2)
        return c

    lax.fori_loop(0, tm, start, 0, unroll=DMA_UNROLL)
    for kk in range(TOP_K):
        pltpu.make_async_copy(h2_ref, xs_ref.at[pl.ds(0, tm * ROW_SUB)], sem).wait()


def _dispatch(pad_start, pad_len, n_used, dest3, h2, n_rows):
    t = dest3.shape[0] * dest3.shape[2] // TOP_K
    tm = DSP_TM
    return pl.pallas_call(
        _dispatch_kernel,
        grid_spec=pltpu.PrefetchScalarGridSpec(
            num_scalar_prefetch=3,
            grid=(t // tm,),
            in_specs=[pl.BlockSpec((1, 1, tm * TOP_K), lambda i, ps, pn, nu: (i, 0, 0), memory_space=pltpu.SMEM),
                      pl.BlockSpec((tm * ROW_SUB, LANES), lambda i, ps, pn, nu: (i, 0))],
            out_specs=pl.BlockSpec(memory_space=pl.ANY),
            scratch_shapes=[pltpu.VMEM((MOE_TG * ROW_SUB, LANES), F32), pltpu.SemaphoreType.DMA,
                            pltpu.SemaphoreType.DMA]),
        out_shape=jax.ShapeDtypeStruct((n_rows * ROW_SUB, LANES), F32),
        compiler_params=pltpu.CompilerParams(dimension_semantics=("arbitrary",), vmem_limit_bytes=VMEM_LIMIT),
        name="dispatch",
    )(pad_start, pad_len, n_used, dest3, h2)


def _experts_kernel(te_ref, nx_ref, nu_ref, xs_ref, wgu_hbm_ref, bgu_ref, wd_hbm_ref, bd_ref, ys_ref,
                    wgu_f32_ref, wd_f32_ref, wgu_bf_ref, wd_bf_ref, sem):
    i = pl.program_id(0)
    e = te_ref[i]

    def weight_copies(ex):
        return (pltpu.make_async_copy(wgu_hbm_ref.at[ex], wgu_f32_ref, sem.at[0]),
                pltpu.make_async_copy(wd_hbm_ref.at[ex], wd_f32_ref, sem.at[1]))

    @pl.when(i == 0)
    def _():
        for cp in weight_copies(e):
            cp.start()

    @pl.when((i == 0) | (e != te_ref[jnp.maximum(i - 1, 0)]))
    def _():
        for cp in weight_copies(e):
            cp.wait()
        wgu_bf_ref[...] = wgu_f32_ref[...].astype(BF16)
        wd_bf_ref[...] = wd_f32_ref[...].astype(BF16)

        @pl.when(nx_ref[e] != e)
        def _():
            for cp in weight_copies(nx_ref[e]):
                cp.start()

    @pl.when(i < nu_ref[0])
    def _():
        gu = _dot(_load_row_tiles(xs_ref).astype(BF16), wgu_bf_ref[...]) + bgu_ref[...]
        gate = jnp.minimum(gu[:, 0:D_FF], SWIGLU_LIMIT)
        up = jnp.clip(gu[:, D_FF:2 * D_FF], -SWIGLU_LIMIT, SWIGLU_LIMIT)
        act = gate * _sigmoid(SWIGLU_ALPHA * gate) * (up + 1.0)
        _store_row_tiles(ys_ref, _dot(act.astype(BF16), wd_bf_ref[...]) + bd_ref[...])

    @pl.when(i >= nu_ref[0])
    def _():
        ys_ref[...] = jnp.zeros_like(ys_ref)


def _experts(tile_expert, next_expert, n_used, xs, wgu, bgu, wd, bd):
    tg = MOE_TG * ROW_SUB
    n_tiles = xs.shape[0] // tg
    tile = lambda i, te, nx, nu: (jnp.minimum(i, nu[0] - 1), 0)
    expert = lambda i, te, nx, nu: (te[i], 0, 0)
    return pl.pallas_call(
        _experts_kernel,
        grid_spec=pltpu.PrefetchScalarGridSpec(
            num_scalar_prefetch=3,
            grid=(n_tiles,),
            in_specs=[pl.BlockSpec((tg, LANES), tile),
                      pl.BlockSpec(memory_space=pl.ANY),
                      pl.BlockSpec((None, 1, 2 * D_FF), expert),
                      pl.BlockSpec(memory_space=pl.ANY),
                      pl.BlockSpec((None, 1, D_MODEL), expert)],
            out_specs=pl.BlockSpec((tg, LANES), lambda i, te, nx, nu: (i, 0)),
            scratch_shapes=[pltpu.VMEM((D_MODEL, 2 * D_FF), F32), pltpu.VMEM((D_FF, D_MODEL), F32),
                            pltpu.VMEM((D_MODEL, 2 * D_FF), BF16), pltpu.VMEM((D_FF, D_MODEL), BF16),
                            pltpu.SemaphoreType.DMA((2,))]),
        out_shape=jax.ShapeDtypeStruct(xs.shape, F32),
        compiler_params=pltpu.CompilerParams(dimension_semantics=("arbitrary",), vmem_limit_bytes=VMEM_LIMIT),
        name="experts",
    )(tile_expert, next_expert, n_used, xs, wgu, bgu, wd, bd)


def _combine_kernel(dest_ref, dest_next_ref, route_ref, x1_ref, gfin_ref, ys_ref, out_ref, ybuf_ref, sem):
    i = pl.program_id(0)
    tm = x1_ref.shape[0]

    def gather_rows(dref, buf):
        def start(r, c):
            for kk in range(TOP_K):
                pltpu.make_async_copy(_row_tile(ys_ref, dref[0, 0, r * TOP_K + kk]),
                                      _row_tile(ybuf_ref.at[buf, kk], r), sem.at[buf]).start(priority=kk % 2)
            return c
        lax.fori_loop(0, tm, start, 0, unroll=DMA_UNROLL)

    @pl.when(i == 0)
    def _():
        gather_rows(dest_ref, 0)

    @pl.when(i + 1 < pl.num_programs(0))
    def _():
        gather_rows(dest_next_ref, (i + 1) % 2)

    buf = i % 2
    for kk in range(TOP_K):
        pltpu.make_async_copy(ys_ref.at[pl.ds(0, tm * ROW_SUB)], ybuf_ref.at[buf, kk], sem.at[buf]).wait()
    route = route_ref[...]
    acc = x1_ref[...]
    for kk in range(TOP_K):
        acc = acc + route[:, kk:kk + 1] * _load_row_tiles(ybuf_ref.at[buf, kk])
    out_ref[...] = _rms(acc, gfin_ref[...])


def _combine(dest3, route, x1, gfin, ys):
    t = route.shape[0]
    tm = CMB_TM
    last = t // tm - 1
    return pl.pallas_call(
        _combine_kernel,
        grid=(t // tm,),
        in_specs=[pl.BlockSpec((1, 1, tm * TOP_K), lambda i: (i, 0, 0), memory_space=pltpu.SMEM),
                  pl.BlockSpec((1, 1, tm * TOP_K), lambda i: (jnp.minimum(i + 1, last), 0, 0),
                               memory_space=pltpu.SMEM),
                  pl.BlockSpec((tm, LANES), lambda i: (i, 0)),
                  pl.BlockSpec((tm, D_MODEL), lambda i: (i, 0)),
                  pl.BlockSpec((1, D_MODEL), lambda i: (0, 0)),
                  pl.BlockSpec(memory_space=pl.ANY)],
        out_specs=pl.BlockSpec((tm, D_MODEL), lambda i: (i, 0)),
        out_shape=jax.ShapeDtypeStruct((t, D_MODEL), F32),
        scratch_shapes=[pltpu.VMEM((2, TOP_K, tm * ROW_SUB, LANES), F32), pltpu.SemaphoreType.DMA((2,))],
        compiler_params=pltpu.CompilerParams(dimension_semantics=("arbitrary",), vmem_limit_bytes=VMEM_LIMIT),
        name="combine",
    )(dest3, dest3, route, x1, gfin, ys)


def kernel(x, g_mix, w_in, conv_w, a_log, dt_bias, dn_norm, w_up_a, pool_w, pool_scale, w_up_b, w_out, g_ffn,
           w_router, b_router, w_gate_up, b_gate_up, w_down, b_down, g_final):
    batch, seq, d = x.shape
    assert d == D_MODEL and seq % DN_LB == 0 and seq % MIX_TM == 0
    assert (batch * seq) % DSP_TM == 0 and (batch * seq * TOP_K) % MOE_TG == 0
    assert g_mix.shape[0] == 1, "one layer"
    t = batch * seq
    x2 = x.reshape(t, d)

    w = w_in[0]
    o_z = 3 * DN_WIDTH
    o_b = o_z + DN_WIDTH
    o_a = o_b + DN_HEADS
    o_u = o_a + DN_HEADS
    o_g = o_u + POOL_WIDTH
    wqkv = w[:, 0:o_z].astype(BF16)
    wz = w[:, o_z:o_b].astype(BF16)
    wu = w[:, o_u:o_g].astype(BF16)
    wg = w[:, o_g:].astype(BF16)
    w_b = w[:, o_b:o_a]
    w_a = w[:, o_a:o_u]
    pad = jnp.zeros((d, LANES - DN_HEADS), F32)
    wba = jnp.concatenate([w_b, pad, w_a, pad], axis=1).astype(BF16)
    wbat = jnp.concatenate([w_b, w_a], axis=1).T.astype(BF16)

    qkv, z, u, gates, ba, bat = _in_proj(x2, g_mix, wqkv, wz, wu, wg, wba, wbat, conv_w[0], seq)

    lane_pad = lambda p: jnp.pad(p.reshape(1, DN_HEADS), ((0, 0), (0, LANES - DN_HEADS)))
    on = _deltanet(qkv, z, ba, bat, lane_pad(a_log[0]), lane_pad(dt_bias[0]),
                   a_log[0].reshape(DN_HEADS, 1), dt_bias[0].reshape(DN_HEADS, 1),
                   jnp.tile(dn_norm[0], DN_HEADS).reshape(1, DN_WIDTH), batch, seq)

    wr = jnp.pad(w_router[0], ((0, 0), (0, LANES - N_EXPERTS)))
    wr_hi = wr.astype(BF16)
    wr_lo = (wr - wr_hi.astype(F32)).astype(BF16)
    br = jnp.pad(b_router[0].reshape(1, N_EXPERTS), ((0, 0), (0, LANES - N_EXPERTS)))
    x1, h2, route, rank, expert, counts = _mix(x2, on, u, gates, w_up_a[0].astype(BF16), pool_w[0].astype(BF16),
                                                pool_scale[0].reshape(1, POOL_WIDTH), w_up_b[0].astype(BF16),
                                                w_out[0].astype(BF16), g_ffn, wr_hi, wr_lo, br, batch, seq)

    n_tiles = t * TOP_K // MOE_TG + N_EXPERTS
    dest, tile_expert, next_expert, n_used, pad_start, pad_len = _routing_tables(rank, expert, counts, MOE_TG,
                                                                                 n_tiles)
    xs = _dispatch(pad_start, pad_len, n_used, dest.reshape(t // DSP_TM, 1, DSP_TM * TOP_K), h2, n_tiles * MOE_TG)
    ys = _experts(tile_expert, next_expert, n_used, xs, w_gate_up[0],
                  b_gate_up[0].reshape(N_EXPERTS, 1, 2 * D_FF), w_down[0], b_down[0].reshape(N_EXPERTS, 1, D_MODEL))
    out = _combine(dest.reshape(t // CMB_TM, 1, CMB_TM * TOP_K), route, x1, g_final.reshape(1, D_MODEL), ys)
    return out.reshape(batch, seq, d)
```

```python
import functools

import jax
import jax.numpy as jnp
from jax import lax
from jax.experimental import pallas as pl
from jax.experimental.pallas import tpu as pltpu

F32 = jnp.float32
BF16 = jnp.bfloat16

D_MODEL = 1024
CHUNK = 64
DN_HEADS = 8
DN_HEAD_DIM = 64
DN_WIDTH = DN_HEADS * DN_HEAD_DIM
CONV_WIDTH = 4
POOL_GROUPS = 4
POOL_WINDOWS = (2, 4, 8, 16)
POOL_WIDTH = 512
POOL_GROUP_DIM = 128
POOL_HALO = 16
N_EXPERTS = 32
TOP_K = 4
D_FF = D_MODEL
SWIGLU_LIMIT = 7.0
SWIGLU_ALPHA = 1.702
NORM_EPS = 1e-6
LANES = 128
SUBLANES = 8
VMEM_LIMIT = 56 * 1024 * 1024

IN_TM = 512
DN_LB = 512
INTRA_GROUP = 3
MIX_TM = 512
MOE_TG = 512
DSP_TM = 2048
CMB_TM = 512
DMA_UNROLL = 8


def _dot(a, b):
    return jnp.dot(a, b, preferred_element_type=F32)


def _dot_nt(a, b):
    return lax.dot_general(a, b, (((1,), (1,)), ((), ())), preferred_element_type=F32)


def _split2(x):
    hi = x.astype(BF16)
    lo = (x - hi.astype(F32)).astype(BF16)
    return hi, lo


def _dot_exact_rhs(x, m):
    hi, lo = _split2(x)
    return _dot(hi, m) + _dot(lo, m)


def _dot_exact_lhs(m, x):
    hi, lo = _split2(x)
    return _dot(m, hi) + _dot(m, lo)


def _softplus(x):
    return jnp.maximum(x, 0.0) + jnp.log1p(jnp.exp(-jnp.abs(x)))


def _sigmoid(x):
    return 1.0 / (1.0 + jnp.exp(-x))


def _rms(x, g):
    return x * lax.rsqrt(jnp.mean(x * x, axis=-1, keepdims=True) + NORM_EPS) * g


ROW_SUB = D_MODEL // LANES


def _row_tile(ref, r):
    return ref.at[pl.ds(pl.multiple_of(r * ROW_SUB, ROW_SUB), ROW_SUB)]


def _store_row_tiles(ref, x):
    for j in range(ROW_SUB):
        ref[pl.ds(j, x.shape[0], stride=ROW_SUB), :] = x[:, j * LANES:(j + 1) * LANES]


def _load_row_tiles(ref):
    n = ref.shape[0] // ROW_SUB
    return jnp.concatenate([ref[pl.ds(j, n, stride=ROW_SUB), :] for j in range(ROW_SUB)], axis=1)


def _in_proj_kernel(x_ref, g_ref, wqkv_ref, wz_ref, wu_ref, wg_ref, wba_ref, wbat_ref, convw_ref,
                    qkv_ref, z_ref, u_ref, gates_ref, ba_ref, bat_ref, hist_ref, *, steps_per_seq):
    i = pl.program_id(0)
    tm = x_ref.shape[0]

    @pl.when(i == 0)
    def _():
        hist_ref[...] = jnp.zeros_like(hist_ref)

    @pl.when((i + steps_per_seq - 1) % steps_per_seq == 0)
    def _():
        hist_ref[0:SUBLANES, :] = jnp.zeros((SUBLANES, hist_ref.shape[1]), F32)

    cw = convw_ref[...]
    prev = hist_ref[SUBLANES:SUBLANES + tm, :]
    act = prev * cw[CONV_WIDTH - 1:CONV_WIDTH]
    for s in range(1, CONV_WIDTH):
        act = act + hist_ref[pl.ds(SUBLANES - s, tm), :] * cw[CONV_WIDTH - 1 - s:CONV_WIDTH - s]
    qkv_ref[...] = act * _sigmoid(act)

    hb = _rms(x_ref[...], g_ref[...]).astype(BF16)
    hist_ref[0:SUBLANES, :] = prev[tm - SUBLANES:tm]
    hist_ref[SUBLANES:SUBLANES + tm, :] = _dot(hb, wqkv_ref[...])
    z_ref[...] = _dot(hb, wz_ref[...]).astype(z_ref.dtype)
    u_ref[...] = _dot(hb, wu_ref[...])
    gates_ref[...] = _dot(hb, wg_ref[...]).astype(gates_ref.dtype)
    ba_ref[...] = _dot(hb, wba_ref[...])
    bat_ref[...] = _dot_nt(wbat_ref[...], hb)


def _in_proj(x2, g_mix, wqkv, wz, wu, wg, wba, wbat, conv_w, seq):
    t = x2.shape[0]
    tm = IN_TM
    n = t // tm
    full = lambda a: pl.BlockSpec(a.shape, lambda i: (0, 0))
    row = lambda c: pl.BlockSpec((tm, c), lambda i: (jnp.minimum(i, n - 1), 0))
    return pl.pallas_call(
        functools.partial(_in_proj_kernel, steps_per_seq=seq // tm),
        grid=(n + 1,),
        in_specs=[row(D_MODEL), full(g_mix), full(wqkv), full(wz), full(wu), full(wg), full(wba), full(wbat),
                  full(conv_w)],
        out_specs=[pl.BlockSpec((tm, 3 * DN_WIDTH), lambda i: (jnp.maximum(i - 1, 0), 0)),
                   row(DN_WIDTH), row(POOL_WIDTH), row(2 * D_MODEL), row(2 * LANES),
                   pl.BlockSpec((2 * SUBLANES, tm), lambda i: (0, jnp.minimum(i, n - 1)))],
        out_shape=[jax.ShapeDtypeStruct((t, 3 * DN_WIDTH), F32),
                   jax.ShapeDtypeStruct((t, DN_WIDTH), BF16),
                   jax.ShapeDtypeStruct((t, POOL_WIDTH), F32),
                   jax.ShapeDtypeStruct((t, 2 * D_MODEL), BF16),
                   jax.ShapeDtypeStruct((t, 2 * LANES), F32),
                   jax.ShapeDtypeStruct((2 * SUBLANES, t), F32)],
        scratch_shapes=[pltpu.VMEM((SUBLANES + tm, 3 * DN_WIDTH), F32)],
        compiler_params=pltpu.CompilerParams(dimension_semantics=("arbitrary",), vmem_limit_bytes=VMEM_LIMIT),
        name="in_proj",
    )(x2, g_mix, wqkv, wz, wu, wg, wba, wbat, conv_w)


def _deltanet_kernel(qkv_ref, z_ref, ba_ref, bat_ref, alog_r_ref, dtb_r_ref, alog_c_ref, dtb_c_ref, dnw_ref,
                     head_ones_ref, tril_ref, triu_ref, chunk_ones_ref, expand_ref, o_ref,
                     s_ref, qn_ref, kn_ref, kbe_ref, vb_ref, qg_ref, kd_ref,
                     xbeta_ref, xgc_ref, xgl_ref, gr_ref, oacc_ref, qkd_ref, wq_ref, u_ref, kdt_ref, gl_ref):
    lb = qkv_ref.shape[0]
    n_chunks = lb // CHUNK

    @pl.when(pl.program_id(1) == 0)
    def _():
        s_ref[...] = jnp.zeros_like(s_ref)

    q = qkv_ref[:, 0:DN_WIDTH]
    k = qkv_ref[:, DN_WIDTH:2 * DN_WIDTH]
    v = qkv_ref[:, 2 * DN_WIDTH:3 * DN_WIDTH]

    head_ones = head_ones_ref[...]
    qn = q * lax.rsqrt(_dot_exact_rhs(q * q, head_ones) + NORM_EPS) * (DN_HEAD_DIM ** -0.5)
    kn = k * lax.rsqrt(_dot_exact_rhs(k * k, head_ones) + NORM_EPS)

    ba = ba_ref[...]
    beta_c = _sigmoid(ba[:, 0:LANES])
    g_c = -jnp.exp(alog_r_ref[...]) * _softplus(ba[:, LANES:2 * LANES] + dtb_r_ref[...])
    lane = lax.broadcasted_iota(jnp.int32, (lb, LANES), 1)
    g_c = jnp.where(lane < DN_HEADS, g_c, 0.0)
    bat = bat_ref[...]
    g_r = -jnp.exp(alog_c_ref[...]) * _softplus(bat[SUBLANES:2 * SUBLANES] + dtb_c_ref[...])

    gc = _dot_exact_lhs(tril_ref[...], g_c)
    gtot = _dot_exact_lhs(chunk_ones_ref[...], g_c)
    gr = _dot_exact_rhs(g_r, triu_ref[...])
    for c in range(n_chunks):
        gr_ref[c] = gr[:, c * CHUNK:(c + 1) * CHUNK]

    expand = expand_ref[...]
    x_beta = _dot_exact_rhs(beta_c, expand)
    x_gc = _dot_exact_rhs(gc, expand)
    x_gtot = _dot_exact_rhs(gtot, expand)
    x_eg = jnp.exp(x_gc)
    xbeta_ref[...] = x_beta
    xgc_ref[...] = x_gc
    xgl_ref[...] = jnp.exp(x_gtot)
    qn_ref[...] = qn.astype(BF16)
    kn_ref[...] = kn.astype(BF16)
    kbe_ref[...] = (kn * (x_beta * x_eg)).astype(BF16)
    vb_ref[...] = (v * x_beta).astype(BF16)
    qg_ref[...] = (qn * x_eg).astype(BF16)
    kd_ref[...] = (kn * jnp.exp(x_gtot - x_gc)).astype(BF16)

    ci = lax.broadcasted_iota(jnp.int32, (CHUNK, CHUNK), 0)
    cj = lax.broadcasted_iota(jnp.int32, (CHUNK, CHUNK), 1)
    eye = (ci == cj).astype(F32)

    heads = range(DN_HEADS)
    lanes = [pl.ds(h * DN_HEAD_DIM, DN_HEAD_DIM) for h in heads]
    stack = lambda top, bot: jnp.concatenate([top, bot], axis=0)

    chunk_rows = lambda c: pl.ds(c * CHUNK, CHUNK)

    def within_chunks(chunk_ids):
        chains = [(c, h) for c in chunk_ids for h in heads]
        ids = range(len(chains))
        rows = [chunk_rows(c) for c, _ in chains]
        ln = [lanes[h] for _, h in chains]
        kb = [kn_ref[rows[i], ln[i]] for i in ids]
        kq = [_dot_nt(stack(kb[i], qn_ref[rows[i], ln[i]]), kb[i]) for i in ids]
        yield
        decay = [jnp.exp(jnp.where(ci >= cj, xgc_ref[rows[i], ln[i]] - gr_ref[c, h:h + 1, :], -jnp.inf))
                 for i, (c, h) in enumerate(chains)]
        a = [jnp.where(ci > cj, xbeta_ref[rows[i], ln[i]] * kq[i][0:CHUNK] * decay[i], 0.0) for i in ids]
        slot = [c * DN_HEADS + h for c, h in chains]
        for i in ids:
            qkd_ref[slot[i]] = (kq[i][CHUNK:2 * CHUNK] * decay[i]).astype(BF16)
            wq_ref[slot[i], CHUNK:2 * CHUNK, :] = qg_ref[rows[i], ln[i]]
            kdt_ref[slot[i]] = kd_ref[rows[i], ln[i]].T.astype(BF16)
            gl_ref[slot[i]] = xgl_ref[rows[i], ln[i]]
        t_inv = [eye - a[i] for i in ids]
        pw = [a[i].astype(BF16) for i in ids]
        pw = [_dot(pw[i], pw[i]).astype(BF16) for i in ids]
        yield
        for _ in range(4):
            r = [_dot(stack(t_inv[i].astype(BF16), pw[i]), pw[i]) for i in ids]
            yield
            t_inv = [t_inv[i] + r[i][0:CHUNK] for i in ids]
            pw = [r[i][CHUNK:2 * CHUNK].astype(BF16) for i in ids]
        t_inv = [t_inv[i] + _dot(t_inv[i].astype(BF16), pw[i]) for i in ids]
        yield
        tb = [t_inv[i].astype(BF16) for i in ids]
        for i in ids:
            wq_ref[slot[i], 0:CHUNK, :] = _dot(tb[i], kbe_ref[rows[i], ln[i]]).astype(BF16)
        yield
        for i in ids:
            u_ref[slot[i]] = _dot(tb[i], vb_ref[rows[i], ln[i]])
        yield

    def state_chunks(chunk_ids):
        for c in chunk_ids:
            rows = chunk_rows(c)
            slot = [c * DN_HEADS + h for h in heads]
            s = [s_ref[h] for h in heads]
            ws = [_dot(wq_ref[slot[h]], s[h].astype(BF16)) for h in heads]
            yield
            vnb = [(u_ref[slot[h]] - ws[h][0:CHUNK]).astype(BF16) for h in heads]
            for h in heads:
                oacc_ref[h, rows, :] = ws[h][CHUNK:2 * CHUNK] + _dot(qkd_ref[slot[h]], vnb[h])
            yield
            for h in heads:
                s_ref[h] = s[h] * gl_ref[slot[h]] + _dot(kdt_ref[slot[h]], vnb[h])
            yield

    groups = [list(range(g, min(g + INTRA_GROUP, n_chunks))) for g in range(0, n_chunks, INTRA_GROUP)]
    pending = iter(())
    for grp in groups:
        for _ in within_chunks(grp):
            next(pending, None)
        for _ in pending:
            pass
        pending = state_chunks(grp)
    for _ in pending:
        pass

    o = jnp.concatenate([oacc_ref[h] for h in heads], axis=1)
    ms = _dot_exact_rhs(o * o, head_ones) * (1.0 / DN_HEAD_DIM)
    z = z_ref[...].astype(F32)
    o_ref[...] = (o * lax.rsqrt(ms + NORM_EPS) * dnw_ref[...] * (z * _sigmoid(z))).astype(o_ref.dtype)


def _deltanet_masks(lb):
    head = jnp.arange(DN_WIDTH) // DN_HEAD_DIM
    head_ones = head[:, None] == head[None, :]
    pos = jnp.arange(lb)
    same = (pos[:, None] // CHUNK) == (pos[None, :] // CHUNK)
    tril = same & (pos[None, :] <= pos[:, None])
    triu = same & (pos[:, None] <= pos[None, :])
    expand = jnp.arange(LANES)[:, None] == head[None, :]
    return tuple(m.astype(BF16) for m in (head_ones, tril, triu, same, expand))


def _deltanet(qkv, z, ba, bat, alog_r, dtb_r, alog_c, dtb_c, dnw, batch, seq):
    lb = DN_LB
    nb = seq // lb
    slots = lb // CHUNK * DN_HEADS
    masks = _deltanet_masks(lb)
    full = lambda a: pl.BlockSpec(a.shape, lambda b, j: (0,) * a.ndim)
    row = lambda n: pl.BlockSpec((lb, n), lambda b, j: (b * nb + j, 0))
    return pl.pallas_call(
        _deltanet_kernel,
        grid=(batch, nb),
        in_specs=[row(3 * DN_WIDTH), row(DN_WIDTH), row(2 * LANES),
                  pl.BlockSpec((2 * SUBLANES, lb), lambda b, j: (0, b * nb + j)),
                  full(alog_r), full(dtb_r), full(alog_c), full(dtb_c), full(dnw)] + [full(m) for m in masks],
        out_specs=row(DN_WIDTH),
        out_shape=jax.ShapeDtypeStruct((batch * seq, DN_WIDTH), BF16),
        scratch_shapes=[
            pltpu.VMEM((DN_HEADS, DN_HEAD_DIM, DN_HEAD_DIM), F32),
            pltpu.VMEM((lb, DN_WIDTH), BF16),
            pltpu.VMEM((lb, DN_WIDTH), BF16),
            pltpu.VMEM((lb, DN_WIDTH), BF16),
            pltpu.VMEM((lb, DN_WIDTH), BF16),
            pltpu.VMEM((lb, DN_WIDTH), BF16),
            pltpu.VMEM((lb, DN_WIDTH), BF16),
            pltpu.VMEM((lb, DN_WIDTH), F32),
            pltpu.VMEM((lb, DN_WIDTH), F32),
            pltpu.VMEM((lb, DN_WIDTH), F32),
            pltpu.VMEM((lb // CHUNK, DN_HEADS, CHUNK), F32),
            pltpu.VMEM((DN_HEADS, lb, DN_HEAD_DIM), F32),
            pltpu.VMEM((slots, CHUNK, CHUNK), BF16),
            pltpu.VMEM((slots, 2 * CHUNK, DN_HEAD_DIM), BF16),
            pltpu.VMEM((slots, CHUNK, DN_HEAD_DIM), F32),
            pltpu.VMEM((slots, DN_HEAD_DIM, CHUNK), BF16),
            pltpu.VMEM((slots, DN_HEAD_DIM, DN_HEAD_DIM), F32),
        ],
        compiler_params=pltpu.CompilerParams(dimension_semantics=("arbitrary", "arbitrary"),
                                             vmem_limit_bytes=VMEM_LIMIT),
        name="deltanet",
    )(qkv, z, ba, bat, alog_r, dtb_r, alog_c, dtb_c, dnw, *masks)


def _mix_kernel(x_ref, on_ref, u_ref, gates_ref, wupa_ref, poolw_ref, pscale_ref, wupb_ref, wout_ref,
                gffn_ref, wr_hi_ref, wr_lo_ref, br_ref,
                x1_ref, h2_ref, route_ref, rank_ref, expert_ref, counts_ref,
                ucarry_ref, ecount_ref, logits_ref, *, steps_per_seq):
    i = pl.program_id(0)
    tm = x_ref.shape[0]

    @pl.when(i == 0)
    def _():
        ecount_ref[...] = jnp.zeros_like(ecount_ref)
        logits_ref[...] = jnp.zeros_like(logits_ref)

    @pl.when(i % steps_per_seq == 0)
    def _():
        ucarry_ref[...] = jnp.zeros_like(ucarry_ref)

    lane = lax.broadcasted_iota(jnp.int32, (tm, LANES), 1)
    lg = jnp.where(lane < N_EXPERTS, logits_ref[...], -jnp.inf)
    vals, idxs, sels = [], [], []
    for _ in range(TOP_K):
        m = jnp.max(lg, axis=-1, keepdims=True)
        idx = jnp.min(jnp.where(lg == m, lane, LANES), axis=-1, keepdims=True)
        sel = lane == idx
        vals.append(m)
        idxs.append(idx)
        sels.append(sel)
        lg = jnp.where(sel, -jnp.inf, lg)
    es = [jnp.exp(vk - vals[0]) for vk in vals]
    denom = es[0] + es[1] + es[2] + es[3]
    chosen = jnp.zeros((tm, LANES), F32)
    for sel in sels:
        chosen = chosen + jnp.where(sel, 1.0, 0.0)
    chosen = jnp.where(i > 0, chosen, 0.0)

    u = u_ref[...]
    ext = jnp.concatenate([ucarry_ref[...], u], axis=0)
    ucarry_ref[...] = u[tm - POOL_HALO:tm]
    t_pos = (i % steps_per_seq) * tm + lax.broadcasted_iota(jnp.int32, (tm, POOL_GROUP_DIM), 0)
    ys = []
    for g in range(POOL_GROUPS):
        s = ext[:, g * POOL_GROUP_DIM:(g + 1) * POOL_GROUP_DIM]
        shift = 1
        while shift < POOL_WINDOWS[g]:
            s = s + pltpu.roll(s, shift, 0)
            shift *= 2
        count = jnp.minimum(t_pos + 1, POOL_WINDOWS[g]).astype(F32)
        pooled = s[POOL_HALO:] / count - u[:, g * POOL_GROUP_DIM:(g + 1) * POOL_GROUP_DIM]
        ys.append(_dot(pooled.astype(BF16), poolw_ref[g]))
    yb = jnp.concatenate(ys, axis=-1) * pscale_ref[...]
    y_b = _dot(yb.astype(BF16), wupb_ref[...])
    y_a = _dot(on_ref[...], wupa_ref[...])
    gates = gates_ref[...].astype(F32)
    merged = _sigmoid(gates[:, 0:D_MODEL]) * y_a + _sigmoid(gates[:, D_MODEL:2 * D_MODEL]) * y_b
    x1 = x_ref[...] + _dot(merged.astype(BF16), wout_ref[...])
    x1_ref[...] = x1

    h2 = _rms(x1, gffn_ref[...])
    _store_row_tiles(h2_ref, h2)

    hi, lo = _split2(h2)
    logits_ref[...] = (_dot(hi, wr_hi_ref[...]) + _dot(lo, wr_hi_ref[...]) + _dot(hi, wr_lo_ref[...])
                       + br_ref[...])

    ri = lax.broadcasted_iota(jnp.int32, (tm, tm), 0)
    rj = lax.broadcasted_iota(jnp.int32, (tm, tm), 1)
    before = (rj < ri).astype(BF16)
    pos = ecount_ref[...] + _dot(before, chosen.astype(BF16))
    ecount_ref[...] = ecount_ref[...] + jnp.sum(chosen, axis=0, keepdims=True)
    counts_ref[...] = ecount_ref[...]

    per_row = LANES // TOP_K
    first_lane = (lax.broadcasted_iota(jnp.int32, (tm, LANES), 0) % per_row) * TOP_K
    weights = jnp.zeros((tm, LANES), F32)
    flat_rank = jnp.zeros((tm, LANES), F32)
    flat_expert = jnp.zeros((tm, LANES), F32)
    for kk in range(TOP_K):
        rank = jnp.sum(jnp.where(sels[kk], pos, 0.0), axis=-1, keepdims=True)
        weights = jnp.where(lane == kk, es[kk] / denom, weights)
        flat_rank = jnp.where(lane == first_lane + kk, rank, flat_rank)
        flat_expert = jnp.where(lane == first_lane + kk, idxs[kk].astype(F32), flat_expert)
    route_ref[...] = weights
    fold = lambda a: jnp.sum(a.reshape(tm // per_row, per_row, LANES), axis=1).astype(jnp.int32)
    rank_ref[...] = fold(flat_rank)
    expert_ref[...] = fold(flat_expert)


def _mix(x2, on, u, gates, wupa, poolw, pscale, wupb, wout, gffn, wr_hi, wr_lo, br, batch, seq):
    tm = MIX_TM
    t = batch * seq
    n = t // tm
    full = lambda a: pl.BlockSpec(a.shape, lambda i: (0,) * a.ndim)
    row = lambda c: pl.BlockSpec((tm, c), lambda i: (jnp.minimum(i, n - 1), 0))
    return pl.pallas_call(
        functools.partial(_mix_kernel, steps_per_seq=seq // tm),
        grid=(n + 1,),
        in_specs=[row(D_MODEL), row(DN_WIDTH), row(POOL_WIDTH), row(2 * D_MODEL),
                  full(wupa), full(poolw), full(pscale), full(wupb), full(wout), full(gffn),
                  full(wr_hi), full(wr_lo), full(br)],
        out_specs=[pl.BlockSpec((tm, D_MODEL), lambda i: (i, 0)),
                   pl.BlockSpec((tm * ROW_SUB, LANES), lambda i: (i, 0)),
                   pl.BlockSpec((tm, LANES), lambda i: (jnp.maximum(i - 1, 0), 0)),
                   pl.BlockSpec((tm * TOP_K // LANES, LANES), lambda i: (jnp.maximum(i - 1, 0), 0)),
                   pl.BlockSpec((tm * TOP_K // LANES, LANES), lambda i: (jnp.maximum(i - 1, 0), 0)),
                   pl.BlockSpec((1, LANES), lambda i: (0, 0))],
        out_shape=[jax.ShapeDtypeStruct((t + tm, D_MODEL), F32),
                   jax.ShapeDtypeStruct(((t + tm) * ROW_SUB, LANES), F32),
                   jax.ShapeDtypeStruct((t, LANES), F32),
                   jax.ShapeDtypeStruct((t * TOP_K // LANES, LANES), jnp.int32),
                   jax.ShapeDtypeStruct((t * TOP_K // LANES, LANES), jnp.int32),
                   jax.ShapeDtypeStruct((1, LANES), F32)],
        scratch_shapes=[pltpu.VMEM((POOL_HALO, POOL_WIDTH), F32), pltpu.VMEM((1, LANES), F32),
                        pltpu.VMEM((tm, LANES), F32)],
        compiler_params=pltpu.CompilerParams(dimension_semantics=("arbitrary",), vmem_limit_bytes=VMEM_LIMIT),
        name="mix",
    )(x2, on, u, gates, wupa, poolw, pscale, wupb, wout, gffn, wr_hi, wr_lo, br)


def _routing_tables(rank, expert, counts, tg, n_tiles):
    cnt = counts[0, 0:N_EXPERTS].astype(jnp.int32)
    padded = (cnt + (tg - 1)) // tg * tg
    ends = jnp.cumsum(padded)
    offs = ends - padded
    experts = jnp.arange(N_EXPERTS, dtype=jnp.int32)
    dest = jnp.sum(jnp.where(expert[..., None] == experts, offs, 0), axis=-1) + rank
    n_used = ends[N_EXPERTS - 1] // tg
    tile_start = jnp.arange(n_tiles, dtype=jnp.int32) * tg
    tile_expert = jnp.sum((tile_start[:, None] >= ends[None, :]).astype(jnp.int32), axis=1)
    tile_expert = jnp.minimum(tile_expert, N_EXPERTS - 1)
    last = tile_expert[jnp.maximum(n_used - 1, 0)]
    tile_expert = jnp.where(jnp.arange(n_tiles) < n_used, tile_expert, last)
    pad_start = offs + cnt
    pad_len = padded - cnt
    later = (experts[None, :] > experts[:, None]) & (cnt[None, :] > 0)
    next_expert = jnp.min(jnp.where(later, experts[None, :], N_EXPERTS), axis=1)
    next_expert = jnp.where(next_expert == N_EXPERTS, experts, next_expert)
    return (dest, tile_expert.astype(jnp.int32), next_expert.astype(jnp.int32),
            n_used.reshape(1).astype(jnp.int32), pad_start.astype(jnp.int32), pad_len.astype(jnp.int32))


def _dispatch_kernel(pad_start_ref, pad_len_ref, nu_ref, dest_ref, h2_ref, xs_ref, zero_ref, sem, zsem):
    tm = h2_ref.shape[0] // ROW_SUB
    tg = zero_ref.shape[0]
    n_tiles = xs_ref.shape[0] // tg

    @pl.when(pl.program_id(0) == 0)
    def _():
        zero_ref[...] = jnp.zeros_like(zero_ref)

        def pad_runs(e, act):
            first, n = pad_start_ref[e], pad_len_ref[e]
            for bit in reversed(range((tg // ROW_SUB).bit_length() - 1)):
                rows = (1 << bit) * ROW_SUB

                @pl.when((n >> bit) & 1 == 1)
                def _():
                    start = pl.multiple_of((first + ((n >> (bit + 1)) << (bit + 1))) * ROW_SUB, ROW_SUB)
                    act(pltpu.make_async_copy(zero_ref.at[pl.ds(0, rows)], xs_ref.at[pl.ds(start, rows)], zsem))

        def tile_copy(i):
            return pltpu.make_async_copy(zero_ref, xs_ref.at[pl.ds(pl.multiple_of(i * tg, tg), tg)], zsem)

        lax.fori_loop(0, N_EXPERTS, lambda e, cc: (pad_runs(e, lambda cp: cp.start()), cc)[1], 0)
        lax.fori_loop(0, N_EXPERTS, lambda e, cc: (pad_runs(e, lambda cp: cp.wait()), cc)[1], 0)
        lax.fori_loop(nu_ref[0], n_tiles, lambda i, cc: (tile_copy(i).start(), cc)[1], 0)
        lax.fori_loop(nu_ref[0], n_tiles, lambda i, cc: (tile_copy(0).wait(), cc)[1], 0)

    def row_copy(r, d):
        return pltpu.make_async_copy(_row_tile(h2_ref, r), _row_tile(xs_ref, d), sem)

    def start(r, c):
        for kk in range(TOP_K):
            row_copy(r, dest_ref[0, 0, r * TOP_K + kk]).start(priority=kk % 2)
        return c

    lax.fori_loop(0, tm, start, 0, unroll=DMA_UNROLL)
    for kk in range(TOP_K):
        pltpu.make_async_copy(h2_ref, xs_ref.at[pl.ds(0, tm * ROW_SUB)], sem).wait()


def _dispatch(pad_start, pad_len, n_used, dest3, h2, n_rows):
    t = dest3.shape[0] * dest3.shape[2] // TOP_K
    tm = DSP_TM
    return pl.pallas_call(
        _dispatch_kernel,
        grid_spec=pltpu.PrefetchScalarGridSpec(
            num_scalar_prefetch=3,
            grid=(t // tm,),
            in_specs=[pl.BlockSpec((1, 1, tm * TOP_K), lambda i, ps, pn, nu: (i, 0, 0), memory_space=pltpu.SMEM),
                      pl.BlockSpec((tm * ROW_SUB, LANES), lambda i, ps, pn, nu: (i, 0))],
            out_specs=pl.BlockSpec(memory_space=pl.ANY),
            scratch_shapes=[pltpu.VMEM((MOE_TG * ROW_SUB, LANES), F32), pltpu.SemaphoreType.DMA,
                            pltpu.SemaphoreType.DMA]),
        out_shape=jax.ShapeDtypeStruct((n_rows * ROW_SUB, LANES), F32),
        compiler_params=pltpu.CompilerParams(dimension_semantics=("arbitrary",), vmem_limit_bytes=VMEM_LIMIT),
        name="dispatch",
    )(pad_start, pad_len, n_used, dest3, h2)


def _experts_kernel(te_ref, nx_ref, nu_ref, xs_ref, wgu_hbm_ref, bgu_ref, wd_hbm_ref, bd_ref, ys_ref,
                    wgu_f32_ref, wd_f32_ref, wgu_bf_ref, wd_bf_ref, sem):
    i = pl.program_id(0)
    e = te_ref[i]

    def weight_copies(ex):
        return (pltpu.make_async_copy(wgu_hbm_ref.at[ex], wgu_f32_ref, sem.at[0]),
                pltpu.make_async_copy(wd_hbm_ref.at[ex], wd_f32_ref, sem.at[1]))

    @pl.when(i == 0)
    def _():
        for cp in weight_copies(e):
            cp.start()

    @pl.when((i == 0) | (e != te_ref[jnp.maximum(i - 1, 0)]))
    def _():
        for cp in weight_copies(e):
            cp.wait()
        wgu_bf_ref[...] = wgu_f32_ref[...].astype(BF16)
        wd_bf_ref[...] = wd_f32_ref[...].astype(BF16)

        @pl.when(nx_ref[e] != e)
        def _():
            for cp in weight_copies(nx_ref[e]):
                cp.start()

    @pl.when(i < nu_ref[0])
    def _():
        gu = _dot(_load_row_tiles(xs_ref).astype(BF16), wgu_bf_ref[...]) + bgu_ref[...]
        gate = jnp.minimum(gu[:, 0:D_FF], SWIGLU_LIMIT)
        up = jnp.clip(gu[:, D_FF:2 * D_FF], -SWIGLU_LIMIT, SWIGLU_LIMIT)
        act = gate * _sigmoid(SWIGLU_ALPHA * gate) * (up + 1.0)
        _store_row_tiles(ys_ref, _dot(act.astype(BF16), wd_bf_ref[...]) + bd_ref[...])

    @pl.when(i >= nu_ref[0])
    def _():
        ys_ref[...] = jnp.zeros_like(ys_ref)


def _experts(tile_expert, next_expert, n_used, xs, wgu, bgu, wd, bd):
    tg = MOE_TG * ROW_SUB
    n_tiles = xs.shape[0] // tg
    tile = lambda i, te, nx, nu: (jnp.minimum(i, nu[0] - 1), 0)
    expert = lambda i, te, nx, nu: (te[i], 0, 0)
    return pl.pallas_call(
        _experts_kernel,
        grid_spec=pltpu.PrefetchScalarGridSpec(
            num_scalar_prefetch=3,
            grid=(n_tiles,),
            in_specs=[pl.BlockSpec((tg, LANES), tile),
                      pl.BlockSpec(memory_space=pl.ANY),
                      pl.BlockSpec((None, 1, 2 * D_FF), expert),
                      pl.BlockSpec(memory_space=pl.ANY),
                      pl.BlockSpec((None, 1, D_MODEL), expert)],
            out_specs=pl.BlockSpec((tg, LANES), lambda i, te, nx, nu: (i, 0)),
            scratch_shapes=[pltpu.VMEM((D_MODEL, 2 * D_FF), F32), pltpu.VMEM((D_FF, D_MODEL), F32),
                            pltpu.VMEM((D_MODEL, 2 * D_FF), BF16), pltpu.VMEM((D_FF, D_MODEL), BF16),
                            pltpu.SemaphoreType.DMA((2,))]),
        out_shape=jax.ShapeDtypeStruct(xs.shape, F32),
        compiler_params=pltpu.CompilerParams(dimension_semantics=("arbitrary",), vmem_limit_bytes=VMEM_LIMIT),
        name="experts",
    )(tile_expert, next_expert, n_used, xs, wgu, bgu, wd, bd)


def _combine_kernel(dest_ref, dest_next_ref, route_ref, x1_ref, gfin_ref, ys_ref, out_ref, ybuf_ref, sem):
    i = pl.program_id(0)
    tm = x1_ref.shape[0]

    def gather_rows(dref, buf):
        def start(r, c):
            for kk in range(TOP_K):
                pltpu.make_async_copy(_row_tile(ys_ref, dref[0, 0, r * TOP_K + kk]),
                                      _row_tile(ybuf_ref.at[buf, kk], r), sem.at[buf]).start(priority=kk % 2)
            return c
        lax.fori_loop(0, tm, start, 0, unroll=DMA_UNROLL)

    @pl.when(i == 0)
    def _():
        gather_rows(dest_ref, 0)

    @pl.when(i + 1 < pl.num_programs(0))
    def _():
        gather_rows(dest_next_ref, (i + 1) % 2)

    buf = i % 2
    for kk in range(TOP_K):
        pltpu.make_async_copy(ys_ref.at[pl.ds(0, tm * ROW_SUB)], ybuf_ref.at[buf, kk], sem.at[buf]).wait()
    route = route_ref[...]
    acc = x1_ref[...]
    for kk in range(TOP_K):
        acc = acc + route[:, kk:kk + 1] * _load_row_tiles(ybuf_ref.at[buf, kk])
    out_ref[...] = _rms(acc, gfin_ref[...])


def _combine(dest3, route, x1, gfin, ys):
    t = route.shape[0]
    tm = CMB_TM
    last = t // tm - 1
    return pl.pallas_call(
        _combine_kernel,
        grid=(t // tm,),
        in_specs=[pl.BlockSpec((1, 1, tm * TOP_K), lambda i: (i, 0, 0), memory_space=pltpu.SMEM),
                  pl.BlockSpec((1, 1, tm * TOP_K), lambda i: (jnp.minimum(i + 1, last), 0, 0),
                               memory_space=pltpu.SMEM),
                  pl.BlockSpec((tm, LANES), lambda i: (i, 0)),
                  pl.BlockSpec((tm, D_MODEL), lambda i: (i, 0)),
                  pl.BlockSpec((1, D_MODEL), lambda i: (0, 0)),
                  pl.BlockSpec(memory_space=pl.ANY)],
        out_specs=pl.BlockSpec((tm, D_MODEL), lambda i: (i, 0)),
        out_shape=jax.ShapeDtypeStruct((t, D_MODEL), F32),
        scratch_shapes=[pltpu.VMEM((2, TOP_K, tm * ROW_SUB, LANES), F32), pltpu.SemaphoreType.DMA((2,))],
        compiler_params=pltpu.CompilerParams(dimension_semantics=("arbitrary",), vmem_limit_bytes=VMEM_LIMIT),
        name="combine",
    )(dest3, dest3, route, x1, gfin, ys)


def kernel(x, g_mix, w_in, conv_w, a_log, dt_bias, dn_norm, w_up_a, pool_w, pool_scale, w_up_b, w_out, g_ffn,
           w_router, b_router, w_gate_up, b_gate_up, w_down, b_down, g_final):
    batch, seq, d = x.shape
    assert d == D_MODEL and seq % DN_LB == 0 and seq % MIX_TM == 0
    assert (batch * seq) % DSP_TM == 0 and (batch * seq * TOP_K) % MOE_TG == 0
    assert g_mix.shape[0] == 1, "one layer"
    t = batch * seq
    x2 = x.reshape(t, d)

    w = w_in[0]
    o_z = 3 * DN_WIDTH
    o_b = o_z + DN_WIDTH
    o_a = o_b + DN_HEADS
    o_u = o_a + DN_HEADS
    o_g = o_u + POOL_WIDTH
    wqkv = w[:, 0:o_z].astype(BF16)
    wz = w[:, o_z:o_b].astype(BF16)
    wu = w[:, o_u:o_g].astype(BF16)
    wg = w[:, o_g:].astype(BF16)
    w_b = w[:, o_b:o_a]
    w_a = w[:, o_a:o_u]
    pad = jnp.zeros((d, LANES - DN_HEADS), F32)
    wba = jnp.concatenate([w_b, pad, w_a, pad], axis=1).astype(BF16)
    wbat = jnp.concatenate([w_b, w_a], axis=1).T.astype(BF16)

    qkv, z, u, gates, ba, bat = _in_proj(x2, g_mix, wqkv, wz, wu, wg, wba, wbat, conv_w[0], seq)

    lane_pad = lambda p: jnp.pad(p.reshape(1, DN_HEADS), ((0, 0), (0, LANES - DN_HEADS)))
    on = _deltanet(qkv, z, ba, bat, lane_pad(a_log[0]), lane_pad(dt_bias[0]),
                   a_log[0].reshape(DN_HEADS, 1), dt_bias[0].reshape(DN_HEADS, 1),
                   jnp.tile(dn_norm[0], DN_HEADS).reshape(1, DN_WIDTH), batch, seq)

    wr = jnp.pad(w_router[0], ((0, 0), (0, LANES - N_EXPERTS)))
    wr_hi = wr.astype(BF16)
    wr_lo = (wr - wr_hi.astype(F32)).astype(BF16)
    br = jnp.pad(b_router[0].reshape(1, N_EXPERTS), ((0, 0), (0, LANES - N_EXPERTS)))
    x1, h2, route, rank, expert, counts = _mix(x2, on, u, gates, w_up_a[0].astype(BF16), pool_w[0].astype(BF16),
                                                pool_scale[0].reshape(1, POOL_WIDTH), w_up_b[0].astype(BF16),
                                                w_out[0].astype(BF16), g_ffn, wr_hi, wr_lo, br, batch, seq)

    n_tiles = t * TOP_K // MOE_TG + N_EXPERTS
    dest, tile_expert, next_expert, n_used, pad_start, pad_len = _routing_tables(rank, expert, counts, MOE_TG,
                                                                                 n_tiles)
    xs = _dispatch(pad_start, pad_len, n_used, dest.reshape(t // DSP_TM, 1, DSP_TM * TOP_K), h2, n_tiles * MOE_TG)
    ys = _experts(tile_expert, next_expert, n_used, xs, w_gate_up[0],
                  b_gate_up[0].reshape(N_EXPERTS, 1, 2 * D_FF), w_down[0], b_down[0].reshape(N_EXPERTS, 1, D_MODEL))
    out = _combine(dest.reshape(t // CMB_TM, 1, CMB_TM * TOP_K), route, x1, g_final.reshape(1, D_MODEL), ys)
    return out.reshape(batch, seq, d)
```

```python
import functools

import jax
import jax.numpy as jnp
from jax import lax
from jax.experimental import pallas as pl
from jax.experimental.pallas import tpu as pltpu

F32 = jnp.float32
BF16 = jnp.bfloat16

D_MODEL = 1024
CHUNK = 64
DN_HEADS = 8
DN_HEAD_DIM = 64
DN_WIDTH = DN_HEADS * DN_HEAD_DIM
CONV_WIDTH = 4
POOL_GROUPS = 4
POOL_WINDOWS = (2, 4, 8, 16)
POOL_WIDTH = 512
POOL_GROUP_DIM = 128
POOL_HALO = 16
N_EXPERTS = 32
TOP_K = 4
D_FF = D_MODEL
SWIGLU_LIMIT = 7.0
SWIGLU_ALPHA = 1.702
NORM_EPS = 1e-6
LANES = 128
SUBLANES = 8
VMEM_LIMIT = 56 * 1024 * 1024

IN_TM = 512
DN_LB = 512
INTRA_GROUP = 3
MIX_TM = 512
MOE_TG = 512
DSP_TM = 2048
CMB_TM = 512
DMA_UNROLL = 8


def _dot(a, b):
    return jnp.dot(a, b, preferred_element_type=F32)


def _dot_nt(a, b):
    return lax.dot_general(a, b, (((1,), (1,)), ((), ())), preferred_element_type=F32)


def _split2(x):
    hi = x.astype(BF16)
    lo = (x - hi.astype(F32)).astype(BF16)
    return hi, lo


def _dot_exact_rhs(x, m):
    hi, lo = _split2(x)
    return _dot(hi, m) + _dot(lo, m)


def _dot_exact_lhs(m, x):
    hi, lo = _split2(x)
    return _dot(m, hi) + _dot(m, lo)


def _softplus(x):
    return jnp.maximum(x, 0.0) + jnp.log1p(jnp.exp(-jnp.abs(x)))


def _sigmoid(x):
    return 1.0 / (1.0 + jnp.exp(-x))


def _rms(x, g):
    return x * lax.rsqrt(jnp.mean(x * x, axis=-1, keepdims=True) + NORM_EPS) * g


ROW_SUB = D_MODEL // LANES


def _row_tile(ref, r):
    return ref.at[pl.ds(pl.multiple_of(r * ROW_SUB, ROW_SUB), ROW_SUB)]


def _store_row_tiles(ref, x):
    for j in range(ROW_SUB):
        ref[pl.ds(j, x.shape[0], stride=ROW_SUB), :] = x[:, j * LANES:(j + 1) * LANES]


def _load_row_tiles(ref):
    n = ref.shape[0] // ROW_SUB
    return jnp.concatenate([ref[pl.ds(j, n, stride=ROW_SUB), :] for j in range(ROW_SUB)], axis=1)


def _in_proj_kernel(x_ref, g_ref, wqkv_ref, wz_ref, wu_ref, wg_ref, wba_ref, wbat_ref, convw_ref,
                    qkv_ref, z_ref, u_ref, gates_ref, ba_ref, bat_ref, hist_ref, *, steps_per_seq):
    i = pl.program_id(0)
    tm = x_ref.shape[0]

    @pl.when(i == 0)
    def _():
        hist_ref[...] = jnp.zeros_like(hist_ref)

    @pl.when((i + steps_per_seq - 1) % steps_per_seq == 0)
    def _():
        hist_ref[0:SUBLANES, :] = jnp.zeros((SUBLANES, hist_ref.shape[1]), F32)

    cw = convw_ref[...]
    prev = hist_ref[SUBLANES:SUBLANES + tm, :]
    act = prev * cw[CONV_WIDTH - 1:CONV_WIDTH]
    for s in range(1, CONV_WIDTH):
        act = act + hist_ref[pl.ds(SUBLANES - s, tm), :] * cw[CONV_WIDTH - 1 - s:CONV_WIDTH - s]
    qkv_ref[...] = act * _sigmoid(act)

    hb = _rms(x_ref[...], g_ref[...]).astype(BF16)
    hist_ref[0:SUBLANES, :] = prev[tm - SUBLANES:tm]
    hist_ref[SUBLANES:SUBLANES + tm, :] = _dot(hb, wqkv_ref[...])
    z_ref[...] = _dot(hb, wz_ref[...])
    u_ref[...] = _dot(hb, wu_ref[...])
    gates_ref[...] = _dot(hb, wg_ref[...])
    ba_ref[...] = _dot(hb, wba_ref[...])
    bat_ref[...] = _dot_nt(wbat_ref[...], hb)


def _in_proj(x2, g_mix, wqkv, wz, wu, wg, wba, wbat, conv_w, seq):
    t = x2.shape[0]
    tm = IN_TM
    n = t // tm
    full = lambda a: pl.BlockSpec(a.shape, lambda i: (0, 0))
    row = lambda c: pl.BlockSpec((tm, c), lambda i: (jnp.minimum(i, n - 1), 0))
    return pl.pallas_call(
        functools.partial(_in_proj_kernel, steps_per_seq=seq // tm),
        grid=(n + 1,),
        in_specs=[row(D_MODEL), full(g_mix), full(wqkv), full(wz), full(wu), full(wg), full(wba), full(wbat),
                  full(conv_w)],
        out_specs=[pl.BlockSpec((tm, 3 * DN_WIDTH), lambda i: (jnp.maximum(i - 1, 0), 0)),
                   row(DN_WIDTH), row(POOL_WIDTH), row(2 * D_MODEL), row(2 * LANES),
                   pl.BlockSpec((2 * SUBLANES, tm), lambda i: (0, jnp.minimum(i, n - 1)))],
        out_shape=[jax.ShapeDtypeStruct((t, 3 * DN_WIDTH), F32),
                   jax.ShapeDtypeStruct((t, DN_WIDTH), F32),
                   jax.ShapeDtypeStruct((t, POOL_WIDTH), F32),
                   jax.ShapeDtypeStruct((t, 2 * D_MODEL), F32),
                   jax.ShapeDtypeStruct((t, 2 * LANES), F32),
                   jax.ShapeDtypeStruct((2 * SUBLANES, t), F32)],
        scratch_shapes=[pltpu.VMEM((SUBLANES + tm, 3 * DN_WIDTH), F32)],
        compiler_params=pltpu.CompilerParams(dimension_semantics=("arbitrary",), vmem_limit_bytes=VMEM_LIMIT),
        name="in_proj",
    )(x2, g_mix, wqkv, wz, wu, wg, wba, wbat, conv_w)


def _deltanet_kernel(qkv_ref, z_ref, ba_ref, bat_ref, alog_r_ref, dtb_r_ref, alog_c_ref, dtb_c_ref, dnw_ref,
                     head_ones_ref, tril_ref, triu_ref, chunk_ones_ref, expand_ref, o_ref,
                     s_ref, qn_ref, kn_ref, kbe_ref, vb_ref, qg_ref, kd_ref,
                     xbeta_ref, xgc_ref, xgl_ref, gr_ref, oacc_ref, qkd_ref, wq_ref, u_ref, kdt_ref, gl_ref):
    lb = qkv_ref.shape[0]
    n_chunks = lb // CHUNK

    @pl.when(pl.program_id(1) == 0)
    def _():
        s_ref[...] = jnp.zeros_like(s_ref)

    q = qkv_ref[:, 0:DN_WIDTH]
    k = qkv_ref[:, DN_WIDTH:2 * DN_WIDTH]
    v = qkv_ref[:, 2 * DN_WIDTH:3 * DN_WIDTH]

    head_ones = head_ones_ref[...]
    qn = q * lax.rsqrt(_dot_exact_rhs(q * q, head_ones) + NORM_EPS) * (DN_HEAD_DIM ** -0.5)
    kn = k * lax.rsqrt(_dot_exact_rhs(k * k, head_ones) + NORM_EPS)

    ba = ba_ref[...]
    beta_c = _sigmoid(ba[:, 0:LANES])
    g_c = -jnp.exp(alog_r_ref[...]) * _softplus(ba[:, LANES:2 * LANES] + dtb_r_ref[...])
    lane = lax.broadcasted_iota(jnp.int32, (lb, LANES), 1)
    g_c = jnp.where(lane < DN_HEADS, g_c, 0.0)
    bat = bat_ref[...]
    g_r = -jnp.exp(alog_c_ref[...]) * _softplus(bat[SUBLANES:2 * SUBLANES] + dtb_c_ref[...])

    gc = _dot_exact_lhs(tril_ref[...], g_c)
    gtot = _dot_exact_lhs(chunk_ones_ref[...], g_c)
    gr = _dot_exact_rhs(g_r, triu_ref[...])
    for c in range(n_chunks):
        gr_ref[c] = gr[:, c * CHUNK:(c + 1) * CHUNK]

    expand = expand_ref[...]
    x_beta = _dot_exact_rhs(beta_c, expand)
    x_gc = _dot_exact_rhs(gc, expand)
    x_gtot = _dot_exact_rhs(gtot, expand)
    x_eg = jnp.exp(x_gc)
    xbeta_ref[...] = x_beta
    xgc_ref[...] = x_gc
    xgl_ref[...] = jnp.exp(x_gtot)
    qn_ref[...] = qn.astype(BF16)
    kn_ref[...] = kn.astype(BF16)
    kbe_ref[...] = (kn * (x_beta * x_eg)).astype(BF16)
    vb_ref[...] = (v * x_beta).astype(BF16)
    qg_ref[...] = (qn * x_eg).astype(BF16)
    kd_ref[...] = (kn * jnp.exp(x_gtot - x_gc)).astype(BF16)

    ci = lax.broadcasted_iota(jnp.int32, (CHUNK, CHUNK), 0)
    cj = lax.broadcasted_iota(jnp.int32, (CHUNK, CHUNK), 1)
    eye = (ci == cj).astype(F32)

    heads = range(DN_HEADS)
    lanes = [pl.ds(h * DN_HEAD_DIM, DN_HEAD_DIM) for h in heads]
    stack = lambda top, bot: jnp.concatenate([top, bot], axis=0)

    chunk_rows = lambda c: pl.ds(c * CHUNK, CHUNK)

    def within_chunks(chunk_ids):
        chains = [(c, h) for c in chunk_ids for h in heads]
        ids = range(len(chains))
        rows = [chunk_rows(c) for c, _ in chains]
        ln = [lanes[h] for _, h in chains]
        kb = [kn_ref[rows[i], ln[i]] for i in ids]
        kq = [_dot_nt(stack(kb[i], qn_ref[rows[i], ln[i]]), kb[i]) for i in ids]
        yield
        decay = [jnp.exp(jnp.where(ci >= cj, xgc_ref[rows[i], ln[i]] - gr_ref[c, h:h + 1, :], -jnp.inf))
                 for i, (c, h) in enumerate(chains)]
        a = [jnp.where(ci > cj, xbeta_ref[rows[i], ln[i]] * kq[i][0:CHUNK] * decay[i], 0.0) for i in ids]
        slot = [c * DN_HEADS + h for c, h in chains]
        for i in ids:
            qkd_ref[slot[i]] = (kq[i][CHUNK:2 * CHUNK] * decay[i]).astype(BF16)
            wq_ref[slot[i], CHUNK:2 * CHUNK, :] = qg_ref[rows[i], ln[i]]
            kdt_ref[slot[i]] = kd_ref[rows[i], ln[i]].T.astype(BF16)
            gl_ref[slot[i]] = xgl_ref[rows[i], ln[i]]
        t_inv = [eye - a[i] for i in ids]
        pw = [a[i].astype(BF16) for i in ids]
        pw = [_dot(pw[i], pw[i]).astype(BF16) for i in ids]
        yield
        for _ in range(4):
            r = [_dot(stack(t_inv[i].astype(BF16), pw[i]), pw[i]) for i in ids]
            yield
            t_inv = [t_inv[i] + r[i][0:CHUNK] for i in ids]
            pw = [r[i][CHUNK:2 * CHUNK].astype(BF16) for i in ids]
        t_inv = [t_inv[i] + _dot(t_inv[i].astype(BF16), pw[i]) for i in ids]
        yield
        tb = [t_inv[i].astype(BF16) for i in ids]
        for i in ids:
            wq_ref[slot[i], 0:CHUNK, :] = _dot(tb[i], kbe_ref[rows[i], ln[i]]).astype(BF16)
        yield
        for i in ids:
            u_ref[slot[i]] = _dot(tb[i], vb_ref[rows[i], ln[i]])
        yield

    def state_chunks(chunk_ids):
        for c in chunk_ids:
            rows = chunk_rows(c)
            slot = [c * DN_HEADS + h for h in heads]
            s = [s_ref[h] for h in heads]
            ws = [_dot(wq_ref[slot[h]], s[h].astype(BF16)) for h in heads]
            yield
            vnb = [(u_ref[slot[h]] - ws[h][0:CHUNK]).astype(BF16) for h in heads]
            for h in heads:
                oacc_ref[h, rows, :] = ws[h][CHUNK:2 * CHUNK] + _dot(qkd_ref[slot[h]], vnb[h])
            yield
            for h in heads:
                s_ref[h] = s[h] * gl_ref[slot[h]] + _dot(kdt_ref[slot[h]], vnb[h])
            yield

    def norm_chunks(chunk_ids):
        rows = pl.ds(chunk_ids[0] * CHUNK, len(chunk_ids) * CHUNK)
        o = jnp.concatenate([oacc_ref[h, rows, :] for h in heads], axis=1)
        hi, lo = _split2(o * o)
        ms = _dot(hi, head_ones)
        yield
        ms = (ms + _dot(lo, head_ones)) * (1.0 / DN_HEAD_DIM)
        yield
        z = z_ref[rows, :]
        o_ref[rows, :] = (o * lax.rsqrt(ms + NORM_EPS) * dnw_ref[...] * (z * _sigmoid(z))).astype(o_ref.dtype)

    groups = [list(range(g, min(g + INTRA_GROUP, n_chunks))) for g in range(0, n_chunks, INTRA_GROUP)]
    state, norm, done = iter(()), iter(()), None
    for grp in groups:
        for _ in within_chunks(grp):
            next(state, None)
            next(norm, None)
        for _ in state:
            pass
        for _ in norm:
            pass
        if done is not None:
            norm = norm_chunks(done)
        state, done = state_chunks(grp), grp
    for _ in state:
        next(norm, None)
    for _ in norm:
        pass
    for _ in norm_chunks(done):
        pass


def _deltanet_masks(lb):
    head = jnp.arange(DN_WIDTH) // DN_HEAD_DIM
    head_ones = head[:, None] == head[None, :]
    pos = jnp.arange(lb)
    same = (pos[:, None] // CHUNK) == (pos[None, :] // CHUNK)
    tril = same & (pos[None, :] <= pos[:, None])
    triu = same & (pos[:, None] <= pos[None, :])
    expand = jnp.arange(LANES)[:, None] == head[None, :]
    return tuple(m.astype(BF16) for m in (head_ones, tril, triu, same, expand))


def _deltanet(qkv, z, ba, bat, alog_r, dtb_r, alog_c, dtb_c, dnw, batch, seq):
    lb = DN_LB
    nb = seq // lb
    slots = lb // CHUNK * DN_HEADS
    masks = _deltanet_masks(lb)
    full = lambda a: pl.BlockSpec(a.shape, lambda b, j: (0,) * a.ndim)
    row = lambda n: pl.BlockSpec((lb, n), lambda b, j: (b * nb + j, 0))
    return pl.pallas_call(
        _deltanet_kernel,
        grid=(batch, nb),
        in_specs=[row(3 * DN_WIDTH), row(DN_WIDTH), row(2 * LANES),
                  pl.BlockSpec((2 * SUBLANES, lb), lambda b, j: (0, b * nb + j)),
                  full(alog_r), full(dtb_r), full(alog_c), full(dtb_c), full(dnw)] + [full(m) for m in masks],
        out_specs=row(DN_WIDTH),
        out_shape=jax.ShapeDtypeStruct((batch * seq, DN_WIDTH), BF16),
        scratch_shapes=[
            pltpu.VMEM((DN_HEADS, DN_HEAD_DIM, DN_HEAD_DIM), F32),
            pltpu.VMEM((lb, DN_WIDTH), BF16),
            pltpu.VMEM((lb, DN_WIDTH), BF16),
            pltpu.VMEM((lb, DN_WIDTH), BF16),
            pltpu.VMEM((lb, DN_WIDTH), BF16),
            pltpu.VMEM((lb, DN_WIDTH), BF16),
            pltpu.VMEM((lb, DN_WIDTH), BF16),
            pltpu.VMEM((lb, DN_WIDTH), F32),
            pltpu.VMEM((lb, DN_WIDTH), F32),
            pltpu.VMEM((lb, DN_WIDTH), F32),
            pltpu.VMEM((lb // CHUNK, DN_HEADS, CHUNK), F32),
            pltpu.VMEM((DN_HEADS, lb, DN_HEAD_DIM), F32),
            pltpu.VMEM((slots, CHUNK, CHUNK), BF16),
            pltpu.VMEM((slots, 2 * CHUNK, DN_HEAD_DIM), BF16),
            pltpu.VMEM((slots, CHUNK, DN_HEAD_DIM), F32),
            pltpu.VMEM((slots, DN_HEAD_DIM, CHUNK), BF16),
            pltpu.VMEM((slots, DN_HEAD_DIM, DN_HEAD_DIM), F32),
        ],
        compiler_params=pltpu.CompilerParams(dimension_semantics=("arbitrary", "arbitrary"),
                                             vmem_limit_bytes=VMEM_LIMIT),
        name="deltanet",
    )(qkv, z, ba, bat, alog_r, dtb_r, alog_c, dtb_c, dnw, *masks)


def _mix_kernel(x_ref, on_ref, u_ref, gates_ref, wupa_ref, poolw_ref, pscale_ref, wupb_ref, wout_ref,
                gffn_ref, wr_hi_ref, wr_lo_ref, br_ref,
                x1_ref, h2_ref, route_ref, rank_ref, expert_ref, counts_ref,
                ucarry_ref, ecount_ref, logits_ref, *, steps_per_seq):
    i = pl.program_id(0)
    tm = x_ref.shape[0]

    @pl.when(i == 0)
    def _():
        ecount_ref[...] = jnp.zeros_like(ecount_ref)
        logits_ref[...] = jnp.zeros_like(logits_ref)

    @pl.when(i % steps_per_seq == 0)
    def _():
        ucarry_ref[...] = jnp.zeros_like(ucarry_ref)

    lane = lax.broadcasted_iota(jnp.int32, (tm, LANES), 1)
    lg = jnp.where(lane < N_EXPERTS, logits_ref[...], -jnp.inf)
    vals, idxs, sels = [], [], []
    for _ in range(TOP_K):
        m = jnp.max(lg, axis=-1, keepdims=True)
        idx = jnp.min(jnp.where(lg == m, lane, LANES), axis=-1, keepdims=True)
        sel = lane == idx
        vals.append(m)
        idxs.append(idx)
        sels.append(sel)
        lg = jnp.where(sel, -jnp.inf, lg)
    es = [jnp.exp(vk - vals[0]) for vk in vals]
    denom = es[0] + es[1] + es[2] + es[3]
    chosen = jnp.zeros((tm, LANES), F32)
    for sel in sels:
        chosen = chosen + jnp.where(sel, 1.0, 0.0)
    chosen = jnp.where(i > 0, chosen, 0.0)

    u = u_ref[...]
    ext = jnp.concatenate([ucarry_ref[...], u], axis=0)
    ucarry_ref[...] = u[tm - POOL_HALO:tm]
    t_pos = (i % steps_per_seq) * tm + lax.broadcasted_iota(jnp.int32, (tm, POOL_GROUP_DIM), 0)
    ys = []
    for g in range(POOL_GROUPS):
        s = ext[:, g * POOL_GROUP_DIM:(g + 1) * POOL_GROUP_DIM]
        shift = 1
        while shift < POOL_WINDOWS[g]:
            s = s + pltpu.roll(s, shift, 0)
            shift *= 2
        count = jnp.minimum(t_pos + 1, POOL_WINDOWS[g]).astype(F32)
        pooled = s[POOL_HALO:] / count - u[:, g * POOL_GROUP_DIM:(g + 1) * POOL_GROUP_DIM]
        ys.append(_dot(pooled.astype(BF16), poolw_ref[g]))
    yb = jnp.concatenate(ys, axis=-1) * pscale_ref[...]
    y_b = _dot(yb.astype(BF16), wupb_ref[...])
    y_a = _dot(on_ref[...], wupa_ref[...])
    gates = gates_ref[...]
    merged = _sigmoid(gates[:, 0:D_MODEL]) * y_a + _sigmoid(gates[:, D_MODEL:2 * D_MODEL]) * y_b
    x1 = x_ref[...] + _dot(merged.astype(BF16), wout_ref[...])
    x1_ref[...] = x1

    h2 = _rms(x1, gffn_ref[...])
    _store_row_tiles(h2_ref, h2)

    hi, lo = _split2(h2)
    logits_ref[...] = (_dot(hi, wr_hi_ref[...]) + _dot(lo, wr_hi_ref[...]) + _dot(hi, wr_lo_ref[...])
                       + br_ref[...])

    ri = lax.broadcasted_iota(jnp.int32, (tm, tm), 0)
    rj = lax.broadcasted_iota(jnp.int32, (tm, tm), 1)
    before = (rj < ri).astype(BF16)
    pos = ecount_ref[...] + _dot(before, chosen.astype(BF16))
    ecount_ref[...] = ecount_ref[...] + jnp.sum(chosen, axis=0, keepdims=True)
    counts_ref[...] = ecount_ref[...]

    per_row = LANES // TOP_K
    first_lane = (lax.broadcasted_iota(jnp.int32, (tm, LANES), 0) % per_row) * TOP_K
    weights = jnp.zeros((tm, LANES), F32)
    flat_rank = jnp.zeros((tm, LANES), F32)
    flat_expert = jnp.zeros((tm, LANES), F32)
    for kk in range(TOP_K):
        rank = jnp.sum(jnp.where(sels[kk], pos, 0.0), axis=-1, keepdims=True)
        weights = jnp.where(lane == kk, es[kk] / denom, weights)
        flat_rank = jnp.where(lane == first_lane + kk, rank, flat_rank)
        flat_expert = jnp.where(lane == first_lane + kk, idxs[kk].astype(F32), flat_expert)
    route_ref[...] = weights
    fold = lambda a: jnp.sum(a.reshape(tm // per_row, per_row, LANES), axis=1).astype(jnp.int32)
    rank_ref[...] = fold(flat_rank)
    expert_ref[...] = fold(flat_expert)


def _mix(x2, on, u, gates, wupa, poolw, pscale, wupb, wout, gffn, wr_hi, wr_lo, br, batch, seq):
    tm = MIX_TM
    t = batch * seq
    n = t // tm
    full = lambda a: pl.BlockSpec(a.shape, lambda i: (0,) * a.ndim)
    row = lambda c: pl.BlockSpec((tm, c), lambda i: (jnp.minimum(i, n - 1), 0))
    return pl.pallas_call(
        functools.partial(_mix_kernel, steps_per_seq=seq // tm),
        grid=(n + 1,),
        in_specs=[row(D_MODEL), row(DN_WIDTH), row(POOL_WIDTH), row(2 * D_MODEL),
                  full(wupa), full(poolw), full(pscale), full(wupb), full(wout), full(gffn),
                  full(wr_hi), full(wr_lo), full(br)],
        out_specs=[pl.BlockSpec((tm, D_MODEL), lambda i: (i, 0)),
                   pl.BlockSpec((tm * ROW_SUB, LANES), lambda i: (i, 0)),
                   pl.BlockSpec((tm, LANES), lambda i: (jnp.maximum(i - 1, 0), 0)),
                   pl.BlockSpec((tm * TOP_K // LANES, LANES), lambda i: (jnp.maximum(i - 1, 0), 0)),
                   pl.BlockSpec((tm * TOP_K // LANES, LANES), lambda i: (jnp.maximum(i - 1, 0), 0)),
                   pl.BlockSpec((1, LANES), lambda i: (0, 0))],
        out_shape=[jax.ShapeDtypeStruct((t + tm, D_MODEL), F32),
                   jax.ShapeDtypeStruct(((t + tm) * ROW_SUB, LANES), F32),
                   jax.ShapeDtypeStruct((t, LANES), F32),
                   jax.ShapeDtypeStruct((t * TOP_K // LANES, LANES), jnp.int32),
                   jax.ShapeDtypeStruct((t * TOP_K // LANES, LANES), jnp.int32),
                   jax.ShapeDtypeStruct((1, LANES), F32)],
        scratch_shapes=[pltpu.VMEM((POOL_HALO, POOL_WIDTH), F32), pltpu.VMEM((1, LANES), F32),
                        pltpu.VMEM((tm, LANES), F32)],
        compiler_params=pltpu.CompilerParams(dimension_semantics=("arbitrary",), vmem_limit_bytes=VMEM_LIMIT),
        name="mix",
    )(x2, on, u, gates, wupa, poolw, pscale, wupb, wout, gffn, wr_hi, wr_lo, br)


def _routing_tables(rank, expert, counts, tg, n_tiles):
    cnt = counts[0, 0:N_EXPERTS].astype(jnp.int32)
    padded = (cnt + (tg - 1)) // tg * tg
    ends = jnp.cumsum(padded)
    offs = ends - padded
    experts = jnp.arange(N_EXPERTS, dtype=jnp.int32)
    dest = jnp.sum(jnp.where(expert[..., None] == experts, offs, 0), axis=-1) + rank
    n_used = ends[N_EXPERTS - 1] // tg
    tile_start = jnp.arange(n_tiles, dtype=jnp.int32) * tg
    tile_expert = jnp.sum((tile_start[:, None] >= ends[None, :]).astype(jnp.int32), axis=1)
    tile_expert = jnp.minimum(tile_expert, N_EXPERTS - 1)
    last = tile_expert[jnp.maximum(n_used - 1, 0)]
    tile_expert = jnp.where(jnp.arange(n_tiles) < n_used, tile_expert, last)
    pad_start = offs + cnt
    pad_len = padded - cnt
    later = (experts[None, :] > experts[:, None]) & (cnt[None, :] > 0)
    next_expert = jnp.min(jnp.where(later, experts[None, :], N_EXPERTS), axis=1)
    next_expert = jnp.where(next_expert == N_EXPERTS, experts, next_expert)
    return (dest, tile_expert.astype(jnp.int32), next_expert.astype(jnp.int32),
            n_used.reshape(1).astype(jnp.int32), pad_start.astype(jnp.int32), pad_len.astype(jnp.int32))


def _dispatch_kernel(pad_start_ref, pad_len_ref, nu_ref, dest_ref, h2_ref, xs_ref, zero_ref, sem, zsem):
    tm = h2_ref.shape[0] // ROW_SUB
    tg = zero_ref.shape[0]
    n_tiles = xs_ref.shape[0] // tg

    @pl.when(pl.program_id(0) == 0)
    def _():
        zero_ref[...] = jnp.zeros_like(zero_ref)

        def pad_runs(e, act):
            first, n = pad_start_ref[e], pad_len_ref[e]
            for bit in reversed(range((tg // ROW_SUB).bit_length() - 1)):
                rows = (1 << bit) * ROW_SUB

                @pl.when((n >> bit) & 1 == 1)
                def _():
                    start = pl.multiple_of((first + ((n >> (bit + 1)) << (bit + 1))) * ROW_SUB, ROW_SUB)
                    act(pltpu.make_async_copy(zero_ref.at[pl.ds(0, rows)], xs_ref.at[pl.ds(start, rows)], zsem))

        def tile_copy(i):
            return pltpu.make_async_copy(zero_ref, xs_ref.at[pl.ds(pl.multiple_of(i * tg, tg), tg)], zsem)

        lax.fori_loop(0, N_EXPERTS, lambda e, cc: (pad_runs(e, lambda cp: cp.start()), cc)[1], 0)
        lax.fori_loop(0, N_EXPERTS, lambda e, cc: (pad_runs(e, lambda cp: cp.wait()), cc)[1], 0)
        lax.fori_loop(nu_ref[0], n_tiles, lambda i, cc: (tile_copy(i).start(), cc)[1], 0)
        lax.fori_loop(nu_ref[0], n_tiles, lambda i, cc: (tile_copy(0).wait(), cc)[1], 0)

    def row_copy(r, d):
        return pltpu.make_async_copy(_row_tile(h2_ref, r), _row_tile(xs_ref, d), sem)

    def start(r, c):
        for kk in range(TOP_K):
            row_copy(r, dest_ref[0, 0, r * TOP_K + kk]).start(priority=kk % 2)
        return c

    lax.fori_loop(0, tm, start, 0, unroll=DMA_UNROLL)
    for kk in range(TOP_K):
        pltpu.make_async_copy(h2_ref, xs_ref.at[pl.ds(0, tm * ROW_SUB)], sem).wait()


def _dispatch(pad_start, pad_len, n_used, dest3, h2, n_rows):
    t = dest3.shape[0] * dest3.shape[2] // TOP_K
    tm = DSP_TM
    return pl.pallas_call(
        _dispatch_kernel,
        grid_spec=pltpu.PrefetchScalarGridSpec(
            num_scalar_prefetch=3,
            grid=(t // tm,),
            in_specs=[pl.BlockSpec((1, 1, tm * TOP_K), lambda i, ps, pn, nu: (i, 0, 0), memory_space=pltpu.SMEM),
                      pl.BlockSpec((tm * ROW_SUB, LANES), lambda i, ps, pn, nu: (i, 0))],
            out_specs=pl.BlockSpec(memory_space=pl.ANY),
            scratch_shapes=[pltpu.VMEM((MOE_TG * ROW_SUB, LANES), F32), pltpu.SemaphoreType.DMA,
                            pltpu.SemaphoreType.DMA]),
        out_shape=jax.ShapeDtypeStruct((n_rows * ROW_SUB, LANES), F32),
        compiler_params=pltpu.CompilerParams(dimension_semantics=("arbitrary",), vmem_limit_bytes=VMEM_LIMIT),
        name="dispatch",
    )(pad_start, pad_len, n_used, dest3, h2)


def _experts_kernel(te_ref, nx_ref, nu_ref, xs_ref, wgu_hbm_ref, bgu_ref, wd_hbm_ref, bd_ref, ys_ref,
                    wgu_f32_ref, wd_f32_ref, wgu_bf_ref, wd_bf_ref, sem):
    i = pl.program_id(0)
    e = te_ref[i]

    def weight_copies(ex):
        return (pltpu.make_async_copy(wgu_hbm_ref.at[ex], wgu_f32_ref, sem.at[0]),
                pltpu.make_async_copy(wd_hbm_ref.at[ex], wd_f32_ref, sem.at[1]))

    @pl.when(i == 0)
    def _():
        for cp in weight_copies(e):
            cp.start()

    @pl.when((i == 0) | (e != te_ref[jnp.maximum(i - 1, 0)]))
    def _():
        for cp in weight_copies(e):
            cp.wait()
        wgu_bf_ref[...] = wgu_f32_ref[...].astype(BF16)
        wd_bf_ref[...] = wd_f32_ref[...].astype(BF16)

        @pl.when(nx_ref[e] != e)
        def _():
            for cp in weight_copies(nx_ref[e]):
                cp.start()

    @pl.when(i < nu_ref[0])
    def _():
        gu = _dot(_load_row_tiles(xs_ref).astype(BF16), wgu_bf_ref[...]) + bgu_ref[...]
        gate = jnp.minimum(gu[:, 0:D_FF], SWIGLU_LIMIT)
        up = jnp.clip(gu[:, D_FF:2 * D_FF], -SWIGLU_LIMIT, SWIGLU_LIMIT)
        act = gate * _sigmoid(SWIGLU_ALPHA * gate) * (up + 1.0)
        _store_row_tiles(ys_ref, _dot(act.astype(BF16), wd_bf_ref[...]) + bd_ref[...])

    @pl.when(i >= nu_ref[0])
    def _():
        ys_ref[...] = jnp.zeros_like(ys_ref)


def _experts(tile_expert, next_expert, n_used, xs, wgu, bgu, wd, bd):
    tg = MOE_TG * ROW_SUB
    n_tiles = xs.shape[0] // tg
    tile = lambda i, te, nx, nu: (jnp.minimum(i, nu[0] - 1), 0)
    expert = lambda i, te, nx, nu: (te[i], 0, 0)
    return pl.pallas_call(
        _experts_kernel,
        grid_spec=pltpu.PrefetchScalarGridSpec(
            num_scalar_prefetch=3,
            grid=(n_tiles,),
            in_specs=[pl.BlockSpec((tg, LANES), tile),
                      pl.BlockSpec(memory_space=pl.ANY),
                      pl.BlockSpec((None, 1, 2 * D_FF), expert),
                      pl.BlockSpec(memory_space=pl.ANY),
                      pl.BlockSpec((None, 1, D_MODEL), expert)],
            out_specs=pl.BlockSpec((tg, LANES), lambda i, te, nx, nu: (i, 0)),
            scratch_shapes=[pltpu.VMEM((D_MODEL, 2 * D_FF), F32), pltpu.VMEM((D_FF, D_MODEL), F32),
                            pltpu.VMEM((D_MODEL, 2 * D_FF), BF16), pltpu.VMEM((D_FF, D_MODEL), BF16),
                            pltpu.SemaphoreType.DMA((2,))]),
        out_shape=jax.ShapeDtypeStruct(xs.shape, F32),
        compiler_params=pltpu.CompilerParams(dimension_semantics=("arbitrary",), vmem_limit_bytes=VMEM_LIMIT),
        name="experts",
    )(tile_expert, next_expert, n_used, xs, wgu, bgu, wd, bd)


def _combine_kernel(dest_ref, dest_next_ref, route_ref, x1_ref, gfin_ref, ys_ref, out_ref, ybuf_ref, sem):
    i = pl.program_id(0)
    tm = x1_ref.shape[0]

    def gather_rows(dref, buf):
        def start(r, c):
            for kk in range(TOP_K):
                pltpu.make_async_copy(_row_tile(ys_ref, dref[0, 0, r * TOP_K + kk]),
                                      _row_tile(ybuf_ref.at[buf, kk], r), sem.at[buf]).start(priority=kk % 2)
            return c
        lax.fori_loop(0, tm, start, 0, unroll=DMA_UNROLL)

    @pl.when(i == 0)
    def _():
        gather_rows(dest_ref, 0)

    @pl.when(i + 1 < pl.num_programs(0))
    def _():
        gather_rows(dest_next_ref, (i + 1) % 2)

    buf = i % 2
    for kk in range(TOP_K):
        pltpu.make_async_copy(ys_ref.at[pl.ds(0, tm * ROW_SUB)], ybuf_ref.at[buf, kk], sem.at[buf]).wait()
    route = route_ref[...]
    acc = x1_ref[...]
    for kk in range(TOP_K):
        acc = acc + route[:, kk:kk + 1] * _load_row_tiles(ybuf_ref.at[buf, kk])
    out_ref[...] = _rms(acc, gfin_ref[...])


def _combine(dest3, route, x1, gfin, ys):
    t = route.shape[0]
    tm = CMB_TM
    last = t // tm - 1
    return pl.pallas_call(
        _combine_kernel,
        grid=(t // tm,),
        in_specs=[pl.BlockSpec((1, 1, tm * TOP_K), lambda i: (i, 0, 0), memory_space=pltpu.SMEM),
                  pl.BlockSpec((1, 1, tm * TOP_K), lambda i: (jnp.minimum(i + 1, last), 0, 0),
                               memory_space=pltpu.SMEM),
                  pl.BlockSpec((tm, LANES), lambda i: (i, 0)),
                  pl.BlockSpec((tm, D_MODEL), lambda i: (i, 0)),
                  pl.BlockSpec((1, D_MODEL), lambda i: (0, 0)),
                  pl.BlockSpec(memory_space=pl.ANY)],
        out_specs=pl.BlockSpec((tm, D_MODEL), lambda i: (i, 0)),
        out_shape=jax.ShapeDtypeStruct((t, D_MODEL), F32),
        scratch_shapes=[pltpu.VMEM((2, TOP_K, tm * ROW_SUB, LANES), F32), pltpu.SemaphoreType.DMA((2,))],
        compiler_params=pltpu.CompilerParams(dimension_semantics=("arbitrary",), vmem_limit_bytes=VMEM_LIMIT),
        name="combine",
    )(dest3, dest3, route, x1, gfin, ys)


def kernel(x, g_mix, w_in, conv_w, a_log, dt_bias, dn_norm, w_up_a, pool_w, pool_scale, w_up_b, w_out, g_ffn,
           w_router, b_router, w_gate_up, b_gate_up, w_down, b_down, g_final):
    batch, seq, d = x.shape
    assert d == D_MODEL and seq % DN_LB == 0 and seq % MIX_TM == 0
    assert (batch * seq) % DSP_TM == 0 and (batch * seq * TOP_K) % MOE_TG == 0
    assert g_mix.shape[0] == 1, "one layer"
    t = batch * seq
    x2 = x.reshape(t, d)

    w = w_in[0]
    o_z = 3 * DN_WIDTH
    o_b = o_z + DN_WIDTH
    o_a = o_b + DN_HEADS
    o_u = o_a + DN_HEADS
    o_g = o_u + POOL_WIDTH
    wqkv = w[:, 0:o_z].astype(BF16)
    wz = w[:, o_z:o_b].astype(BF16)
    wu = w[:, o_u:o_g].astype(BF16)
    wg = w[:, o_g:].astype(BF16)
    w_b = w[:, o_b:o_a]
    w_a = w[:, o_a:o_u]
    pad = jnp.zeros((d, LANES - DN_HEADS), F32)
    wba = jnp.concatenate([w_b, pad, w_a, pad], axis=1).astype(BF16)
    wbat = jnp.concatenate([w_b, w_a], axis=1).T.astype(BF16)

    qkv, z, u, gates, ba, bat = _in_proj(x2, g_mix, wqkv, wz, wu, wg, wba, wbat, conv_w[0], seq)

    lane_pad = lambda p: jnp.pad(p.reshape(1, DN_HEADS), ((0, 0), (0, LANES - DN_HEADS)))
    on = _deltanet(qkv, z, ba, bat, lane_pad(a_log[0]), lane_pad(dt_bias[0]),
                   a_log[0].reshape(DN_HEADS, 1), dt_bias[0].reshape(DN_HEADS, 1),
                   jnp.tile(dn_norm[0], DN_HEADS).reshape(1, DN_WIDTH), batch, seq)

    wr = jnp.pad(w_router[0], ((0, 0), (0, LANES - N_EXPERTS)))
    wr_hi = wr.astype(BF16)
    wr_lo = (wr - wr_hi.astype(F32)).astype(BF16)
    br = jnp.pad(b_router[0].reshape(1, N_EXPERTS), ((0, 0), (0, LANES - N_EXPERTS)))
    x1, h2, route, rank, expert, counts = _mix(x2, on, u, gates, w_up_a[0].astype(BF16), pool_w[0].astype(BF16),
                                                pool_scale[0].reshape(1, POOL_WIDTH), w_up_b[0].astype(BF16),
                                                w_out[0].astype(BF16), g_ffn, wr_hi, wr_lo, br, batch, seq)

    n_tiles = t * TOP_K // MOE_TG + N_EXPERTS
    dest, tile_expert, next_expert, n_used, pad_start, pad_len = _routing_tables(rank, expert, counts, MOE_TG,
                                                                                 n_tiles)
    xs = _dispatch(pad_start, pad_len, n_used, dest.reshape(t // DSP_TM, 1, DSP_TM * TOP_K), h2, n_tiles * MOE_TG)
    ys = _experts(tile_expert, next_expert, n_used, xs, w_gate_up[0],
                  b_gate_up[0].reshape(N_EXPERTS, 1, 2 * D_FF), w_down[0], b_down[0].reshape(N_EXPERTS, 1, D_MODEL))
    out = _combine(dest.reshape(t // CMB_TM, 1, CMB_TM * TOP_K), route, x1, g_final.reshape(1, D_MODEL), ys)
    return out.reshape(batch, seq, d)
```
